```python
import jax, jax.numpy as jnp
from jax import lax
import numpy as np

D_MODEL = 1024
BATCH = 8
SEQ = 2048
DEPTH = 2

RET_HEADS = 4
RET_DK = 128
RET_DV = 128
RET_CHUNK = 128
ROPE_BASE = 10000.0
GDN_HEADS = 4
GDN_DK = 128
GDN_DV = 128
GDN_CHUNK = 64
SHORT_CONV = 4
RET_WIDTH = RET_HEADS * RET_DV
GDN_WIDTH = GDN_HEADS * GDN_DV
MIX0_OUT = RET_WIDTH + GDN_WIDTH
MIX0_SPLITS = (RET_HEADS * RET_DK, RET_HEADS * RET_DK, RET_WIDTH, RET_WIDTH,
               GDN_HEADS * GDN_DK, GDN_HEADS * GDN_DK, GDN_WIDTH, GDN_WIDTH,
               GDN_HEADS, GDN_HEADS)
MIX0_IN = sum(MIX0_SPLITS)
D_RNN = D_MODEL
LRU_BLOCKS = 8
LRU_BLOCK = D_RNN // LRU_BLOCKS
LRU_CONV = 4
LRU_C = 8.0
D_FF = ((8 * D_MODEL // 3 + 127) // 128) * 128
FFN_CONV = 3
EPS = 1e-6
N_EVEN = (DEPTH + 1) // 2
N_ODD = DEPTH // 2

kernel_name = "hybrid_retention_gdn_rglru_convffn"


def _rms(x):
    xf = x.astype(jnp.float32)
    return xf * lax.rsqrt(jnp.mean(xf * xf, axis=-1, keepdims=True) + EPS)


def rms_norm(x, gain):
    return (_rms(x) * gain.astype(jnp.float32)).astype(x.dtype)


def l2norm(x):
    return x * lax.rsqrt(jnp.sum(x * x, axis=-1, keepdims=True) + EPS)


def causal_dwconv(x, w):
    width, ch = w.shape
    return lax.conv_general_dilated(
        x, w[:, None, :].astype(x.dtype), window_strides=(1,),
        padding=[(width - 1, 0)], dimension_numbers=('NWC', 'WIO', 'NWC'),
        feature_group_count=ch)


def rotary(x, pos):
    half = x.shape[-1] // 2
    inv_freq = ROPE_BASE ** (-jnp.arange(half, dtype=jnp.float32) / half)
    ang = pos.astype(jnp.float32)[:, None] * inv_freq[None, :]
    cos = jnp.cos(ang)[None, :, None, :]
    sin = jnp.sin(ang)[None, :, None, :]
    x1, x2 = x[..., :half], x[..., half:]
    return jnp.concatenate([x1 * cos - x2 * sin, x1 * sin + x2 * cos], axis=-1)


def to_chunks(x, c):
    b, t, h, d = x.shape
    return x.reshape(b, t // c, c, h, d).transpose(0, 3, 1, 2, 4)


def from_chunks(x):
    b, h, n, c, d = x.shape
    return x.transpose(0, 2, 3, 1, 4).reshape(b, n * c, h, d)


def retention_chunkwise(q, k, v):
    b, t, h, dk = q.shape
    dv = v.shape[-1]
    c = RET_CHUNK
    log_gamma = jnp.log1p(-jnp.exp2(-5.0 - jnp.arange(h, dtype=jnp.float32)))
    qc, kc, vc = to_chunks(q, c), to_chunks(k * dk ** -0.5, c), to_chunks(v, c)
    idx = jnp.arange(c, dtype=jnp.float32)
    rel = idx[:, None] - idx[None, :]
    causal = rel >= 0
    dmask = jnp.where(causal, jnp.exp(log_gamma[:, None, None] * jnp.where(causal, rel, 0.0)), 0.0)
    scores = jnp.einsum('bhnid,bhnjd->bhnij', qc, kc) * dmask[:, None]
    intra = jnp.einsum('bhnij,bhnjv->bhniv', scores, vc)
    k_tail = kc * jnp.exp(log_gamma[:, None, None] * (c - 1 - idx))[..., None]
    kv = jnp.einsum('bhncd,bhncv->nbhdv', k_tail, vc)
    chunk_decay = jnp.exp(log_gamma * c)[None, :, None, None]

    def step(s, kv_n):
        return s * chunk_decay + kv_n, s

    _, s_prev = lax.scan(step, jnp.zeros((b, h, dk, dv), jnp.float32), kv)
    q_decay = qc * jnp.exp(log_gamma[:, None, None] * (idx + 1.0))[..., None]
    inter = jnp.einsum('bhncd,nbhdv->bhncv', q_decay, s_prev)
    return from_chunks(intra + inter)


def gated_delta_chunkwise(q, k, v, g, beta):
    b, t, h, dk = q.shape
    dv = v.shape[-1]
    c = GDN_CHUNK
    qc = to_chunks(q * dk ** -0.5, c)
    kc = to_chunks(k, c)
    vc = to_chunks(v, c)
    gc = jnp.cumsum(to_chunks(g[..., None], c)[..., 0], axis=-1)
    bc = to_chunks(beta[..., None], c)
    tril = jnp.tril(jnp.ones((c, c), bool))
    strict = jnp.tril(jnp.ones((c, c), bool), -1)
    decay = jnp.exp(jnp.where(tril, gc[..., :, None] - gc[..., None, :], -jnp.inf))
    k_beta = kc * bc
    lmat = jnp.where(strict, jnp.einsum('bhnid,bhnjd->bhnij', k_beta, kc) * decay, 0.0)
    rhs = jnp.concatenate([vc * bc, k_beta * jnp.exp(gc)[..., None]], axis=-1)
    sol = lax.linalg.triangular_solve(lmat + jnp.eye(c, dtype=jnp.float32), rhs,
                                      left_side=True, lower=True)
    u, w = sol[..., :dv], sol[..., dv:]
    attn = jnp.where(tril, jnp.einsum('bhnid,bhnjd->bhnij', qc, kc) * decay, 0.0)
    q_decay = qc * jnp.exp(gc)[..., None]
    g_last = gc[..., -1:]
    k_tail = kc * jnp.exp(g_last - gc)[..., None]
    chunk_decay = jnp.exp(g_last)[..., None]
    xs = tuple(jnp.moveaxis(a, 2, 0) for a in (u, w, attn, q_decay, k_tail, chunk_decay))

    def step(s, inp):
        u_n, w_n, a_n, qd_n, kt_n, cd_n = inp
        v_new = u_n - jnp.einsum('bhcd,bhdv->bhcv', w_n, s)
        o = jnp.einsum('bhcd,bhdv->bhcv', qd_n, s) + jnp.einsum('bhij,bhjv->bhiv', a_n, v_new)
        s = s * cd_n + jnp.einsum('bhcd,bhcv->bhdv', kt_n, v_new)
        return s, o

    _, o = lax.scan(step, jnp.zeros((b, h, dk, dv), jnp.float32), xs)
    return from_chunks(jnp.moveaxis(o, 0, 2))


def retention_deltanet_mixer(hn, pos, w_in, conv_w, a_log, dt_bias, out_gain, w_out):
    b, t, _ = hn.shape
    f32 = jnp.float32
    split_at = np.cumsum(MIX0_SPLITS)[:-1].tolist()
    q_r, k_r, v_r, g_r, q_d, k_d, v_d, g_d, b_d, a_d = jnp.split(hn @ w_in, split_at, axis=-1)
    q_r = rotary(q_r.reshape(b, t, RET_HEADS, RET_DK).astype(f32), pos)
    k_r = rotary(k_r.reshape(b, t, RET_HEADS, RET_DK).astype(f32), pos)
    v_r = v_r.reshape(b, t, RET_HEADS, RET_DV).astype(f32)
    y_r = _rms(retention_chunkwise(q_r, k_r, v_r)).reshape(b, t, RET_WIDTH)
    y_r = y_r * jax.nn.silu(g_r.astype(f32))
    qkv = jax.nn.silu(causal_dwconv(jnp.concatenate([q_d, k_d, v_d], axis=-1), conv_w)).astype(f32)
    q_d, k_d, v_d = jnp.split(qkv, [GDN_HEADS * GDN_DK, 2 * GDN_HEADS * GDN_DK], axis=-1)
    q_d = l2norm(q_d.reshape(b, t, GDN_HEADS, GDN_DK))
    k_d = l2norm(k_d.reshape(b, t, GDN_HEADS, GDN_DK))
    v_d = v_d.reshape(b, t, GDN_HEADS, GDN_DV)
    beta = jax.nn.sigmoid(b_d.astype(f32))
    g = -jnp.exp(a_log.astype(f32)) * jax.nn.softplus(a_d.astype(f32) + dt_bias.astype(f32))
    y_d = gated_delta_chunkwise(q_d, k_d, v_d, g, beta)
    y_d = rms_norm(y_d, out_gain) * jax.nn.silu(g_d.astype(f32).reshape(b, t, GDN_HEADS, GDN_DV))
    y = jnp.concatenate([y_r, y_d.reshape(b, t, GDN_WIDTH)], axis=-1).astype(hn.dtype)
    return y @ w_out


def _linear_combine(e1, e2):
    a1, b1 = e1
    a2, b2 = e2
    return a1 * a2, a2 * b1 + b2


def rglru_mixer(hn, w_in, conv_w, conv_b, w_a, b_a, w_x, b_x, lam, w_out):
    b, t, _ = hn.shape
    f32 = jnp.float32
    gate, xr = jnp.split(hn @ w_in, 2, axis=-1)
    xr = (causal_dwconv(xr, conv_w) + conv_b).astype(f32)
    xb = xr.reshape(b, t, LRU_BLOCKS, LRU_BLOCK)
    r = jax.nn.sigmoid(jnp.einsum('btni,nij->btnj', xb, w_a.astype(f32)).reshape(b, t, D_RNN) + b_a)
    i = jax.nn.sigmoid(jnp.einsum('btni,nij->btnj', xb, w_x.astype(f32)).reshape(b, t, D_RNN) + b_x)
    log_a = -LRU_C * r * jax.nn.softplus(-lam.astype(f32))
    a = jnp.exp(log_a)
    u = jnp.sqrt(-jnp.expm1(2.0 * log_a)) * (i * xr)
    _, hs = lax.associative_scan(_linear_combine, (a, u), axis=1)
    y = jax.nn.gelu(gate.astype(f32)) * hs
    return y.astype(hn.dtype) @ w_out


def conv_ffn(hn, w_up, conv_w, conv_b, w_down):
    up = causal_dwconv(hn @ w_up, conv_w) + conv_b
    gate, val = jnp.split(up, 2, axis=-1)
    return (jax.nn.silu(gate) * val) @ w_down


def setup_inputs(seed: int = 0) -> dict:
    key = jax.random.key(seed)
    ks = jax.random.split(key, 24)
    nrm = jax.random.normal
    f32 = jnp.float32
    x = nrm(ks[0], (BATCH, SEQ, D_MODEL), f32)
    norm_mix = 1.0 + 0.02 * nrm(ks[1], (DEPTH, D_MODEL), f32)
    norm_ffn = 1.0 + 0.02 * nrm(ks[2], (DEPTH, D_MODEL), f32)
    ret_gdn_w_in = nrm(ks[3], (N_EVEN, D_MODEL, MIX0_IN), f32) * D_MODEL ** -0.5
    gdn_conv_w = nrm(ks[4], (N_EVEN, SHORT_CONV, 2 * GDN_HEADS * GDN_DK + GDN_WIDTH), f32) * SHORT_CONV ** -0.5
    gdn_a_log = jnp.log(jax.random.uniform(ks[5], (N_EVEN, GDN_HEADS), f32, 1.0, 16.0))
    dt = jnp.exp(jax.random.uniform(ks[6], (N_EVEN, GDN_HEADS), f32, np.log(1e-3), np.log(1e-1)))
    gdn_dt_bias = dt + jnp.log(-jnp.expm1(-dt))
    gdn_out_gain = 1.0 + 0.02 * nrm(ks[7], (N_EVEN, GDN_DV), f32)
    ret_gdn_w_out = nrm(ks[8], (N_EVEN, MIX0_OUT, D_MODEL), f32) * MIX0_OUT ** -0.5
    lru_w_in = nrm(ks[9], (N_ODD, D_MODEL, 2 * D_RNN), f32) * D_MODEL ** -0.5
    lru_conv_w = nrm(ks[10], (N_ODD, LRU_CONV, D_RNN), f32) * LRU_CONV ** -0.5
    lru_conv_b = 0.01 * nrm(ks[11], (N_ODD, D_RNN), f32)
    lru_w_a = nrm(ks[12], (N_ODD, LRU_BLOCKS, LRU_BLOCK, LRU_BLOCK), f32) * LRU_BLOCK ** -0.5
    lru_b_a = 0.01 * nrm(ks[13], (N_ODD, D_RNN), f32)
    lru_w_x = nrm(ks[14], (N_ODD, LRU_BLOCKS, LRU_BLOCK, LRU_BLOCK), f32) * LRU_BLOCK ** -0.5
    lru_b_x = 0.01 * nrm(ks[15], (N_ODD, D_RNN), f32)
    a_c = jax.random.uniform(ks[16], (N_ODD, D_RNN), f32, 0.9, 0.999)
    a0 = a_c ** (1.0 / LRU_C)
    lru_lambda = jnp.log(a0) - jnp.log1p(-a0)
    lru_w_out = nrm(ks[17], (N_ODD, D_RNN, D_MODEL), f32) * D_RNN ** -0.5
    ffn_w_up = nrm(ks[18], (DEPTH, D_MODEL, 2 * D_FF), f32) * D_MODEL ** -0.5
    ffn_conv_w = nrm(ks[19], (DEPTH, FFN_CONV, 2 * D_FF), f32) * FFN_CONV ** -0.5
    ffn_conv_b = 0.01 * nrm(ks[20], (DEPTH, 2 * D_FF), f32)
    ffn_w_down = nrm(ks[21], (DEPTH, D_FF, D_MODEL), f32) * D_FF ** -0.5
    norm_final = 1.0 + 0.02 * nrm(ks[22], (D_MODEL,), f32)
    return {"x": x, "norm_mix": norm_mix, "norm_ffn": norm_ffn,
            "ret_gdn_w_in": ret_gdn_w_in, "gdn_conv_w": gdn_conv_w, "gdn_a_log": gdn_a_log,
            "gdn_dt_bias": gdn_dt_bias, "gdn_out_gain": gdn_out_gain, "ret_gdn_w_out": ret_gdn_w_out,
            "lru_w_in": lru_w_in, "lru_conv_w": lru_conv_w, "lru_conv_b": lru_conv_b,
            "lru_w_a": lru_w_a, "lru_b_a": lru_b_a, "lru_w_x": lru_w_x, "lru_b_x": lru_b_x,
            "lru_lambda": lru_lambda, "lru_w_out": lru_w_out,
            "ffn_w_up": ffn_w_up, "ffn_conv_w": ffn_conv_w, "ffn_conv_b": ffn_conv_b,
            "ffn_w_down": ffn_w_down, "norm_final": norm_final}


def reference(x, norm_mix, norm_ffn, ret_gdn_w_in, gdn_conv_w, gdn_a_log, gdn_dt_bias,
              gdn_out_gain, ret_gdn_w_out, lru_w_in, lru_conv_w, lru_conv_b, lru_w_a, lru_b_a,
              lru_w_x, lru_b_x, lru_lambda, lru_w_out, ffn_w_up, ffn_conv_w, ffn_conv_b,
              ffn_w_down, norm_final):
    pos = jnp.arange(x.shape[1], dtype=jnp.int32)
    h = x
    for layer in range(DEPTH):
        hn = rms_norm(h, norm_mix[layer])
        if layer % 2 == 0:
            e = layer // 2
            h = h + retention_deltanet_mixer(hn, pos, ret_gdn_w_in[e], gdn_conv_w[e], gdn_a_log[e],
                                             gdn_dt_bias[e], gdn_out_gain[e], ret_gdn_w_out[e])
        else:
            o = layer // 2
            h = h + rglru_mixer(hn, lru_w_in[o], lru_conv_w[o], lru_conv_b[o], lru_w_a[o], lru_b_a[o],
                                lru_w_x[o], lru_b_x[o], lru_lambda[o], lru_w_out[o])
        h = h + conv_ffn(rms_norm(h, norm_ffn[layer]), ffn_w_up[layer], ffn_conv_w[layer],
                         ffn_conv_b[layer], ffn_w_down[layer])
    return rms_norm(h, norm_final)
```

```python
import functools
import math

import numpy as np
import jax
import jax.numpy as jnp
from jax import lax
from jax.experimental import pallas as pl
from jax.experimental.pallas import tpu as pltpu

F32 = jnp.float32
BF16 = jnp.bfloat16

EPS = 1e-6
ROPE_BASE = 10000.0
HEADS = 4
HEAD_DIM = 128
CHUNK = 128
GDN_CONV = 4
LRU_CONV = 4
FFN_CONV = 3
LRU_C = 8.0
LRU_BLOCK = 128
SUBLANES = 8
LANES = 128
NEG_BIG = -1e30
VMEM_LIMIT = 56 * 1024 * 1024

RET_LOG_GAMMA = tuple(math.log1p(-2.0 ** (-5.0 - h)) for h in range(HEADS))


def _rms(x):
    return x * lax.rsqrt(jnp.mean(x * x, axis=-1, keepdims=True) + EPS)


def _sigmoid(x):
    return 1.0 / (1.0 + jnp.exp(-x))


def _silu(x):
    return x * _sigmoid(x)


def _softplus(x):
    return jnp.maximum(x, 0.0) + jnp.log1p(jnp.exp(-jnp.abs(x)))


def _gelu_tanh(x):
    c = math.sqrt(2.0 / math.pi)
    return 0.5 * x * (1.0 + jnp.tanh(c * (x + 0.044715 * (x * x * x))))


def _dot(a, b):
    return jnp.dot(a, b, preferred_element_type=F32)


def _dot_nt(a, b):
    return lax.dot_general(a, b, (((1,), (1,)), ((), ())), preferred_element_type=F32)


def _mm(a, b):
    return _dot(a.astype(BF16), b.astype(BF16))


def _split3(x):
    hi = x.astype(BF16)
    r1 = x - hi.astype(F32)
    mid = r1.astype(BF16)
    lo = (r1 - mid.astype(F32)).astype(BF16)
    return hi, mid, lo


def _inproj_kernel(x_ref, gain_ref, wmain_ref, wsmall_ref, cos_ref, sin_ref, convw_ref,
                   alog_ref, dtb_ref, ret_ref, gdn_ref, small_ref, cbuf_ref):
    ti = pl.program_id(1)
    tile = x_ref.shape[1]
    width = HEADS * HEAD_DIM
    hb = (_rms(x_ref[0]) * gain_ref[...]).astype(BF16)
    cos = cos_ref[...]
    sin = sin_ref[...]

    for part in range(2):
        p = _dot(hb, wmain_ref[:, part * width:(part + 1) * width])
        for h in range(HEADS):
            ph = p[:, h * HEAD_DIM:(h + 1) * HEAD_DIM]
            rot = ph * cos + pltpu.roll(ph, HEAD_DIM // 2, 1) * sin
            if part == 1:
                rot = rot * (HEAD_DIM ** -0.5)
            c0 = part * width + h * HEAD_DIM
            ret_ref[0, :, c0:c0 + HEAD_DIM] = rot.astype(BF16)
    ret_ref[0, :, 2 * width:4 * width] = _dot(hb, wmain_ref[:, 2 * width:4 * width]).astype(BF16)

    pre = _dot(hb, wmain_ref[:, 4 * width:7 * width])

    @pl.when(ti == 0)
    def _():
        cbuf_ref[0:SUBLANES, :] = jnp.zeros((SUBLANES, 3 * width), F32)

    cbuf_ref[SUBLANES:SUBLANES + tile, :] = pre
    cw = convw_ref[...]
    acc = pre * cw[GDN_CONV - 1:GDN_CONV, :]
    for j in range(GDN_CONV - 1):
        off = SUBLANES - (GDN_CONV - 1) + j
        acc = acc + cbuf_ref[off:off + tile, :] * cw[j:j + 1, :]
    cbuf_ref[0:SUBLANES, :] = cbuf_ref[tile:tile + SUBLANES, :]
    act = _silu(acc)
    for part in range(2):
        for h in range(HEADS):
            c0 = part * width + h * HEAD_DIM
            ah = act[:, c0:c0 + HEAD_DIM]
            nrm = ah * lax.rsqrt(jnp.sum(ah * ah, axis=-1, keepdims=True) + EPS)
            if part == 0:
                nrm = nrm * (HEAD_DIM ** -0.5)
            gdn_ref[0, :, c0:c0 + HEAD_DIM] = nrm.astype(BF16)
    gdn_ref[0, :, 2 * width:3 * width] = act[:, 2 * width:3 * width].astype(BF16)
    gdn_ref[0, :, 3 * width:4 * width] = _dot(hb, wmain_ref[:, 7 * width:8 * width]).astype(BF16)

    sm = _dot(hb, wsmall_ref[...])
    col = lax.broadcasted_iota(jnp.int32, sm.shape, 1)
    beta = _sigmoid(sm)
    g = -jnp.exp(alog_ref[...]) * _softplus(sm + dtb_ref[...])
    small_ref[0] = jnp.where(col < HEADS, beta, g)


def _mixer_kernel(ret_ref, gdn_ref, small_ref, x_ref, wout_ref, ogain_ref, out_ref,
                  sret_ref, sgdn_ref, ybuf_ref):
    ti = pl.program_id(1)
    tile = x_ref.shape[1]
    width = HEADS * HEAD_DIM
    c = CHUNK

    @pl.when(ti == 0)
    def _():
        sret_ref[...] = jnp.zeros(sret_ref.shape, F32)
        sgdn_ref[...] = jnp.zeros(sgdn_ref.shape, F32)

    row = lax.broadcasted_iota(jnp.int32, (c, c), 0)
    col = lax.broadcasted_iota(jnp.int32, (c, c), 1)
    tril = row >= col
    strict = row > col
    rowf = row.astype(F32)
    relf = (row - col).astype(F32)
    eye = jnp.where(row == col, 1.0, 0.0).astype(F32)
    tri_b = jnp.where(tril, 1.0, 0.0).astype(BF16)
    ogain = ogain_ref[...]

    for blk in range(tile // c):
        r0 = blk * c
        for h in range(HEADS):
            lg = RET_LOG_GAMMA[h]
            q = ret_ref[0, r0:r0 + c, h * HEAD_DIM:(h + 1) * HEAD_DIM]
            k = ret_ref[0, r0:r0 + c, width + h * HEAD_DIM:width + (h + 1) * HEAD_DIM]
            v = ret_ref[0, r0:r0 + c, 2 * width + h * HEAD_DIM:2 * width + (h + 1) * HEAD_DIM]
            gate = ret_ref[0, r0:r0 + c, 3 * width + h * HEAD_DIM:3 * width + (h + 1) * HEAD_DIM]
            dmask = jnp.where(tril, jnp.exp(lg * jnp.where(tril, relf, 0.0)), 0.0)
            scores = _dot_nt(q, k) * dmask
            intra = _dot(scores.astype(BF16), v)
            state = sret_ref[h]
            inter = _dot(q, state.astype(BF16)) * jnp.exp(lg * (rowf + 1.0))
            k_tail_t = (k.astype(F32) * jnp.exp(lg * (c - 1.0 - rowf))).T.astype(BF16)
            sret_ref[h] = state * math.exp(lg * c) + _dot(k_tail_t, v)
            o = intra + inter
            y = _rms(o) * _silu(gate.astype(F32))
            ybuf_ref[r0:r0 + c, h * HEAD_DIM:(h + 1) * HEAD_DIM] = y.astype(BF16)

        sm = small_ref[0, r0:r0 + c, :]
        hi, mid, lo = _split3(sm)
        gcum = _dot(tri_b, hi) + _dot(tri_b, mid) + _dot(tri_b, lo)
        gcum_t = gcum.T
        for h in range(HEADS):
            q = gdn_ref[0, r0:r0 + c, h * HEAD_DIM:(h + 1) * HEAD_DIM]
            k = gdn_ref[0, r0:r0 + c, width + h * HEAD_DIM:width + (h + 1) * HEAD_DIM]
            v = gdn_ref[0, r0:r0 + c, 2 * width + h * HEAD_DIM:2 * width + (h + 1) * HEAD_DIM]
            gate = gdn_ref[0, r0:r0 + c, 3 * width + h * HEAD_DIM:3 * width + (h + 1) * HEAD_DIM]
            gc_col = jnp.broadcast_to(gcum[:, HEADS + h:HEADS + h + 1], (c, c))
            gc_row = jnp.broadcast_to(gcum_t[HEADS + h:HEADS + h + 1, :], (c, c))
            g_last = jnp.broadcast_to(gcum[c - 1:c, HEADS + h:HEADS + h + 1], (c, c))
            beta_col = jnp.broadcast_to(sm[:, h:h + 1], (c, c))
            decay = jnp.exp(jnp.where(tril, gc_col - gc_row, NEG_BIG))
            kf = k.astype(F32)
            lmat = jnp.where(strict, _dot_nt(k, k) * beta_col * decay, 0.0)
            attn = jnp.where(tril, _dot_nt(q, k) * decay, 0.0).astype(BF16)
            tinv = eye - lmat
            lp = lmat
            for _ in range(int(math.log2(c)) - 1):
                lp = _mm(lp, lp)
                tinv = tinv + _mm(tinv, lp)
            tb = tinv.astype(BF16)
            egc = jnp.exp(gc_col)
            u = _dot(tb, (v.astype(F32) * beta_col).astype(BF16)).astype(BF16)
            w = _dot(tb, (kf * (beta_col * egc)).astype(BF16)).astype(BF16)
            au = _dot(attn, u)
            q_eff = (q.astype(F32) * egc - _dot(attn, w)).astype(BF16)
            kt_t = (kf * jnp.exp(g_last - gc_col)).T.astype(BF16)
            p_mat = _dot(kt_t, w)
            n_mat = _dot(kt_t, u)
            state = sgdn_ref[h]
            sb = state.astype(BF16)
            o = _dot(q_eff, sb) + au
            sgdn_ref[h] = jnp.exp(g_last) * state - _dot(p_mat.astype(BF16), sb) + n_mat
            y = _rms(o) * ogain * _silu(gate.astype(F32))
            ybuf_ref[r0:r0 + c, width + h * HEAD_DIM:width + (h + 1) * HEAD_DIM] = y.astype(BF16)

    out_ref[...] = x_ref[0] + _dot(ybuf_ref[...], wout_ref[...])


def _rglru_kernel(h_ref, gain_ref, win_ref, convw_ref, convb_ref, wax_ref, ba_ref, bx_ref,
                  lam_ref, wout_ref, out_ref, halo_ref, state_ref, a_ref, u_ref, hs_ref):
    ti = pl.program_id(0)
    rows = h_ref.shape[0]
    d_rnn = lam_ref.shape[1]
    steps = rows // SUBLANES
    halo = (LRU_CONV - 1) * SUBLANES

    @pl.when(ti == 0)
    def _():
        halo_ref[...] = jnp.zeros(halo_ref.shape, F32)
        state_ref[...] = jnp.zeros(state_ref.shape, F32)

    h_in = h_ref[...]
    hb = (_rms(h_in) * gain_ref[...]).astype(BF16)
    gate = _dot(hb, win_ref[:, 0:d_rnn])
    xpre = _dot(hb, win_ref[:, d_rnn:2 * d_rnn])
    prev = halo_ref[...]
    cw = convw_ref[...]
    xr = xpre * cw[LRU_CONV - 1:LRU_CONV, :] + convb_ref[...]
    for j in range(LRU_CONV - 1):
        back = (LRU_CONV - 1 - j) * SUBLANES
        shifted = jnp.concatenate([prev[halo - back:, :], xpre[:rows - back, :]], axis=0)
        xr = xr + shifted * cw[j:j + 1, :]
    halo_ref[...] = xpre[rows - halo:, :]

    sp = _softplus(-lam_ref[...])
    xrb = xr.astype(BF16)
    for n in range(d_rnn // LRU_BLOCK):
        c0 = n * LRU_BLOCK
        ri = _dot(xrb[:, c0:c0 + LRU_BLOCK], wax_ref[n])
        r = _sigmoid(ri[:, :LRU_BLOCK] + ba_ref[:, c0:c0 + LRU_BLOCK])
        i = _sigmoid(ri[:, LRU_BLOCK:] + bx_ref[:, c0:c0 + LRU_BLOCK])
        log_a = (-LRU_C) * r * sp[:, c0:c0 + LRU_BLOCK]
        a_ref[:, c0:c0 + LRU_BLOCK] = jnp.exp(log_a)
        u_ref[:, c0:c0 + LRU_BLOCK] = jnp.sqrt(1.0 - jnp.exp(2.0 * log_a)) * (i * xr[:, c0:c0 + LRU_BLOCK])

    state = state_ref[...]
    for t in range(steps):
        sl = slice(t * SUBLANES, (t + 1) * SUBLANES)
        state = a_ref[sl, :] * state + u_ref[sl, :]
        hs_ref[sl, :] = state
    state_ref[...] = state

    y = (_gelu_tanh(gate) * hs_ref[...]).astype(BF16)
    out_ref[...] = h_in + _dot(y, wout_ref[...])


def _ffn_kernel(h_ref, gain_ref, wup_ref, convw_ref, convb_ref, wdown_ref, fgain_ref, out_ref,
                halo_ref, *scratch, d_ff, col_chunk, final):
    ti = pl.program_id(0)
    rows = h_ref.shape[0]
    halo = (FFN_CONV - 1) * SUBLANES

    @pl.when(ti == 0)
    def _():
        halo_ref[...] = jnp.zeros(halo_ref.shape, F32)

    h_in = h_ref[...]
    hb = (_rms(h_in) * gain_ref[...]).astype(BF16)

    def conv(cur, c0):
        prev = halo_ref[:, c0:c0 + col_chunk]
        cw = convw_ref[:, c0:c0 + col_chunk]
        out = cur * cw[FFN_CONV - 1:FFN_CONV, :] + convb_ref[:, c0:c0 + col_chunk]
        for j in range(FFN_CONV - 1):
            back = (FFN_CONV - 1 - j) * SUBLANES
            shifted = jnp.concatenate([prev[halo - back:, :], cur[:rows - back, :]], axis=0)
            out = out + shifted * cw[j:j + 1, :]
        halo_ref[:, c0:c0 + col_chunk] = cur[rows - halo:, :]
        return out

    acc = jnp.zeros(h_in.shape, F32)
    for j in range(d_ff // col_chunk):
        g0 = j * col_chunk
        v0 = d_ff + j * col_chunk
        gate = conv(_dot(hb, wup_ref[:, g0:g0 + col_chunk]), g0)
        val = conv(_dot(hb, wup_ref[:, v0:v0 + col_chunk]), v0)
        act = (_silu(gate) * val).astype(BF16)
        acc = acc + _dot(act, wdown_ref[g0:g0 + col_chunk, :])
    res = h_in + acc
    if not final:
        out_ref[...] = res
    else:
        fin_ref, = scratch
        fin = _rms(res) * fgain_ref[...]
        steps = rows // SUBLANES
        for s in range(fin.shape[1] // LANES):
            fin_ref[s] = fin[:, s * LANES:(s + 1) * LANES]
        for s in range(fin.shape[1] // LANES):
            for b in range(SUBLANES):
                out_ref[b, :, s * LANES:(s + 1) * LANES] = fin_ref[s, pl.ds(b, steps, stride=SUBLANES), :]


def _const_spec(shape):
    zeros = (0,) * len(shape)
    return pl.BlockSpec(shape, lambda *_: zeros)


def _params(sem):
    return pltpu.CompilerParams(dimension_semantics=sem, vmem_limit_bytes=VMEM_LIMIT)


def _rope_tables(t):
    half = HEAD_DIM // 2
    inv_freq = ROPE_BASE ** (-jnp.arange(half, dtype=F32) / half)
    ang = jnp.arange(t, dtype=F32)[:, None] * inv_freq[None, :]
    cos = jnp.cos(ang)
    sin = jnp.sin(ang)
    return jnp.concatenate([cos, cos], axis=-1), jnp.concatenate([-sin, sin], axis=-1)


def _layer0_inproj(x, gain, w_in, conv_w, a_log, dt_bias, tile):
    b, t, d = x.shape
    width = HEADS * HEAD_DIM
    wmain = w_in[:, :8 * width].astype(BF16)
    wsmall = jnp.zeros((d, LANES), F32).at[:, :2 * HEADS].set(w_in[:, 8 * width:]).astype(BF16)
    alog = jnp.zeros((1, LANES), F32).at[0, HEADS:2 * HEADS].set(a_log)
    dtb = jnp.zeros((1, LANES), F32).at[0, HEADS:2 * HEADS].set(dt_bias)
    cos, sin = _rope_tables(t)
    return pl.pallas_call(
        _inproj_kernel,
        grid=(b, t // tile),
        in_specs=[
            pl.BlockSpec((1, tile, d), lambda i, j: (i, j, 0)),
            _const_spec((1, d)),
            _const_spec((d, 8 * width)),
            _const_spec((d, LANES)),
            pl.BlockSpec((tile, HEAD_DIM), lambda i, j: (j, 0)),
            pl.BlockSpec((tile, HEAD_DIM), lambda i, j: (j, 0)),
            _const_spec((GDN_CONV, 3 * width)),
            _const_spec((1, LANES)),
            _const_spec((1, LANES)),
        ],
        out_specs=[
            pl.BlockSpec((1, tile, 4 * width), lambda i, j: (i, j, 0)),
            pl.BlockSpec((1, tile, 4 * width), lambda i, j: (i, j, 0)),
            pl.BlockSpec((1, tile, LANES), lambda i, j: (i, j, 0)),
        ],
        out_shape=[
            jax.ShapeDtypeStruct((b, t, 4 * width), BF16),
            jax.ShapeDtypeStruct((b, t, 4 * width), BF16),
            jax.ShapeDtypeStruct((b, t, LANES), F32),
        ],
        scratch_shapes=[pltpu.VMEM((SUBLANES + tile + SUBLANES, 3 * width), F32)],
        compiler_params=_params(("arbitrary", "arbitrary")),
        name="l0_inproj",
    )(x, gain.reshape(1, d), wmain, wsmall, cos, sin, conv_w, alog, dtb)


def _layer0_mixer(ret, gdn, small, x, w_out, out_gain, tile):
    b, t, d = x.shape
    width = HEADS * HEAD_DIM
    return pl.pallas_call(
        _mixer_kernel,
        grid=(b, t // tile),
        in_specs=[
            pl.BlockSpec((1, tile, 4 * width), lambda i, j: (i, j, 0)),
            pl.BlockSpec((1, tile, 4 * width), lambda i, j: (i, j, 0)),
            pl.BlockSpec((1, tile, LANES), lambda i, j: (i, j, 0)),
            pl.BlockSpec((1, tile, d), lambda i, j: (i, j, 0)),
            _const_spec((2 * width, d)),
            _const_spec((1, HEAD_DIM)),
        ],
        out_specs=pl.BlockSpec((tile, d), lambda i, j: (j, i)),
        out_shape=jax.ShapeDtypeStruct((t, b * d), F32),
        scratch_shapes=[
            pltpu.VMEM((HEADS, HEAD_DIM, HEAD_DIM), F32),
            pltpu.VMEM((HEADS, HEAD_DIM, HEAD_DIM), F32),
            pltpu.VMEM((tile, 2 * width), BF16),
        ],
        compiler_params=_params(("arbitrary", "arbitrary")),
        name="l0_mixer",
    )(ret, gdn, small, x, w_out.astype(BF16), out_gain.reshape(1, HEAD_DIM))


def _rglru(h_tm, gain, w_in, conv_w, conv_b, w_a, b_a, w_x, b_x, lam, w_out, steps):
    n, d = h_tm.shape
    d_rnn = lam.shape[0]
    rows = steps * SUBLANES
    wax = jnp.concatenate([w_a, w_x], axis=-1).astype(BF16)
    return pl.pallas_call(
        _rglru_kernel,
        grid=(n // rows,),
        in_specs=[
            pl.BlockSpec((rows, d), lambda i: (i, 0)),
            _const_spec((1, d)),
            _const_spec((d, 2 * d_rnn)),
            _const_spec((LRU_CONV, d_rnn)),
            _const_spec((1, d_rnn)),
            _const_spec(wax.shape),
            _const_spec((1, d_rnn)),
            _const_spec((1, d_rnn)),
            _const_spec((1, d_rnn)),
            _const_spec((d_rnn, d)),
        ],
        out_specs=pl.BlockSpec((rows, d), lambda i: (i, 0)),
        out_shape=jax.ShapeDtypeStruct((n, d), F32),
        scratch_shapes=[
            pltpu.VMEM(((LRU_CONV - 1) * SUBLANES, d_rnn), F32),
            pltpu.VMEM((SUBLANES, d_rnn), F32),
            pltpu.VMEM((rows, d_rnn), F32),
            pltpu.VMEM((rows, d_rnn), F32),
            pltpu.VMEM((rows, d_rnn), F32),
        ],
        compiler_params=_params(("arbitrary",)),
        name="l1_rglru",
    )(h_tm, gain.reshape(1, d), w_in.astype(BF16), conv_w, conv_b.reshape(1, d_rnn), wax,
      b_a.reshape(1, d_rnn), b_x.reshape(1, d_rnn), lam.reshape(1, d_rnn), w_out.astype(BF16))


def _ffn(h_tm, gain, w_up, conv_w, conv_b, w_down, final_gain, steps, batch, final):
    n, d = h_tm.shape
    d_ff = w_down.shape[0]
    rows = steps * SUBLANES
    col_chunk = 2 * LANES
    kern = functools.partial(_ffn_kernel, d_ff=d_ff, col_chunk=col_chunk, final=final)
    scratch = [pltpu.VMEM(((FFN_CONV - 1) * SUBLANES, 2 * d_ff), F32)]
    if final:
        out_spec = pl.BlockSpec((batch, steps, d), lambda i: (0, i, 0))
        out_shape = jax.ShapeDtypeStruct((batch, n // batch, d), F32)
        scratch.append(pltpu.VMEM((d // LANES, rows, LANES), F32))
    else:
        out_spec = pl.BlockSpec((rows, d), lambda i: (i, 0))
        out_shape = jax.ShapeDtypeStruct((n, d), F32)
    return pl.pallas_call(
        kern,
        grid=(n // rows,),
        in_specs=[
            pl.BlockSpec((rows, d), lambda i: (i, 0)),
            _const_spec((1, d)),
            _const_spec((d, 2 * d_ff)),
            _const_spec((FFN_CONV, 2 * d_ff)),
            _const_spec((1, 2 * d_ff)),
            _const_spec((d_ff, d)),
            _const_spec((1, d)),
        ],
        out_specs=out_spec,
        out_shape=out_shape,
        scratch_shapes=scratch,
        compiler_params=_params(("arbitrary",)),
        name="ffn_final" if final else "ffn",
    )(h_tm, gain.reshape(1, d), w_up.astype(BF16), conv_w, conv_b.reshape(1, 2 * d_ff),
      w_down.astype(BF16), final_gain.reshape(1, d))


def kernel(x, norm_mix, norm_ffn, ret_gdn_w_in, gdn_conv_w, gdn_a_log, gdn_dt_bias, gdn_out_gain,
           ret_gdn_w_out, lru_w_in, lru_conv_w, lru_conv_b, lru_w_a, lru_b_a, lru_w_x, lru_b_x,
           lru_lambda, lru_w_out, ffn_w_up, ffn_conv_w, ffn_conv_b, ffn_w_down, norm_final):
    b, t, d = x.shape
    assert b == SUBLANES and norm_mix.shape[0] == 2
    tile0 = 256
    steps = 32

    ret, gdn, small = _layer0_inproj(x, norm_mix[0], ret_gdn_w_in[0], gdn_conv_w[0], gdn_a_log[0],
                                     gdn_dt_bias[0], tile0)
    h = _layer0_mixer(ret, gdn, small, x, ret_gdn_w_out[0], gdn_out_gain[0], tile0)
    h = h.reshape(t * b, d)
    h = _ffn(h, norm_ffn[0], ffn_w_up[0], ffn_conv_w[0], ffn_conv_b[0], ffn_w_down[0], norm_final,
             steps, b, final=False)
    h = _rglru(h, norm_mix[1], lru_w_in[0], lru_conv_w[0], lru_conv_b[0], lru_w_a[0], lru_b_a[0],
               lru_w_x[0], lru_b_x[0], lru_lambda[0], lru_w_out[0], steps)
    return _ffn(h, norm_ffn[1], ffn_w_up[1], ffn_conv_w[1], ffn_conv_b[1], ffn_w_down[1], norm_final,
                steps, b, final=True)
```

```python
import functools
import math

import numpy as np
import jax
import jax.numpy as jnp
from jax import lax
from jax.experimental import pallas as pl
from jax.experimental.pallas import tpu as pltpu

F32 = jnp.float32
BF16 = jnp.bfloat16

EPS = 1e-6
ROPE_BASE = 10000.0
HEADS = 4
HEAD_DIM = 128
CHUNK = 128
INV_BASE = 16
GDN_CONV = 4
LRU_CONV = 4
FFN_CONV = 3
LRU_C = 8.0
LRU_BLOCK = 128
SUBLANES = 8
LANES = 128
NEG_BIG = -1e30
VMEM_LIMIT = 56 * 1024 * 1024

RET_LOG_GAMMA = tuple(math.log1p(-2.0 ** (-5.0 - h)) for h in range(HEADS))


def _rms(x):
    return x * lax.rsqrt(jnp.mean(x * x, axis=-1, keepdims=True) + EPS)


def _sigmoid(x):
    return 1.0 / (1.0 + jnp.exp(-x))


def _silu(x):
    return x * _sigmoid(x)


def _softplus(x):
    return jnp.maximum(x, 0.0) + jnp.log1p(jnp.exp(-jnp.abs(x)))


def _gelu_tanh(x):
    c = math.sqrt(2.0 / math.pi)
    return 0.5 * x * (1.0 + jnp.tanh(c * (x + 0.044715 * (x * x * x))))


def _dot(a, b):
    return jnp.dot(a, b, preferred_element_type=F32)


def _dot_nt(a, b):
    return lax.dot_general(a, b, (((1,), (1,)), ((), ())), preferred_element_type=F32)


def _mm(a, b):
    return _dot(a.astype(BF16), b.astype(BF16))


def _split3(x):
    hi = x.astype(BF16)
    r1 = x - hi.astype(F32)
    mid = r1.astype(BF16)
    lo = (r1 - mid.astype(F32)).astype(BF16)
    return hi, mid, lo


def _inproj_kernel(x_ref, gain_ref, wmain_ref, wsmall_ref, cos_ref, sin_ref, convw_ref,
                   alog_ref, dtb_ref, ret_ref, gdn_ref, small_ref, cbuf_ref):
    ti = pl.program_id(1)
    tile = x_ref.shape[1]
    width = HEADS * HEAD_DIM
    hb = (_rms(x_ref[0]) * gain_ref[...]).astype(BF16)
    cos = cos_ref[...]
    sin = sin_ref[...]

    proj_qk = [_dot(hb, wmain_ref[:, part * width:(part + 1) * width]) for part in range(2)]
    proj_vg = _dot(hb, wmain_ref[:, 2 * width:4 * width])
    pre = _dot(hb, wmain_ref[:, 4 * width:7 * width])
    proj_gd = _dot(hb, wmain_ref[:, 7 * width:8 * width])
    sm = _dot(hb, wsmall_ref[...])

    for part in range(2):
        p = proj_qk[part]
        for h in range(HEADS):
            ph = p[:, h * HEAD_DIM:(h + 1) * HEAD_DIM]
            rot = ph * cos + pltpu.roll(ph, HEAD_DIM // 2, 1) * sin
            if part == 1:
                rot = rot * (HEAD_DIM ** -0.5)
            c0 = part * width + h * HEAD_DIM
            ret_ref[0, :, c0:c0 + HEAD_DIM] = rot.astype(BF16)
    ret_ref[0, :, 2 * width:4 * width] = proj_vg.astype(BF16)


    @pl.when(ti == 0)
    def _():
        cbuf_ref[0:SUBLANES, :] = jnp.zeros((SUBLANES, 3 * width), F32)

    cbuf_ref[SUBLANES:SUBLANES + tile, :] = pre
    cw = convw_ref[...]
    acc = pre * cw[GDN_CONV - 1:GDN_CONV, :]
    for j in range(GDN_CONV - 1):
        off = SUBLANES - (GDN_CONV - 1) + j
        acc = acc + cbuf_ref[off:off + tile, :] * cw[j:j + 1, :]
    cbuf_ref[0:SUBLANES, :] = cbuf_ref[tile:tile + SUBLANES, :]
    act = _silu(acc)
    for part in range(2):
        for h in range(HEADS):
            c0 = part * width + h * HEAD_DIM
            ah = act[:, c0:c0 + HEAD_DIM]
            nrm = ah * lax.rsqrt(jnp.sum(ah * ah, axis=-1, keepdims=True) + EPS)
            if part == 0:
                nrm = nrm * (HEAD_DIM ** -0.5)
            gdn_ref[0, :, c0:c0 + HEAD_DIM] = nrm.astype(BF16)
    gdn_ref[0, :, 2 * width:3 * width] = act[:, 2 * width:3 * width].astype(BF16)
    gdn_ref[0, :, 3 * width:4 * width] = proj_gd.astype(BF16)

    col =lax.broadcasted_iota(jnp.int32, sm.shape, 1)
    beta = _sigmoid(sm)
    g = -jnp.exp(alog_ref[...]) * _softplus(sm + dtb_ref[...])
    small_ref[0] = jnp.where(col < HEADS, beta, g)


def _mixer_kernel(ret_ref, gdn_ref, small_ref, x_ref, wout_ref, ogain_ref, out_ref,
                  sret_ref, sgdn_ref, ybuf_ref):
    ti = pl.program_id(1)
    tile = x_ref.shape[1]
    width = HEADS * HEAD_DIM
    c = CHUNK

    @pl.when(ti == 0)
    def _():
        sret_ref[...] = jnp.zeros(sret_ref.shape, F32)
        sgdn_ref[...] = jnp.zeros(sgdn_ref.shape, F32)

    row = lax.broadcasted_iota(jnp.int32, (c, c), 0)
    col = lax.broadcasted_iota(jnp.int32, (c, c), 1)
    tril = row >= col
    strict = row > col
    rowf = row.astype(F32)
    relf = (row - col).astype(F32)
    eye = jnp.where(row == col, 1.0, 0.0).astype(F32)
    tri_b = jnp.where(tril, 1.0, 0.0).astype(BF16)
    ogain = ogain_ref[...]
    nblk = tile // c
    probs = [(blk, h) for blk in range(nblk) for h in range(HEADS)]

    def cols(ref, part, blk, h):
        c0 = part * width + h * HEAD_DIM
        return ref[0, blk * c:(blk + 1) * c, c0:c0 + HEAD_DIM]


    sms, gcums, gcum_ts = [], [], []
    for blk in range(nblk):
        sm = small_ref[0, blk * c:(blk + 1) * c, :]
        parts = _dot(tri_b, jnp.concatenate(_split3(sm), axis=1))
        gcum = parts[:, :LANES] + parts[:, LANES:2 * LANES] + parts[:, 2 * LANES:]
        sms.append(sm)
        gcums.append(gcum)
        gcum_ts.append(gcum.T)
    ret_qk = {}
    gdn_kq = {}
    for p in probs:
        blk, h = p
        ret_qk[p] = _dot_nt(cols(ret_ref, 0, blk, h), cols(ret_ref, 1, blk, h))
        k = cols(gdn_ref, 1, blk, h)
        gdn_kq[p] = _dot_nt(jnp.concatenate([k, cols(gdn_ref, 0, blk, h)], axis=0), k)

    ret_iv = {}
    for p in probs:
        blk, h = p
        lg = RET_LOG_GAMMA[h]
        dmask = jnp.where(tril, jnp.exp(lg * jnp.where(tril, relf, 0.0)), 0.0)
        scores = (ret_qk[p] * dmask).astype(BF16)
        k_tail_t = (cols(ret_ref, 1, blk, h).astype(F32) * jnp.exp(lg * (c - 1.0 - rowf))).T.astype(BF16)
        ret_iv[p] = _dot(jnp.concatenate([scores, k_tail_t], axis=0), cols(ret_ref, 2, blk, h))

    lmats, attns, rhs, kts, egcs, glasts = {}, {}, {}, {}, {}, {}
    for p in probs:
        blk, h = p
        gcum, sm = gcums[blk], sms[blk]
        gc_col = jnp.broadcast_to(gcum[:, HEADS + h:HEADS + h + 1], (c, c))
        gc_row = jnp.broadcast_to(gcum_ts[blk][HEADS + h:HEADS + h + 1, :], (c, c))
        g_last = jnp.broadcast_to(gcum[c - 1:c, HEADS + h:HEADS + h + 1], (c, c))
        beta_col = jnp.broadcast_to(sm[:, h:h + 1], (c, c))
        decay = jnp.exp(jnp.where(tril, gc_col - gc_row, NEG_BIG))
        kf = cols(gdn_ref, 1, blk, h).astype(F32)
        egc = jnp.exp(gc_col)
        lmats[p] = jnp.where(strict, gdn_kq[p][:c] * beta_col * decay, 0.0)
        attns[p] = jnp.where(tril, gdn_kq[p][c:] * decay, 0.0).astype(BF16)
        rhs[p] = jnp.concatenate([(cols(gdn_ref, 2, blk, h).astype(F32) * beta_col).astype(BF16),
                                  (kf * (beta_col * egc)).astype(BF16)], axis=1)
        kts[p] = (kf * jnp.exp(g_last - gc_col)).T.astype(BF16)
        egcs[p] = egc
        glasts[p] = g_last

    def same_block(size):
        return (row // size) == (col // size)

    diag = same_block(INV_BASE)
    tinv, power = {}, {}
    for p in probs:
        ld = jnp.where(diag, lmats[p], 0.0)
        tinv[p] = eye - ld
        power[p] = _mm(ld, ld)
    factors = int(math.log2(INV_BASE)) - 1
    for f in range(factors):
        last = f == factors - 1
        for p in probs:
            pw = power[p].astype(BF16)
            if last:
                tinv[p] = tinv[p] + _dot(tinv[p].astype(BF16), pw)
            else:
                both = _dot(jnp.concatenate([tinv[p].astype(BF16), pw], axis=0), pw)
                tinv[p] = tinv[p] + both[:c]
                power[p] = both[c:]
    size = INV_BASE
    while size < c:
        off = same_block(2 * size) & jnp.logical_not(same_block(size))
        ct = {p: _dot(jnp.where(off, lmats[p], 0.0).astype(BF16), tinv[p].astype(BF16)) for p in probs}
        for p in probs:
            tinv[p] = tinv[p] - _dot(tinv[p].astype(BF16), ct[p].astype(BF16))
        size *= 2

    ret_inter = {}
    for h in range(HEADS):
        lg = RET_LOG_GAMMA[h]
        state = sret_ref[h]
        for blk in range(nblk):
            ret_inter[(blk, h)] = _dot(cols(ret_ref, 0, blk, h), state.astype(BF16))
            state = state * math.exp(lg * c) + ret_iv[(blk, h)][c:]
        sret_ref[h] = state

    uw = {p: _dot(tinv[p].astype(BF16), rhs[p]).astype(BF16) for p in probs}
    apn = {}
    for p in probs:
        wu = jnp.concatenate([uw[p][:, HEAD_DIM:], uw[p][:, :HEAD_DIM]], axis=1)
        apn[p] = _dot(jnp.concatenate([attns[p], kts[p]], axis=0), wu)

    for p in probs:
        blk, h = p
        lg = RET_LOG_GAMMA[h]
        o = ret_iv[p][:c] + ret_inter[p] * jnp.exp(lg * (rowf + 1.0))
        y = _rms(o) * _silu(cols(ret_ref, 3, blk, h).astype(F32))
        ybuf_ref[blk * c:(blk + 1) * c, h * HEAD_DIM:(h + 1) * HEAD_DIM] = y.astype(BF16)

    states = [sgdn_ref[h] for h in range(HEADS)]
    for blk in range(nblk):
        for h in range(HEADS):
            p = (blk, h)
            q_eff = (cols(gdn_ref, 0, blk, h).astype(F32) * egcs[p] - apn[p][:c, :HEAD_DIM]).astype(BF16)
            pq = jnp.concatenate([apn[p][c:, :HEAD_DIM].astype(BF16), q_eff], axis=0)
            ps = _dot(pq, states[h].astype(BF16))
            o = ps[c:] + apn[p][:c, HEAD_DIM:]
            states[h] = jnp.exp(glasts[p]) * states[h] - ps[:c] + apn[p][c:, HEAD_DIM:]
            y = _rms(o) * ogain * _silu(cols(gdn_ref, 3, blk, h).astype(F32))
            ybuf_ref[blk * c:(blk + 1) * c, width + h * HEAD_DIM:width + (h + 1) * HEAD_DIM] = y.astype(BF16)
    for h in range(HEADS):
        sgdn_ref[h] = states[h]

    out_ref[0] = x_ref[0] + _dot(ybuf_ref[...], wout_ref[...])


def _rglru_kernel(h_ref, gain_ref, win_ref, convw_ref, convb_ref, wax_ref, ba_ref, bx_ref,
                  lam_ref, wout_ref, out_ref, halo_ref, state_ref, a_ref, u_ref, hs_ref):
    ti = pl.program_id(0)
    rows = h_ref.shape[0]
    d_rnn = lam_ref.shape[1]
    steps = rows // SUBLANES
    halo = (LRU_CONV - 1) * SUBLANES

    @pl.when(ti == 0)
    def _():
        halo_ref[...] = jnp.zeros(halo_ref.shape, F32)
        state_ref[...] = jnp.zeros(state_ref.shape, F32)

    h_in = h_ref[...]
    hb = (_rms(h_in) * gain_ref[...]).astype(BF16)
    gate = _dot(hb, win_ref[:, 0:d_rnn])
    xpre = _dot(hb, win_ref[:, d_rnn:2 * d_rnn])
    prev = halo_ref[...]
    cw = convw_ref[...]
    xr = xpre * cw[LRU_CONV - 1:LRU_CONV, :] + convb_ref[...]
    for j in range(LRU_CONV - 1):
        back = (LRU_CONV - 1 - j) * SUBLANES
        shifted = jnp.concatenate([prev[halo - back:, :], xpre[:rows - back, :]], axis=0)
        xr = xr + shifted * cw[j:j + 1, :]
    halo_ref[...] = xpre[rows - halo:, :]

    sp = _softplus(-lam_ref[...])
    xrb = xr.astype(BF16)
    nblock = d_rnn // LRU_BLOCK
    ris = [_dot(xrb[:, n * LRU_BLOCK:(n + 1) * LRU_BLOCK], wax_ref[n]) for n in range(nblock)]
    for n in range(nblock):
        c0 = n * LRU_BLOCK
        ri = ris[n]
        r = _sigmoid(ri[:, :LRU_BLOCK] + ba_ref[:, c0:c0 + LRU_BLOCK])
        i = _sigmoid(ri[:, LRU_BLOCK:] + bx_ref[:, c0:c0 + LRU_BLOCK])
        log_a = (-LRU_C) * r * sp[:, c0:c0 + LRU_BLOCK]
        a_ref[:, c0:c0 + LRU_BLOCK] = jnp.exp(log_a)
        u_ref[:, c0:c0 + LRU_BLOCK] = jnp.sqrt(1.0 - jnp.exp(2.0 * log_a)) * (i * xr[:, c0:c0 + LRU_BLOCK])

    state = state_ref[...]
    for t in range(steps):
        sl = slice(t * SUBLANES, (t + 1) * SUBLANES)
        state = a_ref[sl, :] * state + u_ref[sl, :]
        hs_ref[sl, :] = state
    state_ref[...] = state

    y = (_gelu_tanh(gate) * hs_ref[...]).astype(BF16)
    out_ref[...] = h_in + _dot(y, wout_ref[...])


def _ffn_kernel(h_ref, gain_ref, wup_ref, convw_ref, convb_ref, wdown_ref, fgain_ref, out_ref,
                halo_ref, slab_ref, *, d_ff, col_chunk, batch_major_in, final):
    ti = pl.program_id(0)
    halo = (FFN_CONV - 1) * SUBLANES

    @pl.when(ti == 0)
    def _():
        halo_ref[...] = jnp.zeros(halo_ref.shape, F32)

    if batch_major_in:
        steps = h_ref.shape[1]
        nslab = h_ref.shape[2] // LANES
        for s in range(nslab):
            for b in range(SUBLANES):
                slab_ref[s, pl.ds(b, steps, stride=SUBLANES), :] = h_ref[b, :, s * LANES:(s + 1) * LANES]
        h_in = jnp.concatenate([slab_ref[s] for s in range(nslab)], axis=1)
    else:
        h_in = h_ref[...]
    rows = h_in.shape[0]
    hb = (_rms(h_in) * gain_ref[...]).astype(BF16)

    def conv(cur, c0):
        prev = halo_ref[:, c0:c0 + col_chunk]
        cw = convw_ref[:, c0:c0 + col_chunk]
        out = cur * cw[FFN_CONV - 1:FFN_CONV, :] + convb_ref[:, c0:c0 + col_chunk]
        for j in range(FFN_CONV - 1):
            back = (FFN_CONV - 1 - j) * SUBLANES
            shifted = jnp.concatenate([prev[halo - back:, :], cur[:rows - back, :]], axis=0)
            out = out + shifted * cw[j:j + 1, :]
        halo_ref[:, c0:c0 + col_chunk] = cur[rows - halo:, :]
        return out

    def up(j):
        g0 = j * col_chunk
        v0 = d_ff + j * col_chunk
        return _dot(hb, wup_ref[:, g0:g0 + col_chunk]), _dot(hb, wup_ref[:, v0:v0 + col_chunk])

    nchunk = d_ff // col_chunk
    acc = jnp.zeros(h_in.shape, F32)
    cur = up(0)
    for j in range(nchunk):
        nxt = up(j + 1) if j + 1 < nchunk else None
        g0 = j * col_chunk
        gate = conv(cur[0], g0)
        val = conv(cur[1], d_ff + g0)
        act = (_silu(gate) * val).astype(BF16)
        acc = acc + _dot(act, wdown_ref[g0:g0 + col_chunk, :])
        cur = nxt
    res = h_in + acc
    if not final:
        out_ref[...] = res
    else:
        fin = _rms(res) * fgain_ref[...]
        steps = rows // SUBLANES
        for s in range(fin.shape[1] // LANES):
            slab_ref[s] = fin[:, s * LANES:(s + 1) * LANES]
        for s in range(fin.shape[1] // LANES):
            for b in range(SUBLANES):
                out_ref[b, :, s * LANES:(s + 1) * LANES] = slab_ref[s, pl.ds(b, steps, stride=SUBLANES), :]


def _const_spec(shape):
    zeros = (0,) * len(shape)
    return pl.BlockSpec(shape, lambda *_: zeros)


def _params(sem, flags=None):
    return pltpu.CompilerParams(dimension_semantics=sem, vmem_limit_bytes=VMEM_LIMIT, flags=flags)


def _rope_tables(t):
    half = HEAD_DIM // 2
    inv_freq = ROPE_BASE ** (-jnp.arange(half, dtype=F32) / half)
    ang = jnp.arange(t, dtype=F32)[:, None] * inv_freq[None, :]
    cos = jnp.cos(ang)
    sin = jnp.sin(ang)
    return jnp.concatenate([cos, cos], axis=-1), jnp.concatenate([-sin, sin], axis=-1)


def _layer0_inproj(x, gain, w_in, conv_w, a_log, dt_bias, tile):
    b, t, d = x.shape
    width = HEADS * HEAD_DIM
    wmain = w_in[:, :8 * width].astype(BF16)
    wsmall = jnp.zeros((d, LANES), F32).at[:, :2 * HEADS].set(w_in[:, 8 * width:]).astype(BF16)
    alog = jnp.zeros((1, LANES), F32).at[0, HEADS:2 * HEADS].set(a_log)
    dtb = jnp.zeros((1, LANES), F32).at[0, HEADS:2 * HEADS].set(dt_bias)
    cos, sin = _rope_tables(t)
    return pl.pallas_call(
        _inproj_kernel,
        grid=(b, t // tile),
        in_specs=[
            pl.BlockSpec((1, tile, d), lambda i, j: (i, j, 0)),
            _const_spec((1, d)),
            _const_spec((d, 8 * width)),
            _const_spec((d, LANES)),
            pl.BlockSpec((tile, HEAD_DIM), lambda i, j: (j, 0)),
            pl.BlockSpec((tile, HEAD_DIM), lambda i, j: (j, 0)),
            _const_spec((GDN_CONV, 3 * width)),
            _const_spec((1, LANES)),
            _const_spec((1, LANES)),
        ],
        out_specs=[
            pl.BlockSpec((1, tile, 4 * width), lambda i, j: (i, j, 0)),
            pl.BlockSpec((1, tile, 4 * width), lambda i, j: (i, j, 0)),
            pl.BlockSpec((1, tile, LANES), lambda i, j: (i, j, 0)),
        ],
        out_shape=[
            jax.ShapeDtypeStruct((b, t, 4 * width), BF16),
            jax.ShapeDtypeStruct((b, t, 4 * width), BF16),
            jax.ShapeDtypeStruct((b, t, LANES), F32),
        ],
        scratch_shapes=[pltpu.VMEM((SUBLANES + tile + SUBLANES, 3 * width), F32)],
        compiler_params=_params(("arbitrary", "arbitrary")),
        name="l0_inproj",
    )(x, gain.reshape(1, d), wmain, wsmall, cos, sin, conv_w, alog, dtb)


def _layer0_mixer(ret, gdn, small, x, w_out, out_gain, tile):
    b, t, d = x.shape
    width = HEADS * HEAD_DIM
    return pl.pallas_call(
        _mixer_kernel,
        grid=(b, t // tile),
        in_specs=[
            pl.BlockSpec((1, tile, 4 * width), lambda i, j: (i, j, 0)),
            pl.BlockSpec((1, tile, 4 * width), lambda i, j: (i, j, 0)),
            pl.BlockSpec((1, tile, LANES), lambda i, j: (i, j, 0)),
            pl.BlockSpec((1, tile, d), lambda i, j: (i, j, 0)),
            _const_spec((2 * width, d)),
            _const_spec((1, HEAD_DIM)),
        ],
        out_specs=pl.BlockSpec((1, tile, d), lambda i, j: (i, j, 0)),
        out_shape=jax.ShapeDtypeStruct((b, t, d), F32),
        scratch_shapes=[
            pltpu.VMEM((HEADS, HEAD_DIM, HEAD_DIM), F32),
            pltpu.VMEM((HEADS, HEAD_DIM, HEAD_DIM), F32),
            pltpu.VMEM((tile, 2 * width), BF16),
        ],
        compiler_params=_params(("arbitrary", "arbitrary")),
        name="l0_mixer",
    )(ret, gdn, small, x, w_out.astype(BF16), out_gain.reshape(1, HEAD_DIM))


def _rglru(h_tm, gain, w_in, conv_w, conv_b, w_a, b_a, w_x, b_x, lam, w_out, steps):
    n, d = h_tm.shape
    d_rnn = lam.shape[0]
    rows = steps * SUBLANES
    wax = jnp.concatenate([w_a, w_x], axis=-1).astype(BF16)
    return pl.pallas_call(
        _rglru_kernel,
        grid=(n // rows,),
        in_specs=[
            pl.BlockSpec((rows, d), lambda i: (i, 0)),
            _const_spec((1, d)),
            _const_spec((d, 2 * d_rnn)),
            _const_spec((LRU_CONV, d_rnn)),
            _const_spec((1, d_rnn)),
            _const_spec(wax.shape),
            _const_spec((1, d_rnn)),
            _const_spec((1, d_rnn)),
            _const_spec((1, d_rnn)),
            _const_spec((d_rnn, d)),
        ],
        out_specs=pl.BlockSpec((rows, d), lambda i: (i, 0)),
        out_shape=jax.ShapeDtypeStruct((n, d), F32),
        scratch_shapes=[
            pltpu.VMEM(((LRU_CONV - 1) * SUBLANES, d_rnn), F32),
            pltpu.VMEM((SUBLANES, d_rnn), F32),
            pltpu.VMEM((rows, d_rnn), F32),
            pltpu.VMEM((rows, d_rnn), F32),
            pltpu.VMEM((rows, d_rnn), F32),
        ],
        compiler_params=_params(("arbitrary",)),
        name="l1_rglru",
    )(h_tm, gain.reshape(1, d), w_in.astype(BF16), conv_w, conv_b.reshape(1, d_rnn), wax,
      b_a.reshape(1, d_rnn), b_x.reshape(1, d_rnn), lam.reshape(1, d_rnn), w_out.astype(BF16))


def _ffn(h, gain, w_up, conv_w, conv_b, w_down, final_gain, steps, batch_major_in, final):
    d = h.shape[-1]
    n = h.size // d
    d_ff = w_down.shape[0]
    rows = steps * SUBLANES
    col_chunk = 2 * LANES
    kern = functools.partial(_ffn_kernel, d_ff=d_ff, col_chunk=col_chunk,
                             batch_major_in=batch_major_in, final=final)
    scratch = [pltpu.VMEM(((FFN_CONV - 1) * SUBLANES, 2 * d_ff), F32),
               pltpu.VMEM((d // LANES, rows, LANES), F32)]
    bm_spec = pl.BlockSpec((SUBLANES, steps, d), lambda i: (0, i, 0))
    tm_spec = pl.BlockSpec((rows, d), lambda i: (i, 0))
    if final:
        out_spec = bm_spec
        out_shape = jax.ShapeDtypeStruct((SUBLANES, n // SUBLANES, d), F32)
    else:
        out_spec = tm_spec
        out_shape = jax.ShapeDtypeStruct((n, d), F32)
    return pl.pallas_call(
        kern,
        grid=(n // rows,),
        in_specs=[
            bm_spec if batch_major_in else tm_spec,
            _const_spec((1, d)),
            _const_spec((d, 2 * d_ff)),
            _const_spec((FFN_CONV, 2 * d_ff)),
            _const_spec((1, 2 * d_ff)),
            _const_spec((d_ff, d)),
            _const_spec((1, d)),
        ],
        out_specs=out_spec,
        out_shape=out_shape,
        scratch_shapes=scratch,
        compiler_params=_params(("arbitrary",)),
        name="ffn_final" if final else "ffn",
    )(h, gain.reshape(1, d), w_up.astype(BF16), conv_w, conv_b.reshape(1, 2 * d_ff),
      w_down.astype(BF16), final_gain.reshape(1, d))


def kernel(x, norm_mix, norm_ffn, ret_gdn_w_in, gdn_conv_w, gdn_a_log, gdn_dt_bias, gdn_out_gain,
           ret_gdn_w_out, lru_w_in, lru_conv_w, lru_conv_b, lru_w_a, lru_b_a, lru_w_x, lru_b_x,
           lru_lambda, lru_w_out, ffn_w_up, ffn_conv_w, ffn_conv_b, ffn_w_down, norm_final):
    b, t, d = x.shape
    assert b == SUBLANES and norm_mix.shape[0] == 2
    tile0 = 256
    steps = 32

    ret, gdn, small = _layer0_inproj(x, norm_mix[0], ret_gdn_w_in[0], gdn_conv_w[0], gdn_a_log[0],
                                     gdn_dt_bias[0], tile0)
    h = _layer0_mixer(ret, gdn, small, x, ret_gdn_w_out[0], gdn_out_gain[0], tile0)
    h = _ffn(h, norm_ffn[0], ffn_w_up[0], ffn_conv_w[0], ffn_conv_b[0], ffn_w_down[0], norm_final,
             steps, batch_major_in=True, final=False)
    h = _rglru(h, norm_mix[1], lru_w_in[0], lru_conv_w[0], lru_conv_b[0], lru_w_a[0], lru_b_a[0],
               lru_w_x[0], lru_b_x[0], lru_lambda[0], lru_w_out[0], steps)
    return _ffn(h, norm_ffn[1], ffn_w_up[1], ffn_conv_w[1], ffn_conv_b[1], ffn_w_down[1], norm_final,
                steps, batch_major_in=False, final=True)
```

```python
import functools
import math

import numpy as np
import jax
import jax.numpy as jnp
from jax import lax
from jax.experimental import pallas as pl
from jax.experimental.pallas import tpu as pltpu

F32 = jnp.float32
BF16 = jnp.bfloat16

EPS = 1e-6
ROPE_BASE = 10000.0
HEADS = 4
HEAD_DIM = 128
CHUNK = 128
ROW_CHUNK = 64
INV_BASE = 16
GDN_CONV = 4
LRU_CONV = 4
FFN_CONV = 3
LRU_C = 8.0
LRU_BLOCK = 128
SUBLANES = 8
LANES = 128
NEG_BIG = -1e30
VMEM_LIMIT = 56 * 1024 * 1024

RET_LOG_GAMMA = tuple(math.log1p(-2.0 ** (-5.0 - h)) for h in range(HEADS))


def _rms(x):
    return x * lax.rsqrt(jnp.mean(x * x, axis=-1, keepdims=True) + EPS)


def _sigmoid(x):
    return 1.0 / (1.0 + jnp.exp(-x))


def _silu(x):
    return x * _sigmoid(x)


def _softplus(x):
    return jnp.maximum(x, 0.0) + jnp.log1p(jnp.exp(-jnp.abs(x)))


def _gelu_tanh(x):
    c = math.sqrt(2.0 / math.pi)
    return 0.5 * x * (1.0 + jnp.tanh(c * (x + 0.044715 * (x * x * x))))


def _dot(a, b):
    return jnp.dot(a, b, preferred_element_type=F32)


def _dot_nt(a, b):
    return lax.dot_general(a, b, (((1,), (1,)), ((), ())), preferred_element_type=F32)


def _mm(a, b):
    return _dot(a.astype(BF16), b.astype(BF16))


def _split3(x):
    hi = x.astype(BF16)
    r1 = x - hi.astype(F32)
    mid = r1.astype(BF16)
    lo = (r1 - mid.astype(F32)).astype(BF16)
    return hi, mid, lo


def _inproj_kernel(x_ref, gain_ref, wmain_ref, wsmall_ref, cos_ref, sin_ref, convw_ref,
                   alog_ref, dtb_ref, ret_ref, gdn_ref, small_ref, pbuf0_ref, pbuf1_ref, hb_ref, *, tiles_per_seq):
    s = pl.program_id(0)
    tile = x_ref.shape[1]
    width = HEADS * HEAD_DIM
    nmain = 8 * width
    body = slice(SUBLANES, SUBLANES + tile)
    qkv = slice(4 * width, 7 * width)

    @pl.when(s == 0)
    def _():
        pbuf1_ref[...] = jnp.zeros(pbuf1_ref.shape, F32)

    def step(pa_ref, pb_ref):
        for r in range(tile // ROW_CHUNK):
            rows = slice(r * ROW_CHUNK, (r + 1) * ROW_CHUNK)
            hb_ref[rows, :] = (_rms(x_ref[0, rows, :]) * gain_ref[...]).astype(BF16)
        for c0, c1 in ((0, 2 * width), (2 * width, 4 * width), (4 * width, 7 * width), (7 * width, nmain)):
            pa_ref[body, c0:c1] = _dot(hb_ref[...], wmain_ref[:, c0:c1])
        pa_ref[body, nmain:nmain + LANES] = _dot(hb_ref[...], wsmall_ref[...])

        first_of_seq = lax.rem(jnp.maximum(s - 1, 0), tiles_per_seq) == 0
        pb_ref[0:SUBLANES, qkv] = jnp.where(first_of_seq, 0.0, pb_ref[0:SUBLANES, qkv])
        for r in range(tile // ROW_CHUNK):
            out_rows = slice(r * ROW_CHUNK, (r + 1) * ROW_CHUNK)
            rows = slice(SUBLANES + r * ROW_CHUNK, SUBLANES + (r + 1) * ROW_CHUNK)
            cos = cos_ref[out_rows, :]
            sin = sin_ref[out_rows, :]
            for part in range(2):
                for h in range(HEADS):
                    c0 = part * width + h * HEAD_DIM
                    ph = pb_ref[rows, c0:c0 + HEAD_DIM]
                    rot = ph * cos + pltpu.roll(ph, HEAD_DIM // 2, 1) * sin
                    if part == 1:
                        rot = rot * (HEAD_DIM ** -0.5)
                    ret_ref[0, out_rows, c0:c0 + HEAD_DIM] = rot.astype(BF16)
            for c0 in range(2 * width, 4 * width, LANES):
                ret_ref[0, out_rows, c0:c0 + LANES] = pb_ref[rows, c0:c0 + LANES].astype(BF16)
            for c0 in range(0, width, LANES):
                gdn_ref[0, out_rows, 3 * width + c0:3 * width + c0 + LANES] = (
                    pb_ref[rows, 7 * width + c0:7 * width + c0 + LANES].astype(BF16))
            for part in range(3):
                for h in range(HEADS):
                    c0 = part * width + h * HEAD_DIM
                    src_cols = slice(4 * width + c0, 4 * width + c0 + HEAD_DIM)
                    cw = convw_ref[:, c0:c0 + HEAD_DIM]
                    acc = pb_ref[rows, src_cols] * cw[GDN_CONV - 1:GDN_CONV, :]
                    for j in range(GDN_CONV - 1):
                        back = GDN_CONV - 1 - j
                        shifted = pb_ref[rows.start - back:rows.stop - back, src_cols]
                        acc = acc + shifted * cw[j:j + 1, :]
                    act = _silu(acc)
                    if part < 2:
                        act = act * lax.rsqrt(jnp.sum(act * act, axis=-1, keepdims=True) + EPS)
                    if part == 0:
                        act = act * (HEAD_DIM ** -0.5)
                    gdn_ref[0, out_rows, c0:c0 + HEAD_DIM] = act.astype(BF16)
            sm = pb_ref[rows, nmain:nmain + LANES]
            col = lax.broadcasted_iota(jnp.int32, sm.shape, 1)
            beta = _sigmoid(sm)
            g = -jnp.exp(alog_ref[...]) * _softplus(sm + dtb_ref[...])
            small_ref[0, out_rows, :] = jnp.where(col < HEADS, beta, g)
        pa_ref[0:SUBLANES, qkv] = pb_ref[tile:tile + SUBLANES, qkv]

    parity = lax.rem(s, 2)

    @pl.when(parity == 0)
    def _():
        step(pbuf0_ref, pbuf1_ref)

    @pl.when(parity == 1)
    def _():
        step(pbuf1_ref, pbuf0_ref)


def _mixer_kernel(ret_ref, gdn_ref, small_ref, x_ref, wout_ref, ogain_ref, out_ref,
                  sret_ref, sgdn_ref, ybuf_ref):
    ti = pl.program_id(1)
    tile = x_ref.shape[1]
    width = HEADS * HEAD_DIM
    c = CHUNK

    @pl.when(ti == 0)
    def _():
        sret_ref[...] = jnp.zeros(sret_ref.shape, F32)
        sgdn_ref[...] = jnp.zeros(sgdn_ref.shape, F32)

    row = lax.broadcasted_iota(jnp.int32, (c, c), 0)
    col = lax.broadcasted_iota(jnp.int32, (c, c), 1)
    tril = row >= col
    strict = row > col
    rowf = row.astype(F32)
    relf = (row - col).astype(F32)
    eye = jnp.where(row == col, 1.0, 0.0).astype(F32)
    tri_b = jnp.where(tril, 1.0, 0.0).astype(BF16)
    ogain = ogain_ref[...]
    nblk = tile // c
    probs = [(blk, h) for blk in range(nblk) for h in range(HEADS)]

    def cols(ref, part, blk, h):
        c0 = part * width + h * HEAD_DIM
        return ref[0, blk * c:(blk + 1) * c, c0:c0 + HEAD_DIM]


    sms, gcums, gcum_ts = [], [], []
    for blk in range(nblk):
        sm = small_ref[0, blk * c:(blk + 1) * c, :]
        parts = _dot(tri_b, jnp.concatenate(_split3(sm), axis=1))
        gcum = parts[:, :LANES] + parts[:, LANES:2 * LANES] + parts[:, 2 * LANES:]
        sms.append(sm)
        gcums.append(gcum)
        gcum_ts.append(gcum.T)
    ret_qk = {}
    gdn_kq = {}
    for p in probs:
        blk, h = p
        ret_qk[p] = _dot_nt(cols(ret_ref, 0, blk, h), cols(ret_ref, 1, blk, h))
        k = cols(gdn_ref, 1, blk, h)
        gdn_kq[p] = _dot_nt(jnp.concatenate([k, cols(gdn_ref, 0, blk, h)], axis=0), k)

    ret_iv = {}
    for p in probs:
        blk, h = p
        lg = RET_LOG_GAMMA[h]
        dmask = jnp.where(tril, jnp.exp(lg * jnp.where(tril, relf, 0.0)), 0.0)
        scores = (ret_qk[p] * dmask).astype(BF16)
        k_tail_t = (cols(ret_ref, 1, blk, h).astype(F32) * jnp.exp(lg * (c - 1.0 - rowf))).T.astype(BF16)
        ret_iv[p] = _dot(jnp.concatenate([scores, k_tail_t], axis=0), cols(ret_ref, 2, blk, h))

    lmats, attns, rhs, kts, egcs, glasts = {}, {}, {}, {}, {}, {}
    for p in probs:
        blk, h = p
        gcum, sm = gcums[blk], sms[blk]
        gc_col = jnp.broadcast_to(gcum[:, HEADS + h:HEADS + h + 1], (c, c))
        gc_row = jnp.broadcast_to(gcum_ts[blk][HEADS + h:HEADS + h + 1, :], (c, c))
        g_last = jnp.broadcast_to(gcum[c - 1:c, HEADS + h:HEADS + h + 1], (c, c))
        beta_col = jnp.broadcast_to(sm[:, h:h + 1], (c, c))
        decay = jnp.exp(jnp.where(tril, gc_col - gc_row, NEG_BIG))
        kf = cols(gdn_ref, 1, blk, h).astype(F32)
        egc = jnp.exp(gc_col)
        lmats[p] = jnp.where(strict, gdn_kq[p][:c] * beta_col * decay, 0.0)
        attns[p] = jnp.where(tril, gdn_kq[p][c:] * decay, 0.0).astype(BF16)
        rhs[p] = jnp.concatenate([(cols(gdn_ref, 2, blk, h).astype(F32) * beta_col).astype(BF16),
                                  (kf * (beta_col * egc)).astype(BF16)], axis=1)
        kts[p] = (kf * jnp.exp(g_last - gc_col)).T.astype(BF16)
        egcs[p] = egc
        glasts[p] = g_last

    def same_block(size):
        return (row // size) == (col // size)

    diag = same_block(INV_BASE)
    tinv, power = {}, {}
    for p in probs:
        ld = jnp.where(diag, lmats[p], 0.0)
        tinv[p] = eye - ld
        power[p] = _mm(ld, ld)
    factors = int(math.log2(INV_BASE)) - 1
    for f in range(factors):
        last = f == factors - 1
        for p in probs:
            pw = power[p].astype(BF16)
            if last:
                tinv[p] = tinv[p] + _dot(tinv[p].astype(BF16), pw)
            else:
                both = _dot(jnp.concatenate([tinv[p].astype(BF16), pw], axis=0), pw)
                tinv[p] = tinv[p] + both[:c]
                power[p] = both[c:]
    size = INV_BASE
    while size < c:
        off = same_block(2 * size) & jnp.logical_not(same_block(size))
        ct = {p: _dot(jnp.where(off, lmats[p], 0.0).astype(BF16), tinv[p].astype(BF16)) for p in probs}
        for p in probs:
            tinv[p] = tinv[p] - _dot(tinv[p].astype(BF16), ct[p].astype(BF16))
        size *= 2

    ret_inter = {}
    for h in range(HEADS):
        lg = RET_LOG_GAMMA[h]
        state = sret_ref[h]
        for blk in range(nblk):
            ret_inter[(blk, h)] = _dot(cols(ret_ref, 0, blk, h), state.astype(BF16))
            state = state * math.exp(lg * c) + ret_iv[(blk, h)][c:]
        sret_ref[h] = state

    uw = {p: _dot(tinv[p].astype(BF16), rhs[p]).astype(BF16) for p in probs}
    apn = {}
    for p in probs:
        wu = jnp.concatenate([uw[p][:, HEAD_DIM:], uw[p][:, :HEAD_DIM]], axis=1)
        apn[p] = _dot(jnp.concatenate([attns[p], kts[p]], axis=0), wu)

    for p in probs:
        blk, h = p
        lg = RET_LOG_GAMMA[h]
        o = ret_iv[p][:c] + ret_inter[p] * jnp.exp(lg * (rowf + 1.0))
        y = _rms(o) * _silu(cols(ret_ref, 3, blk, h).astype(F32))
        ybuf_ref[blk * c:(blk + 1) * c, h * HEAD_DIM:(h + 1) * HEAD_DIM] = y.astype(BF16)

    states = [sgdn_ref[h] for h in range(HEADS)]
    for blk in range(nblk):
        for h in range(HEADS):
            p = (blk, h)
            q_eff = (cols(gdn_ref, 0, blk, h).astype(F32) * egcs[p] - apn[p][:c, :HEAD_DIM]).astype(BF16)
            pq = jnp.concatenate([apn[p][c:, :HEAD_DIM].astype(BF16), q_eff], axis=0)
            ps = _dot(pq, states[h].astype(BF16))
            o = ps[c:] + apn[p][:c, HEAD_DIM:]
            states[h] = jnp.exp(glasts[p]) * states[h] - ps[:c] + apn[p][c:, HEAD_DIM:]
            y = _rms(o) * ogain * _silu(cols(gdn_ref, 3, blk, h).astype(F32))
            ybuf_ref[blk * c:(blk + 1) * c, width + h * HEAD_DIM:width + (h + 1) * HEAD_DIM] = y.astype(BF16)
    for h in range(HEADS):
        sgdn_ref[h] = states[h]

    out_ref[0] = x_ref[0] + _dot(ybuf_ref[...], wout_ref[...])


def _rglru_kernel(h_ref, gain_ref, win_ref, convw_ref, convb_ref, wax_ref, ba_ref, bx_ref,
                  lam_ref, wout_ref, out_ref, halo_ref, state_ref, a_ref, u_ref, hs_ref):
    ti = pl.program_id(0)
    rows = h_ref.shape[0]
    d_rnn = lam_ref.shape[1]
    steps = rows // SUBLANES
    halo = (LRU_CONV - 1) * SUBLANES

    @pl.when(ti == 0)
    def _():
        halo_ref[...] = jnp.zeros(halo_ref.shape, F32)
        state_ref[...] = jnp.zeros(state_ref.shape, F32)

    h_in = h_ref[...]
    hb = (_rms(h_in) * gain_ref[...]).astype(BF16)
    gate = _dot(hb, win_ref[:, 0:d_rnn])
    xpre = _dot(hb, win_ref[:, d_rnn:2 * d_rnn])
    prev = halo_ref[...]
    cw = convw_ref[...]
    xr = xpre * cw[LRU_CONV - 1:LRU_CONV, :] + convb_ref[...]
    for j in range(LRU_CONV - 1):
        back = (LRU_CONV - 1 - j) * SUBLANES
        shifted = jnp.concatenate([prev[halo - back:, :], xpre[:rows - back, :]], axis=0)
        xr = xr + shifted * cw[j:j + 1, :]
    halo_ref[...] = xpre[rows - halo:, :]

    sp = _softplus(-lam_ref[...])
    xrb = xr.astype(BF16)
    nblock = d_rnn // LRU_BLOCK
    ris = [_dot(xrb[:, n * LRU_BLOCK:(n + 1) * LRU_BLOCK], wax_ref[n]) for n in range(nblock)]
    for n in range(nblock):
        c0 = n * LRU_BLOCK
        ri = ris[n]
        r = _sigmoid(ri[:, :LRU_BLOCK] + ba_ref[:, c0:c0 + LRU_BLOCK])
        i = _sigmoid(ri[:, LRU_BLOCK:] + bx_ref[:, c0:c0 + LRU_BLOCK])
        log_a = (-LRU_C) * r * sp[:, c0:c0 + LRU_BLOCK]
        a_ref[:, c0:c0 + LRU_BLOCK] = jnp.exp(log_a)
        u_ref[:, c0:c0 + LRU_BLOCK] = jnp.sqrt(1.0 - jnp.exp(2.0 * log_a)) * (i * xr[:, c0:c0 + LRU_BLOCK])

    state = state_ref[...]
    for t in range(steps):
        sl = slice(t * SUBLANES, (t + 1) * SUBLANES)
        state = a_ref[sl, :] * state + u_ref[sl, :]
        hs_ref[sl, :] = state
    state_ref[...] = state

    y = (_gelu_tanh(gate) * hs_ref[...]).astype(BF16)
    out_ref[...] = h_in + _dot(y, wout_ref[...])


def _ffn_kernel(h_ref, gain_ref, wup_ref, convw_ref, convb_ref, wdown_ref, fgain_ref, out_ref,
                halo_ref, slab_ref, *, d_ff, col_chunk, batch_major_in, final):
    ti = pl.program_id(0)
    halo = (FFN_CONV - 1) * SUBLANES

    @pl.when(ti == 0)
    def _():
        halo_ref[...] = jnp.zeros(halo_ref.shape, F32)

    if batch_major_in:
        steps = h_ref.shape[1]
        nslab = h_ref.shape[2] // LANES
        for s in range(nslab):
            for b in range(SUBLANES):
                slab_ref[s, pl.ds(b, steps, stride=SUBLANES), :] = h_ref[b, :, s * LANES:(s + 1) * LANES]
        h_in = jnp.concatenate([slab_ref[s] for s in range(nslab)], axis=1)
    else:
        h_in = h_ref[...]
    rows = h_in.shape[0]
    hb = (_rms(h_in) * gain_ref[...]).astype(BF16)

    def conv(cur, c0):
        prev = halo_ref[:, c0:c0 + col_chunk]
        cw = convw_ref[:, c0:c0 + col_chunk]
        out = cur * cw[FFN_CONV - 1:FFN_CONV, :] + convb_ref[:, c0:c0 + col_chunk]
        for j in range(FFN_CONV - 1):
            back = (FFN_CONV - 1 - j) * SUBLANES
            shifted = jnp.concatenate([prev[halo - back:, :], cur[:rows - back, :]], axis=0)
            out = out + shifted * cw[j:j + 1, :]
        halo_ref[:, c0:c0 + col_chunk] = cur[rows - halo:, :]
        return out

    def up(j):
        g0 = j * col_chunk
        v0 = d_ff + j * col_chunk
        return _dot(hb, wup_ref[:, g0:g0 + col_chunk]), _dot(hb, wup_ref[:, v0:v0 + col_chunk])

    nchunk = d_ff // col_chunk
    acc = jnp.zeros(h_in.shape, F32)
    cur = up(0)
    for j in range(nchunk):
        nxt = up(j + 1) if j + 1 < nchunk else None
        g0 = j * col_chunk
        gate = conv(cur[0], g0)
        val = conv(cur[1], d_ff + g0)
        act = (_silu(gate) * val).astype(BF16)
        acc = acc + _dot(act, wdown_ref[g0:g0 + col_chunk, :])
        cur = nxt
    res = h_in + acc
    if not final:
        out_ref[...] = res
    else:
        fin = _rms(res) * fgain_ref[...]
        steps = rows // SUBLANES
        for s in range(fin.shape[1] // LANES):
            slab_ref[s] = fin[:, s * LANES:(s + 1) * LANES]
        for s in range(fin.shape[1] // LANES):
            for b in range(SUBLANES):
                out_ref[b, :, s * LANES:(s + 1) * LANES] = slab_ref[s, pl.ds(b, steps, stride=SUBLANES), :]


def _const_spec(shape):
    zeros = (0,) * len(shape)
    return pl.BlockSpec(shape, lambda *_: zeros)


def _params(sem, flags=None):
    return pltpu.CompilerParams(dimension_semantics=sem, vmem_limit_bytes=VMEM_LIMIT, flags=flags)


def _rope_tables(t):
    half = HEAD_DIM // 2
    inv_freq = ROPE_BASE ** (-jnp.arange(half, dtype=F32) / half)
    ang = jnp.arange(t, dtype=F32)[:, None] * inv_freq[None, :]
    cos = jnp.cos(ang)
    sin = jnp.sin(ang)
    return jnp.concatenate([cos, cos], axis=-1), jnp.concatenate([-sin, sin], axis=-1)


def _layer0_inproj(x, gain, w_in, conv_w, a_log, dt_bias, tile):
    b, t, d = x.shape
    width = HEADS * HEAD_DIM
    wmain = w_in[:, :8 * width].astype(BF16)
    wsmall = jnp.zeros((d, LANES), F32).at[:, :2 * HEADS].set(w_in[:, 8 * width:]).astype(BF16)
    alog = jnp.zeros((1, LANES), F32).at[0, HEADS:2 * HEADS].set(a_log)
    dtb = jnp.zeros((1, LANES), F32).at[0, HEADS:2 * HEADS].set(dt_bias)
    cos, sin = _rope_tables(t)
    tiles_per_seq = t // tile
    ntiles = b * tiles_per_seq

    def in_tile(s):
        s = jnp.minimum(s, ntiles - 1)
        return s // tiles_per_seq, s % tiles_per_seq

    def out_tile(s):
        s = jnp.maximum(s - 1, 0)
        return s // tiles_per_seq, s % tiles_per_seq

    out_map = lambda s: (*out_tile(s), 0)
    pbuf = pltpu.VMEM((SUBLANES + tile, 8 * width + LANES), F32)
    return pl.pallas_call(
        functools.partial(_inproj_kernel, tiles_per_seq=tiles_per_seq),
        grid=(ntiles + 1,),
        in_specs=[
            pl.BlockSpec((1, tile, d), lambda s: (*in_tile(s), 0)),
            _const_spec((1, d)),
            _const_spec((d, 8 * width)),
            _const_spec((d, LANES)),
            pl.BlockSpec((tile, HEAD_DIM), lambda s: (out_tile(s)[1], 0)),
            pl.BlockSpec((tile, HEAD_DIM), lambda s: (out_tile(s)[1], 0)),
            _const_spec((GDN_CONV, 3 * width)),
            _const_spec((1, LANES)),
            _const_spec((1, LANES)),
        ],
        out_specs=[
            pl.BlockSpec((1, tile, 4 * width), out_map),
            pl.BlockSpec((1, tile, 4 * width), out_map),
            pl.BlockSpec((1, tile, LANES), out_map),
        ],
        out_shape=[
            jax.ShapeDtypeStruct((b, t, 4 * width), BF16),
            jax.ShapeDtypeStruct((b, t, 4 * width), BF16),
            jax.ShapeDtypeStruct((b, t, LANES), F32),
        ],
        scratch_shapes=[pbuf, pbuf, pltpu.VMEM((tile, d), BF16)],
        compiler_params=_params(("arbitrary",)),
        name="l0_inproj",
    )(x, gain.reshape(1, d), wmain, wsmall, cos, sin, conv_w, alog, dtb)


def _layer0_mixer(ret, gdn, small, x, w_out, out_gain, tile):
    b, t, d = x.shape
    width = HEADS * HEAD_DIM
    return pl.pallas_call(
        _mixer_kernel,
        grid=(b, t // tile),
        in_specs=[
            pl.BlockSpec((1, tile, 4 * width), lambda i, j: (i, j, 0)),
            pl.BlockSpec((1, tile, 4 * width), lambda i, j: (i, j, 0)),
            pl.BlockSpec((1, tile, LANES), lambda i, j: (i, j, 0)),
            pl.BlockSpec((1, tile, d), lambda i, j: (i, j, 0)),
            _const_spec((2 * width, d)),
            _const_spec((1, HEAD_DIM)),
        ],
        out_specs=pl.BlockSpec((1, tile, d), lambda i, j: (i, j, 0)),
        out_shape=jax.ShapeDtypeStruct((b, t, d), F32),
        scratch_shapes=[
            pltpu.VMEM((HEADS, HEAD_DIM, HEAD_DIM), F32),
            pltpu.VMEM((HEADS, HEAD_DIM, HEAD_DIM), F32),
            pltpu.VMEM((tile, 2 * width), BF16),
        ],
        compiler_params=_params(("arbitrary", "arbitrary")),
        name="l0_mixer",
    )(ret, gdn, small, x, w_out.astype(BF16), out_gain.reshape(1, HEAD_DIM))


def _rglru(h_tm, gain, w_in, conv_w, conv_b, w_a, b_a, w_x, b_x, lam, w_out, steps):
    n, d = h_tm.shape
    d_rnn = lam.shape[0]
    rows = steps * SUBLANES
    wax = jnp.concatenate([w_a, w_x], axis=-1).astype(BF16)
    return pl.pallas_call(
        _rglru_kernel,
        grid=(n // rows,),
        in_specs=[
            pl.BlockSpec((rows, d), lambda i: (i, 0)),
            _const_spec((1, d)),
            _const_spec((d, 2 * d_rnn)),
            _const_spec((LRU_CONV, d_rnn)),
            _const_spec((1, d_rnn)),
            _const_spec(wax.shape),
            _const_spec((1, d_rnn)),
            _const_spec((1, d_rnn)),
            _const_spec((1, d_rnn)),
            _const_spec((d_rnn, d)),
        ],
        out_specs=pl.BlockSpec((rows, d), lambda i: (i, 0)),
        out_shape=jax.ShapeDtypeStruct((n, d), F32),
        scratch_shapes=[
            pltpu.VMEM(((LRU_CONV - 1) * SUBLANES, d_rnn), F32),
            pltpu.VMEM((SUBLANES, d_rnn), F32),
            pltpu.VMEM((rows, d_rnn), F32),
            pltpu.VMEM((rows, d_rnn), F32),
            pltpu.VMEM((rows, d_rnn), F32),
        ],
        compiler_params=_params(("arbitrary",)),
        name="l1_rglru",
    )(h_tm, gain.reshape(1, d), w_in.astype(BF16), conv_w, conv_b.reshape(1, d_rnn), wax,
      b_a.reshape(1, d_rnn), b_x.reshape(1, d_rnn), lam.reshape(1, d_rnn), w_out.astype(BF16))


def _ffn(h, gain, w_up, conv_w, conv_b, w_down, final_gain, steps, batch_major_in, final):
    d = h.shape[-1]
    n = h.size // d
    d_ff = w_down.shape[0]
    rows = steps * SUBLANES
    col_chunk = 2 * LANES
    kern = functools.partial(_ffn_kernel, d_ff=d_ff, col_chunk=col_chunk,
                             batch_major_in=batch_major_in, final=final)
    scratch = [pltpu.VMEM(((FFN_CONV - 1) * SUBLANES, 2 * d_ff), F32),
               pltpu.VMEM((d // LANES, rows, LANES), F32)]
    bm_spec = pl.BlockSpec((SUBLANES, steps, d), lambda i: (0, i, 0))
    tm_spec = pl.BlockSpec((rows, d), lambda i: (i, 0))
    if final:
        out_spec = bm_spec
        out_shape = jax.ShapeDtypeStruct((SUBLANES, n // SUBLANES, d), F32)
    else:
        out_spec = tm_spec
        out_shape = jax.ShapeDtypeStruct((n, d), F32)
    return pl.pallas_call(
        kern,
        grid=(n // rows,),
        in_specs=[
            bm_spec if batch_major_in else tm_spec,
            _const_spec((1, d)),
            _const_spec((d, 2 * d_ff)),
            _const_spec((FFN_CONV, 2 * d_ff)),
            _const_spec((1, 2 * d_ff)),
            _const_spec((d_ff, d)),
            _const_spec((1, d)),
        ],
        out_specs=out_spec,
        out_shape=out_shape,
        scratch_shapes=scratch,
        compiler_params=_params(("arbitrary",)),
        name="ffn_final" if final else "ffn",
    )(h, gain.reshape(1, d), w_up.astype(BF16), conv_w, conv_b.reshape(1, 2 * d_ff),
      w_down.astype(BF16), final_gain.reshape(1, d))


def kernel(x, norm_mix, norm_ffn, ret_gdn_w_in, gdn_conv_w, gdn_a_log, gdn_dt_bias, gdn_out_gain,
           ret_gdn_w_out, lru_w_in, lru_conv_w, lru_conv_b, lru_w_a, lru_b_a, lru_w_x, lru_b_x,
           lru_lambda, lru_w_out, ffn_w_up, ffn_conv_w, ffn_conv_b, ffn_w_down, norm_final):
    b, t, d = x.shape
    assert b == SUBLANES and norm_mix.shape[0] == 2
    tile_in = 256
    tile_mix = 512
    steps = 32
    steps_lru = 64

    ret, gdn, small = _layer0_inproj(x, norm_mix[0], ret_gdn_w_in[0], gdn_conv_w[0], gdn_a_log[0],
                                     gdn_dt_bias[0], tile_in)
    h = _layer0_mixer(ret, gdn, small, x, ret_gdn_w_out[0], gdn_out_gain[0], tile_mix)
    h = _ffn(h, norm_ffn[0], ffn_w_up[0], ffn_conv_w[0], ffn_conv_b[0], ffn_w_down[0], norm_final,
             steps, batch_major_in=True, final=False)
    h = _rglru(h, norm_mix[1], lru_w_in[0], lru_conv_w[0], lru_conv_b[0], lru_w_a[0], lru_b_a[0],
               lru_w_x[0], lru_b_x[0], lru_lambda[0], lru_w_out[0], steps_lru)
    return _ffn(h, norm_ffn[1], ffn_w_up[1], ffn_conv_w[1], ffn_conv_b[1], ffn_w_down[1], norm_final,
                steps, batch_major_in=False, final=True)
```

```python
import functools
import math

import numpy as np
import jax
import jax.numpy as jnp
from jax import lax
from jax.experimental import pallas as pl
from jax.experimental.pallas import tpu as pltpu

F32 = jnp.float32
BF16 = jnp.bfloat16

EPS = 1e-6
ROPE_BASE = 10000.0
HEADS = 4
HEAD_DIM = 128
CHUNK = 128
STAGE_ROWS = 128
ROW_CHUNK = 64
INV_BASE = 16
GDN_CONV = 4
LRU_CONV = 4
FFN_CONV = 3
LRU_C = 8.0
LRU_BLOCK = 128
SUBLANES = 8
LANES = 128
NEG_BIG = -1e30
VMEM_LIMIT = 56 * 1024 * 1024

RET_LOG_GAMMA = tuple(math.log1p(-2.0 ** (-5.0 - h)) for h in range(HEADS))


def _rms(x):
    return x * lax.rsqrt(jnp.mean(x * x, axis=-1, keepdims=True) + EPS)


def _sigmoid(x):
    return 1.0 / (1.0 + jnp.exp(-x))


def _silu(x):
    return x * _sigmoid(x)


def _softplus(x):
    return jnp.maximum(x, 0.0) + jnp.log1p(jnp.exp(-jnp.abs(x)))


def _gelu_tanh(x):
    c = math.sqrt(2.0 / math.pi)
    return 0.5 * x * (1.0 + jnp.tanh(c * (x + 0.044715 * (x * x * x))))


def _dot(a, b):
    return jnp.dot(a, b, preferred_element_type=F32)


def _dot_nt(a, b):
    return lax.dot_general(a, b, (((1,), (1,)), ((), ())), preferred_element_type=F32)


def _mm(a, b):
    return _dot(a.astype(BF16), b.astype(BF16))


def _split3(x):
    hi = x.astype(BF16)
    r1 = x - hi.astype(F32)
    mid = r1.astype(BF16)
    lo = (r1 - mid.astype(F32)).astype(BF16)
    return hi, mid, lo


def _stage_bf16(src, dst_ref, stage_ref, sem_ref, row_chunk):
    rows, cols = src.shape
    nchunk = rows // row_chunk

    def copy(i):
        return pltpu.make_async_copy(src.at[pl.ds(i * row_chunk, row_chunk), :],
                                     stage_ref.at[i % 2, pl.ds(0, row_chunk), pl.ds(0, cols)],
                                     sem_ref.at[i % 2])

    copy(0).start()
    for i in range(nchunk):
        if i + 1 < nchunk:
            copy(i + 1).start()
        copy(i).wait()
        dst_ref[pl.ds(i * row_chunk, row_chunk), :] = stage_ref[i % 2, 0:row_chunk, 0:cols].astype(BF16)


def _inproj_kernel(x_ref, gain_ref, win_hbm, wsmall_ref, cos_ref, sin_ref, convw_ref,
                   alog_ref, dtb_ref, ret_ref, gdn_ref, small_ref, pbuf0_ref, pbuf1_ref, hb_ref,
                   wmain_ref, stage_ref, sem_ref, *, tiles_per_seq):
    s = pl.program_id(0)
    tile = x_ref.shape[1]
    width = HEADS * HEAD_DIM
    nmain = 8 * width
    body = slice(SUBLANES, SUBLANES + tile)
    qkv = slice(4 * width, 7 * width)

    @pl.when(s == 0)
    def _():
        pbuf1_ref[...] = jnp.zeros(pbuf1_ref.shape, F32)
        _stage_bf16(win_hbm.at[0, :, pl.ds(0, 8 * HEADS * HEAD_DIM)], wmain_ref, stage_ref, sem_ref, STAGE_ROWS)

    def step(pa_ref, pb_ref):
        for r in range(tile // ROW_CHUNK):
            rows = slice(r * ROW_CHUNK, (r + 1) * ROW_CHUNK)
            hb_ref[rows, :] = (_rms(x_ref[0, rows, :]) * gain_ref[...]).astype(BF16)
        for c0, c1 in ((0, 2 * width), (2 * width, 4 * width), (4 * width, 7 * width), (7 * width, nmain)):
            pa_ref[body, c0:c1] = _dot(hb_ref[...], wmain_ref[:, c0:c1])
        pa_ref[body, nmain:nmain + LANES] = _dot(hb_ref[...], wsmall_ref[...])

        first_of_seq = lax.rem(jnp.maximum(s - 1, 0), tiles_per_seq) == 0
        pb_ref[0:SUBLANES, qkv] = jnp.where(first_of_seq, 0.0, pb_ref[0:SUBLANES, qkv])
        for r in range(tile // ROW_CHUNK):
            out_rows = slice(r * ROW_CHUNK, (r + 1) * ROW_CHUNK)
            rows = slice(SUBLANES + r * ROW_CHUNK, SUBLANES + (r + 1) * ROW_CHUNK)
            cos = cos_ref[out_rows, :]
            sin = sin_ref[out_rows, :]
            for part in range(2):
                for h in range(HEADS):
                    c0 = part * width + h * HEAD_DIM
                    ph = pb_ref[rows, c0:c0 + HEAD_DIM]
                    rot = ph * cos + pltpu.roll(ph, HEAD_DIM // 2, 1) * sin
                    if part == 1:
                        rot = rot * (HEAD_DIM ** -0.5)
                    ret_ref[0, out_rows, c0:c0 + HEAD_DIM] = rot.astype(BF16)
            for c0 in range(2 * width, 4 * width, LANES):
                ret_ref[0, out_rows, c0:c0 + LANES] = pb_ref[rows, c0:c0 + LANES].astype(BF16)
            for c0 in range(0, width, LANES):
                gdn_ref[0, out_rows, 3 * width + c0:3 * width + c0 + LANES] = (
                    pb_ref[rows, 7 * width + c0:7 * width + c0 + LANES].astype(BF16))
            for part in range(3):
                for h in range(HEADS):
                    c0 = part * width + h * HEAD_DIM
                    src_cols = slice(4 * width + c0, 4 * width + c0 + HEAD_DIM)
                    cw = convw_ref[:, c0:c0 + HEAD_DIM]
                    acc = pb_ref[rows, src_cols] * cw[GDN_CONV - 1:GDN_CONV, :]
                    for j in range(GDN_CONV - 1):
                        back = GDN_CONV - 1 - j
                        shifted = pb_ref[rows.start - back:rows.stop - back, src_cols]
                        acc = acc + shifted * cw[j:j + 1, :]
                    act = _silu(acc)
                    if part < 2:
                        act = act * lax.rsqrt(jnp.sum(act * act, axis=-1, keepdims=True) + EPS)
                    if part == 0:
                        act = act * (HEAD_DIM ** -0.5)
                    gdn_ref[0, out_rows, c0:c0 + HEAD_DIM] = act.astype(BF16)
            sm = pb_ref[rows, nmain:nmain + LANES]
            col = lax.broadcasted_iota(jnp.int32, sm.shape, 1)
            beta = _sigmoid(sm)
            g = -jnp.exp(alog_ref[...]) * _softplus(sm + dtb_ref[...])
            small_ref[0, out_rows, :] = jnp.where(col < HEADS, beta, g)
        pa_ref[0:SUBLANES, qkv] = pb_ref[tile:tile + SUBLANES, qkv]

    parity = lax.rem(s, 2)

    @pl.when(parity == 0)
    def _():
        step(pbuf0_ref, pbuf1_ref)

    @pl.when(parity == 1)
    def _():
        step(pbuf1_ref, pbuf0_ref)


def _mixer_kernel(ret_ref, gdn_ref, small_ref, x_ref, wout_hbm, ogain_ref, out_ref,
                  sret_ref, sgdn_ref, ybuf_ref, wout_ref, stage_ref, sem_ref):
    ti = pl.program_id(1)

    @pl.when((pl.program_id(0) == 0) & (ti == 0))
    def _():
        _stage_bf16(wout_hbm.at[0], wout_ref, stage_ref, sem_ref, STAGE_ROWS)

    tile = x_ref.shape[1]
    width = HEADS * HEAD_DIM
    c = CHUNK

    @pl.when(ti == 0)
    def _():
        sret_ref[...] = jnp.zeros(sret_ref.shape, F32)
        sgdn_ref[...] = jnp.zeros(sgdn_ref.shape, F32)

    row = lax.broadcasted_iota(jnp.int32, (c, c), 0)
    col = lax.broadcasted_iota(jnp.int32, (c, c), 1)
    tril = row >= col
    strict = row > col
    rowf = row.astype(F32)
    relf = (row - col).astype(F32)
    eye = jnp.where(row == col, 1.0, 0.0).astype(F32)
    tri_b = jnp.where(tril, 1.0, 0.0).astype(BF16)
    ogain = ogain_ref[...]
    nblk = tile // c
    probs = [(blk, h) for blk in range(nblk) for h in range(HEADS)]

    def cols(ref, part, blk, h):
        c0 = part * width + h * HEAD_DIM
        return ref[0, blk * c:(blk + 1) * c, c0:c0 + HEAD_DIM]


    sms, gcums, gcum_ts = [], [], []
    for blk in range(nblk):
        sm = small_ref[0, blk * c:(blk + 1) * c, :]
        parts = _dot(tri_b, jnp.concatenate(_split3(sm), axis=1))
        gcum = parts[:, :LANES] + parts[:, LANES:2 * LANES] + parts[:, 2 * LANES:]
        sms.append(sm)
        gcums.append(gcum)
        gcum_ts.append(gcum.T)
    ret_qk = {}
    gdn_kq = {}
    for p in probs:
        blk, h = p
        ret_qk[p] = _dot_nt(cols(ret_ref, 0, blk, h), cols(ret_ref, 1, blk, h))
        k = cols(gdn_ref, 1, blk, h)
        gdn_kq[p] = _dot_nt(jnp.concatenate([k, cols(gdn_ref, 0, blk, h)], axis=0), k)

    ret_iv = {}
    for p in probs:
        blk, h = p
        lg = RET_LOG_GAMMA[h]
        dmask = jnp.where(tril, jnp.exp(lg * jnp.where(tril, relf, 0.0)), 0.0)
        scores = (ret_qk[p] * dmask).astype(BF16)
        k_tail_t = (cols(ret_ref, 1, blk, h).astype(F32) * jnp.exp(lg * (c - 1.0 - rowf))).T.astype(BF16)
        ret_iv[p] = _dot(jnp.concatenate([scores, k_tail_t], axis=0), cols(ret_ref, 2, blk, h))

    lmats, attns, rhs, kts, egcs, glasts = {}, {}, {}, {}, {}, {}
    for p in probs:
        blk, h = p
        gcum, sm = gcums[blk], sms[blk]
        gc_col = jnp.broadcast_to(gcum[:, HEADS + h:HEADS + h + 1], (c, c))
        gc_row = jnp.broadcast_to(gcum_ts[blk][HEADS + h:HEADS + h + 1, :], (c, c))
        g_last = jnp.broadcast_to(gcum[c - 1:c, HEADS + h:HEADS + h + 1], (c, c))
        beta_col = jnp.broadcast_to(sm[:, h:h + 1], (c, c))
        decay = jnp.exp(jnp.where(tril, gc_col - gc_row, NEG_BIG))
        kf = cols(gdn_ref, 1, blk, h).astype(F32)
        egc = jnp.exp(gc_col)
        lmats[p] = jnp.where(strict, gdn_kq[p][:c] * beta_col * decay, 0.0)
        attns[p] = jnp.where(tril, gdn_kq[p][c:] * decay, 0.0).astype(BF16)
        rhs[p] = jnp.concatenate([(cols(gdn_ref, 2, blk, h).astype(F32) * beta_col).astype(BF16),
                                  (kf * (beta_col * egc)).astype(BF16)], axis=1)
        kts[p] = (kf * jnp.exp(g_last - gc_col)).T.astype(BF16)
        egcs[p] = egc
        glasts[p] = g_last

    def same_block(size):
        return (row // size) == (col // size)

    diag = same_block(INV_BASE)
    tinv, power = {}, {}
    for p in probs:
        ld = jnp.where(diag, lmats[p], 0.0)
        tinv[p] = eye - ld
        power[p] = _mm(ld, ld)
    factors = int(math.log2(INV_BASE)) - 1
    for f in range(factors):
        last = f == factors - 1
        for p in probs:
            pw = power[p].astype(BF16)
            if last:
                tinv[p] = tinv[p] + _dot(tinv[p].astype(BF16), pw)
            else:
                both = _dot(jnp.concatenate([tinv[p].astype(BF16), pw], axis=0), pw)
                tinv[p] = tinv[p] + both[:c]
                power[p] = both[c:]
    size = INV_BASE
    while size < c:
        off = same_block(2 * size) & jnp.logical_not(same_block(size))
        ct = {p: _dot(jnp.where(off, lmats[p], 0.0).astype(BF16), tinv[p].astype(BF16)) for p in probs}
        for p in probs:
            tinv[p] = tinv[p] - _dot(tinv[p].astype(BF16), ct[p].astype(BF16))
        size *= 2

    ret_inter = {}
    for h in range(HEADS):
        lg = RET_LOG_GAMMA[h]
        state = sret_ref[h]
        for blk in range(nblk):
            ret_inter[(blk, h)] = _dot(cols(ret_ref, 0, blk, h), state.astype(BF16))
            state = state * math.exp(lg * c) + ret_iv[(blk, h)][c:]
        sret_ref[h] = state

    uw = {p: _dot(tinv[p].astype(BF16), rhs[p]).astype(BF16) for p in probs}
    apn = {}
    for p in probs:
        wu = jnp.concatenate([uw[p][:, HEAD_DIM:], uw[p][:, :HEAD_DIM]], axis=1)
        apn[p] = _dot(jnp.concatenate([attns[p], kts[p]], axis=0), wu)

    for p in probs:
        blk, h = p
        lg = RET_LOG_GAMMA[h]
        o = ret_iv[p][:c] + ret_inter[p] * jnp.exp(lg * (rowf + 1.0))
        y = _rms(o) * _silu(cols(ret_ref, 3, blk, h).astype(F32))
        ybuf_ref[blk * c:(blk + 1) * c, h * HEAD_DIM:(h + 1) * HEAD_DIM] = y.astype(BF16)

    states = [sgdn_ref[h] for h in range(HEADS)]
    for blk in range(nblk):
        for h in range(HEADS):
            p = (blk, h)
            q_eff = (cols(gdn_ref, 0, blk, h).astype(F32) * egcs[p] - apn[p][:c, :HEAD_DIM]).astype(BF16)
            pq = jnp.concatenate([apn[p][c:, :HEAD_DIM].astype(BF16), q_eff], axis=0)
            ps = _dot(pq, states[h].astype(BF16))
            o = ps[c:] + apn[p][:c, HEAD_DIM:]
            states[h] = jnp.exp(glasts[p]) * states[h] - ps[:c] + apn[p][c:, HEAD_DIM:]
            y = _rms(o) * ogain * _silu(cols(gdn_ref, 3, blk, h).astype(F32))
            ybuf_ref[blk * c:(blk + 1) * c, width + h * HEAD_DIM:width + (h + 1) * HEAD_DIM] = y.astype(BF16)
    for h in range(HEADS):
        sgdn_ref[h] = states[h]

    out_ref[0] = x_ref[0] + _dot(ybuf_ref[...], wout_ref[...])


def _rglru_kernel(h_ref, gain_ref, win_hbm, convw_ref, convb_ref, wa_hbm, wx_hbm, ba_ref, bx_ref,
                  lam_ref, wout_hbm, out_ref, halo_ref, state_ref, a_ref, u_ref, hs_ref,
                  win_ref, wout_ref, wax_ref, stage_ref, sem_ref):
    ti = pl.program_id(0)
    rows = h_ref.shape[0]
    d_rnn = lam_ref.shape[1]
    steps = rows // SUBLANES
    halo = (LRU_CONV - 1) * SUBLANES

    @pl.when(ti == 0)
    def _():
        halo_ref[...] = jnp.zeros(halo_ref.shape, F32)
        state_ref[...] = jnp.zeros(state_ref.shape, F32)
        _stage_bf16(win_hbm.at[0], win_ref, stage_ref, sem_ref, STAGE_ROWS)
        _stage_bf16(wout_hbm.at[0], wout_ref, stage_ref, sem_ref, STAGE_ROWS)
        for n in range(wax_ref.shape[0]):
            _stage_bf16(wa_hbm.at[0, n], wax_ref.at[n, :, pl.ds(0, LRU_BLOCK)], stage_ref, sem_ref, LRU_BLOCK)
            _stage_bf16(wx_hbm.at[0, n], wax_ref.at[n, :, pl.ds(LRU_BLOCK, LRU_BLOCK)], stage_ref, sem_ref, LRU_BLOCK)

    h_in = h_ref[...]
    hb = (_rms(h_in) * gain_ref[...]).astype(BF16)
    gate = _dot(hb, win_ref[:, 0:d_rnn])
    xpre = _dot(hb, win_ref[:, d_rnn:2 * d_rnn])
    prev = halo_ref[...]
    cw = convw_ref[...]
    xr = xpre * cw[LRU_CONV - 1:LRU_CONV, :] + convb_ref[...]
    for j in range(LRU_CONV - 1):
        back = (LRU_CONV - 1 - j) * SUBLANES
        shifted = jnp.concatenate([prev[halo - back:, :], xpre[:rows - back, :]], axis=0)
        xr = xr + shifted * cw[j:j + 1, :]
    halo_ref[...] = xpre[rows - halo:, :]

    sp = _softplus(-lam_ref[...])
    xrb = xr.astype(BF16)
    nblock = d_rnn // LRU_BLOCK
    ris = [_dot(xrb[:, n * LRU_BLOCK:(n + 1) * LRU_BLOCK], wax_ref[n]) for n in range(nblock)]
    for n in range(nblock):
        c0 = n * LRU_BLOCK
        ri = ris[n]
        r = _sigmoid(ri[:, :LRU_BLOCK] + ba_ref[:, c0:c0 + LRU_BLOCK])
        i = _sigmoid(ri[:, LRU_BLOCK:] + bx_ref[:, c0:c0 + LRU_BLOCK])
        log_a = (-LRU_C) * r * sp[:, c0:c0 + LRU_BLOCK]
        a_ref[:, c0:c0 + LRU_BLOCK] = jnp.exp(log_a)
        u_ref[:, c0:c0 + LRU_BLOCK] = jnp.sqrt(1.0 - jnp.exp(2.0 * log_a)) * (i * xr[:, c0:c0 + LRU_BLOCK])

    state = state_ref[...]
    for t in range(steps):
        sl = slice(t * SUBLANES, (t + 1) * SUBLANES)
        state = a_ref[sl, :] * state + u_ref[sl, :]
        hs_ref[sl, :] = state
    state_ref[...] = state

    y = (_gelu_tanh(gate) * hs_ref[...]).astype(BF16)
    out_ref[...] = h_in + _dot(y, wout_ref[...])


def _ffn_kernel(h_ref, gain_ref, wup_hbm, convw_ref, convb_ref, wdown_hbm, fgain_ref, out_ref,
                halo_ref, slab_ref, wup_ref, wdown_ref, stage_ref, sem_ref,
                *, layer, d_ff, col_chunk, batch_major_in, final):
    ti = pl.program_id(0)
    halo = (FFN_CONV - 1) * SUBLANES

    @pl.when(ti == 0)
    def _():
        halo_ref[...] = jnp.zeros(halo_ref.shape, F32)
        _stage_bf16(wup_hbm.at[layer], wup_ref, stage_ref, sem_ref, STAGE_ROWS)
        _stage_bf16(wdown_hbm.at[layer], wdown_ref, stage_ref, sem_ref, STAGE_ROWS)

    if batch_major_in:
        steps = h_ref.shape[1]
        nslab = h_ref.shape[2] // LANES
        for s in range(nslab):
            for b in range(SUBLANES):
                slab_ref[s, pl.ds(b, steps, stride=SUBLANES), :] = h_ref[b, :, s * LANES:(s + 1) * LANES]
        h_in = jnp.concatenate([slab_ref[s] for s in range(nslab)], axis=1)
    else:
        h_in = h_ref[...]
    rows = h_in.shape[0]
    hb = (_rms(h_in) * gain_ref[...]).astype(BF16)

    def conv(cur, c0):
        prev = halo_ref[:, c0:c0 + col_chunk]
        cw = convw_ref[:, c0:c0 + col_chunk]
        out = cur * cw[FFN_CONV - 1:FFN_CONV, :] + convb_ref[:, c0:c0 + col_chunk]
        for j in range(FFN_CONV - 1):
            back = (FFN_CONV - 1 - j) * SUBLANES
            shifted = jnp.concatenate([prev[halo - back:, :], cur[:rows - back, :]], axis=0)
            out = out + shifted * cw[j:j + 1, :]
        halo_ref[:, c0:c0 + col_chunk] = cur[rows - halo:, :]
        return out

    def up(j):
        g0 = j * col_chunk
        v0 = d_ff + j * col_chunk
        return _dot(hb, wup_ref[:, g0:g0 + col_chunk]), _dot(hb, wup_ref[:, v0:v0 + col_chunk])

    nchunk = d_ff // col_chunk
    acc = jnp.zeros(h_in.shape, F32)
    cur = up(0)
    for j in range(nchunk):
        nxt = up(j + 1) if j + 1 < nchunk else None
        g0 = j * col_chunk
        gate = conv(cur[0], g0)
        val = conv(cur[1], d_ff + g0)
        act = (_silu(gate) * val).astype(BF16)
        acc = acc + _dot(act, wdown_ref[g0:g0 + col_chunk, :])
        cur = nxt
    res = h_in + acc
    if not final:
        out_ref[...] = res
    else:
        fin = _rms(res) * fgain_ref[...]
        steps = rows // SUBLANES
        for s in range(fin.shape[1] // LANES):
            slab_ref[s] = fin[:, s * LANES:(s + 1) * LANES]
        for s in range(fin.shape[1] // LANES):
            for b in range(SUBLANES):
                out_ref[b, :, s * LANES:(s + 1) * LANES] = slab_ref[s, pl.ds(b, steps, stride=SUBLANES), :]


def _const_spec(shape):
    zeros = (0,) * len(shape)
    return pl.BlockSpec(shape, lambda *_: zeros)


def _params(sem, flags=None):
    return pltpu.CompilerParams(dimension_semantics=sem, vmem_limit_bytes=VMEM_LIMIT, flags=flags)


def _rope_tables(t):
    half = HEAD_DIM // 2
    inv_freq = ROPE_BASE ** (-jnp.arange(half, dtype=F32) / half)
    ang = jnp.arange(t, dtype=F32)[:, None] * inv_freq[None, :]
    cos = jnp.cos(ang)
    sin = jnp.sin(ang)
    return jnp.concatenate([cos, cos], axis=-1), jnp.concatenate([-sin, sin], axis=-1)


def _layer0_inproj(x, gain, w_in_all, conv_w, a_log, dt_bias, tile):
    b, t, d = x.shape
    width = HEADS * HEAD_DIM
    wsmall = jnp.zeros((d, LANES), F32).at[:, :2 * HEADS].set(w_in_all[0, :, 8 * width:]).astype(BF16)
    alog = jnp.zeros((1, LANES), F32).at[0, HEADS:2 * HEADS].set(a_log)
    dtb = jnp.zeros((1, LANES), F32).at[0, HEADS:2 * HEADS].set(dt_bias)
    cos, sin = _rope_tables(t)
    tiles_per_seq = t // tile
    ntiles = b * tiles_per_seq

    def in_tile(s):
        s = jnp.minimum(s, ntiles - 1)
        return s // tiles_per_seq, s % tiles_per_seq

    def out_tile(s):
        s = jnp.maximum(s - 1, 0)
        return s // tiles_per_seq, s % tiles_per_seq

    out_map = lambda s: (*out_tile(s), 0)
    pbuf = pltpu.VMEM((SUBLANES + tile, 8 * width + LANES), F32)
    return pl.pallas_call(
        functools.partial(_inproj_kernel, tiles_per_seq=tiles_per_seq),
        grid=(ntiles + 1,),
        in_specs=[
            pl.BlockSpec((1, tile, d), lambda s: (*in_tile(s), 0)),
            _const_spec((1, d)),
            pl.BlockSpec(memory_space=pl.ANY),
            _const_spec((d, LANES)),
            pl.BlockSpec((tile, HEAD_DIM), lambda s: (out_tile(s)[1], 0)),
            pl.BlockSpec((tile, HEAD_DIM), lambda s: (out_tile(s)[1], 0)),
            _const_spec((GDN_CONV, 3 * width)),
            _const_spec((1, LANES)),
            _const_spec((1, LANES)),
        ],
        out_specs=[
            pl.BlockSpec((1, tile, 4 * width), out_map),
            pl.BlockSpec((1, tile, 4 * width), out_map),
            pl.BlockSpec((1, tile, LANES), out_map),
        ],
        out_shape=[
            jax.ShapeDtypeStruct((b, t, 4 * width), BF16),
            jax.ShapeDtypeStruct((b, t, 4 * width), BF16),
            jax.ShapeDtypeStruct((b, t, LANES), F32),
        ],
        scratch_shapes=[pbuf, pbuf, pltpu.VMEM((tile, d), BF16),
                        pltpu.VMEM((d, 8 * width), BF16),
                        pltpu.VMEM((2, STAGE_ROWS, 8 * width), F32),
                        pltpu.SemaphoreType.DMA((2,))],
        compiler_params=_params(("arbitrary",)),
        name="l0_inproj",
    )(x, gain.reshape(1, d), w_in_all, wsmall, cos, sin, conv_w, alog, dtb)


def _layer0_mixer(ret, gdn, small, x, w_out_all, out_gain, tile):
    b, t, d = x.shape
    width = HEADS * HEAD_DIM
    return pl.pallas_call(
        _mixer_kernel,
        grid=(b, t // tile),
        in_specs=[
            pl.BlockSpec((1, tile, 4 * width), lambda i, j: (i, j, 0)),
            pl.BlockSpec((1, tile, 4 * width), lambda i, j: (i, j, 0)),
            pl.BlockSpec((1, tile, LANES), lambda i, j: (i, j, 0)),
            pl.BlockSpec((1, tile, d), lambda i, j: (i, j, 0)),
            pl.BlockSpec(memory_space=pl.ANY),
            _const_spec((1, HEAD_DIM)),
        ],
        out_specs=pl.BlockSpec((1, tile, d), lambda i, j: (i, j, 0)),
        out_shape=jax.ShapeDtypeStruct((b, t, d), F32),
        scratch_shapes=[
            pltpu.VMEM((HEADS, HEAD_DIM, HEAD_DIM), F32),
            pltpu.VMEM((HEADS, HEAD_DIM, HEAD_DIM), F32),
            pltpu.VMEM((tile, 2 * width), BF16),
            pltpu.VMEM((2 * width, d), BF16),
            pltpu.VMEM((2, STAGE_ROWS, d), F32),
            pltpu.SemaphoreType.DMA((2,)),
        ],
        compiler_params=_params(("arbitrary", "arbitrary")),
        name="l0_mixer",
    )(ret, gdn, small, x, w_out_all, out_gain.reshape(1, HEAD_DIM))


def _rglru(h_tm, gain, w_in_all, conv_w, conv_b, w_a_all, b_a, w_x_all, b_x, lam, w_out_all, steps):
    n, d = h_tm.shape
    d_rnn = lam.shape[0]
    rows = steps * SUBLANES
    nblock = d_rnn // LRU_BLOCK
    return pl.pallas_call(
        _rglru_kernel,
        grid=(n // rows,),
        in_specs=[
            pl.BlockSpec((rows, d), lambda i: (i, 0)),
            _const_spec((1, d)),
            pl.BlockSpec(memory_space=pl.ANY),
            _const_spec((LRU_CONV, d_rnn)),
            _const_spec((1, d_rnn)),
            pl.BlockSpec(memory_space=pl.ANY),
            pl.BlockSpec(memory_space=pl.ANY),
            _const_spec((1, d_rnn)),
            _const_spec((1, d_rnn)),
            _const_spec((1, d_rnn)),
            pl.BlockSpec(memory_space=pl.ANY),
        ],
        out_specs=pl.BlockSpec((rows, d), lambda i: (i, 0)),
        out_shape=jax.ShapeDtypeStruct((n, d), F32),
        scratch_shapes=[
            pltpu.VMEM(((LRU_CONV - 1) * SUBLANES, d_rnn), F32),
            pltpu.VMEM((SUBLANES, d_rnn), F32),
            pltpu.VMEM((rows, d_rnn), F32),
            pltpu.VMEM((rows, d_rnn), F32),
            pltpu.VMEM((rows, d_rnn), F32),
            pltpu.VMEM((d, 2 * d_rnn), BF16),
            pltpu.VMEM((d_rnn, d), BF16),
            pltpu.VMEM((nblock, LRU_BLOCK, 2 * LRU_BLOCK), BF16),
            pltpu.VMEM((2, STAGE_ROWS, 2 * d_rnn), F32),
            pltpu.SemaphoreType.DMA((2,)),
        ],
        compiler_params=_params(("arbitrary",)),
        name="l1_rglru",
    )(h_tm, gain.reshape(1, d), w_in_all, conv_w, conv_b.reshape(1, d_rnn), w_a_all, w_x_all,
      b_a.reshape(1, d_rnn), b_x.reshape(1, d_rnn), lam.reshape(1, d_rnn), w_out_all)


def _ffn(h, gain, w_up_all, conv_w, conv_b, w_down_all, final_gain, layer, steps, batch_major_in, final):
    d = h.shape[-1]
    n = h.size // d
    d_ff = w_down_all.shape[1]
    rows = steps * SUBLANES
    col_chunk = 2 * LANES
    kern = functools.partial(_ffn_kernel, layer=layer, d_ff=d_ff, col_chunk=col_chunk,
                             batch_major_in=batch_major_in, final=final)
    scratch = [pltpu.VMEM(((FFN_CONV - 1) * SUBLANES, 2 * d_ff), F32),
               pltpu.VMEM((d // LANES, rows, LANES), F32),
               pltpu.VMEM((d, 2 * d_ff), BF16),
               pltpu.VMEM((d_ff, d), BF16),
               pltpu.VMEM((2, STAGE_ROWS, 2 * d_ff), F32),
               pltpu.SemaphoreType.DMA((2,))]
    bm_spec = pl.BlockSpec((SUBLANES, steps, d), lambda i: (0, i, 0))
    tm_spec = pl.BlockSpec((rows, d), lambda i: (i, 0))
    if final:
        out_spec = bm_spec
        out_shape = jax.ShapeDtypeStruct((SUBLANES, n // SUBLANES, d), F32)
    else:
        out_spec = tm_spec
        out_shape = jax.ShapeDtypeStruct((n, d), F32)
    return pl.pallas_call(
        kern,
        grid=(n // rows,),
        in_specs=[
            bm_spec if batch_major_in else tm_spec,
            _const_spec((1, d)),
            pl.BlockSpec(memory_space=pl.ANY),
            _const_spec((FFN_CONV, 2 * d_ff)),
            _const_spec((1, 2 * d_ff)),
            pl.BlockSpec(memory_space=pl.ANY),
            _const_spec((1, d)),
        ],
        out_specs=out_spec,
        out_shape=out_shape,
        scratch_shapes=scratch,
        compiler_params=_params(("arbitrary",)),
        name="ffn_final" if final else "ffn",
    )(h, gain.reshape(1, d), w_up_all, conv_w, conv_b.reshape(1, 2 * d_ff),
      w_down_all, final_gain.reshape(1, d))


def kernel(x, norm_mix, norm_ffn, ret_gdn_w_in, gdn_conv_w, gdn_a_log, gdn_dt_bias, gdn_out_gain,
           ret_gdn_w_out, lru_w_in, lru_conv_w, lru_conv_b, lru_w_a, lru_b_a, lru_w_x, lru_b_x,
           lru_lambda, lru_w_out, ffn_w_up, ffn_conv_w, ffn_conv_b, ffn_w_down, norm_final):
    b, t, d = x.shape
    assert b == SUBLANES and norm_mix.shape[0] == 2
    tile_in = 512
    tile_mix = 512
    steps = 32
    steps_lru = 64

    ret, gdn, small = _layer0_inproj(x, norm_mix[0], ret_gdn_w_in, gdn_conv_w[0], gdn_a_log[0],
                                     gdn_dt_bias[0], tile_in)
    h = _layer0_mixer(ret, gdn, small, x, ret_gdn_w_out, gdn_out_gain[0], tile_mix)
    h = _ffn(h, norm_ffn[0], ffn_w_up, ffn_conv_w[0], ffn_conv_b[0], ffn_w_down, norm_final,
             0, steps, batch_major_in=True, final=False)
    h = _rglru(h, norm_mix[1], lru_w_in, lru_conv_w[0], lru_conv_b[0], lru_w_a, lru_b_a[0],
               lru_w_x, lru_b_x[0], lru_lambda[0], lru_w_out, steps_lru)
    return _ffn(h, norm_ffn[1], ffn_w_up, ffn_conv_w[1], ffn_conv_b[1], ffn_w_down, norm_final,
                1, steps, batch_major_in=False, final=True)
```

```python
import functools
import math

import numpy as np
import jax
import jax.numpy as jnp
from jax import lax
from jax.experimental import pallas as pl
from jax.experimental.pallas import tpu as pltpu

F32 = jnp.float32
BF16 = jnp.bfloat16

EPS = 1e-6
ROPE_BASE = 10000.0
HEADS = 4
HEAD_DIM = 128
CHUNK = 128
UP_AHEAD = 2
STAGE_ROWS = 128
ROW_CHUNK = 64
INV_BASE = 16
GDN_CONV = 4
LRU_CONV = 4
FFN_CONV = 3
LRU_C = 8.0
LRU_BLOCK = 128
SUBLANES = 8
LANES = 128
NEG_BIG = -1e30
VMEM_LIMIT = 56 * 1024 * 1024

RET_LOG_GAMMA = tuple(math.log1p(-2.0 ** (-5.0 - h)) for h in range(HEADS))


def _rms(x):
    return x * lax.rsqrt(jnp.mean(x * x, axis=-1, keepdims=True) + EPS)


def _sigmoid(x):
    return 1.0 / (1.0 + jnp.exp(-x))


def _silu(x):
    return x * _sigmoid(x)


def _softplus(x):
    return jnp.maximum(x, 0.0) + jnp.log1p(jnp.exp(-jnp.abs(x)))


def _gelu_tanh(x):
    c = 2.0 * math.sqrt(2.0 / math.pi)
    return x * _sigmoid(x * (c + (0.044715 * c) * (x * x)))


def _dot(a, b):
    return jnp.dot(a, b, preferred_element_type=F32)


def _dot_nt(a, b):
    return lax.dot_general(a, b, (((1,), (1,)), ((), ())), preferred_element_type=F32)


def _mm(a, b):
    return _dot(a.astype(BF16), b.astype(BF16))


def _split3(x):
    hi = x.astype(BF16)
    r1 = x - hi.astype(F32)
    mid = r1.astype(BF16)
    lo = (r1 - mid.astype(F32)).astype(BF16)
    return hi, mid, lo


def _stage_bf16(src, dst_ref, stage_ref, sem_ref, row_chunk):
    rows, cols = src.shape
    nchunk = rows // row_chunk

    def copy(i):
        return pltpu.make_async_copy(src.at[pl.ds(i * row_chunk, row_chunk), :],
                                     stage_ref.at[i % 2, pl.ds(0, row_chunk), pl.ds(0, cols)],
                                     sem_ref.at[i % 2])

    copy(0).start()
    for i in range(nchunk):
        if i + 1 < nchunk:
            copy(i + 1).start()
        copy(i).wait()
        dst_ref[pl.ds(i * row_chunk, row_chunk), :] = stage_ref[i % 2, 0:row_chunk, 0:cols].astype(BF16)


def _inproj_kernel(x_ref, gain_ref, win_hbm, wsmall_ref, cos_ref, sin_ref, convw_ref,
                   alog_ref, dtb_ref, ret_ref, gdn_ref, small_ref, pbuf0_ref, pbuf1_ref, hb_ref,
                   wmain_ref, stage_ref, sem_ref, *, tiles_per_seq):
    s = pl.program_id(0)
    tile = x_ref.shape[1]
    width = HEADS * HEAD_DIM
    nmain = 8 * width
    body = slice(SUBLANES, SUBLANES + tile)
    qkv = slice(4 * width, 7 * width)

    @pl.when(s == 0)
    def _():
        pbuf1_ref[...] = jnp.zeros(pbuf1_ref.shape, F32)
        _stage_bf16(win_hbm.at[0, :, pl.ds(0, 8 * HEADS * HEAD_DIM)], wmain_ref, stage_ref, sem_ref, STAGE_ROWS)

    def step(pa_ref, pb_ref):
        for r in range(tile // ROW_CHUNK):
            rows = slice(r * ROW_CHUNK, (r + 1) * ROW_CHUNK)
            hb_ref[rows, :] = (_rms(x_ref[0, rows, :]) * gain_ref[...]).astype(BF16)
        for c0, c1 in ((0, 2 * width), (2 * width, 4 * width), (4 * width, 7 * width), (7 * width, nmain)):
            pa_ref[body, c0:c1] = _dot(hb_ref[...], wmain_ref[:, c0:c1])
        pa_ref[body, nmain:nmain + LANES] = _dot(hb_ref[...], wsmall_ref[...])

        first_of_seq = lax.rem(jnp.maximum(s - 1, 0), tiles_per_seq) == 0
        pb_ref[0:SUBLANES, qkv] = jnp.where(first_of_seq, 0.0, pb_ref[0:SUBLANES, qkv])
        for r in range(tile // ROW_CHUNK):
            out_rows = slice(r * ROW_CHUNK, (r + 1) * ROW_CHUNK)
            rows = slice(SUBLANES + r * ROW_CHUNK, SUBLANES + (r + 1) * ROW_CHUNK)
            cos = cos_ref[out_rows, :]
            sin = sin_ref[out_rows, :]
            for part in range(2):
                for h in range(HEADS):
                    c0 = part * width + h * HEAD_DIM
                    ph = pb_ref[rows, c0:c0 + HEAD_DIM]
                    rot = ph * cos + pltpu.roll(ph, HEAD_DIM // 2, 1) * sin
                    if part == 1:
                        rot = rot * (HEAD_DIM ** -0.5)
                    ret_ref[0, out_rows, c0:c0 + HEAD_DIM] = rot.astype(BF16)
            for c0 in range(2 * width, 4 * width, LANES):
                ret_ref[0, out_rows, c0:c0 + LANES] = pb_ref[rows, c0:c0 + LANES].astype(BF16)
            for c0 in range(0, width, LANES):
                gdn_ref[0, out_rows, 3 * width + c0:3 * width + c0 + LANES] = (
                    pb_ref[rows, 7 * width + c0:7 * width + c0 + LANES].astype(BF16))
            for part in range(3):
                for h in range(HEADS):
                    c0 = part * width + h * HEAD_DIM
                    src_cols = slice(4 * width + c0, 4 * width + c0 + HEAD_DIM)
                    cw = convw_ref[:, c0:c0 + HEAD_DIM]
                    acc = pb_ref[rows, src_cols] * cw[GDN_CONV - 1:GDN_CONV, :]
                    for j in range(GDN_CONV - 1):
                        back = GDN_CONV - 1 - j
                        shifted = pb_ref[rows.start - back:rows.stop - back, src_cols]
                        acc = acc + shifted * cw[j:j + 1, :]
                    act = _silu(acc)
                    if part < 2:
                        act = act * lax.rsqrt(jnp.sum(act * act, axis=-1, keepdims=True) + EPS)
                    if part == 0:
                        act = act * (HEAD_DIM ** -0.5)
                    gdn_ref[0, out_rows, c0:c0 + HEAD_DIM] = act.astype(BF16)
            sm = pb_ref[rows, nmain:nmain + LANES]
            col = lax.broadcasted_iota(jnp.int32, sm.shape, 1)
            beta = _sigmoid(sm)
            g = -jnp.exp(alog_ref[...]) * _softplus(sm + dtb_ref[...])
            small_ref[0, out_rows, :] = jnp.where(col < HEADS, beta, g)
        pa_ref[0:SUBLANES, qkv] = pb_ref[tile:tile + SUBLANES, qkv]

    parity = lax.rem(s, 2)

    @pl.when(parity == 0)
    def _():
        step(pbuf0_ref, pbuf1_ref)

    @pl.when(parity == 1)
    def _():
        step(pbuf1_ref, pbuf0_ref)


def _mixer_kernel(ret_ref, gdn_ref, small_ref, x_ref, wout_hbm, ogain_ref, out_ref,
                  sret_ref, sgdn_ref, ybuf_ref, wout_ref, stage_ref, sem_ref):
    ti = pl.program_id(1)

    @pl.when((pl.program_id(0) == 0) & (ti == 0))
    def _():
        _stage_bf16(wout_hbm.at[0], wout_ref, stage_ref, sem_ref, STAGE_ROWS)

    tile = x_ref.shape[1]
    width = HEADS * HEAD_DIM
    c = CHUNK

    @pl.when(ti == 0)
    def _():
        sret_ref[...] = jnp.zeros(sret_ref.shape, F32)
        sgdn_ref[...] = jnp.zeros(sgdn_ref.shape, F32)

    row = lax.broadcasted_iota(jnp.int32, (c, c), 0)
    col = lax.broadcasted_iota(jnp.int32, (c, c), 1)
    tril = row >= col
    strict = row > col
    rowf = row.astype(F32)
    relf = (row - col).astype(F32)
    eye = jnp.where(row == col, 1.0, 0.0).astype(F32)
    tri_b = jnp.where(tril, 1.0, 0.0).astype(BF16)
    ogain = ogain_ref[...]
    nblk = tile // c
    probs = [(blk, h) for blk in range(nblk) for h in range(HEADS)]

    def cols(ref, part, blk, h):
        c0 = part * width + h * HEAD_DIM
        return ref[0, blk * c:(blk + 1) * c, c0:c0 + HEAD_DIM]


    sms, gcums, gcum_ts = [], [], []
    for blk in range(nblk):
        sm = small_ref[0, blk * c:(blk + 1) * c, :]
        parts = _dot(tri_b, jnp.concatenate(_split3(sm), axis=1))
        gcum = parts[:, :LANES] + parts[:, LANES:2 * LANES] + parts[:, 2 * LANES:]
        sms.append(sm)
        gcums.append(gcum)
        gcum_ts.append(gcum.T)
    ret_qk = {}
    gdn_kq = {}
    for p in probs:
        blk, h = p
        ret_qk[p] = _dot_nt(cols(ret_ref, 0, blk, h), cols(ret_ref, 1, blk, h))
        k = cols(gdn_ref, 1, blk, h)
        gdn_kq[p] = _dot_nt(jnp.concatenate([k, cols(gdn_ref, 0, blk, h)], axis=0), k)

    ret_iv = {}
    for p in probs:
        blk, h = p
        lg = RET_LOG_GAMMA[h]
        dmask = jnp.where(tril, jnp.exp(lg * jnp.where(tril, relf, 0.0)), 0.0)
        scores = (ret_qk[p] * dmask).astype(BF16)
        k_tail_t = (cols(ret_ref, 1, blk, h).astype(F32) * jnp.exp(lg * (c - 1.0 - rowf))).T.astype(BF16)
        ret_iv[p] = _dot(jnp.concatenate([scores, k_tail_t], axis=0), cols(ret_ref, 2, blk, h))

    lmats, attns, rhs, kts, egcs, glasts = {}, {}, {}, {}, {}, {}
    for p in probs:
        blk, h = p
        gcum, sm = gcums[blk], sms[blk]
        gc_col = jnp.broadcast_to(gcum[:, HEADS + h:HEADS + h + 1], (c, c))
        gc_row = jnp.broadcast_to(gcum_ts[blk][HEADS + h:HEADS + h + 1, :], (c, c))
        g_last = jnp.broadcast_to(gcum[c - 1:c, HEADS + h:HEADS + h + 1], (c, c))
        beta_col = jnp.broadcast_to(sm[:, h:h + 1], (c, c))
        decay = jnp.exp(jnp.where(tril, gc_col - gc_row, NEG_BIG))
        kf = cols(gdn_ref, 1, blk, h).astype(F32)
        egc = jnp.exp(gc_col)
        lmats[p] = jnp.where(strict, gdn_kq[p][:c] * beta_col * decay, 0.0)
        attns[p] = jnp.where(tril, gdn_kq[p][c:] * decay, 0.0).astype(BF16)
        rhs[p] = jnp.concatenate([(cols(gdn_ref, 2, blk, h).astype(F32) * beta_col).astype(BF16),
                                  (kf * (beta_col * egc)).astype(BF16)], axis=1)
        kts[p] = (kf * jnp.exp(g_last - gc_col)).T.astype(BF16)
        egcs[p] = egc
        glasts[p] = g_last

    def same_block(size):
        return (row // size) == (col // size)

    diag = same_block(INV_BASE)
    tinv, power = {}, {}
    for p in probs:
        ld = jnp.where(diag, lmats[p], 0.0)
        tinv[p] = eye - ld
        power[p] = _mm(ld, ld)
    factors = int(math.log2(INV_BASE)) - 1
    for f in range(factors):
        last = f == factors - 1
        for p in probs:
            pw = power[p].astype(BF16)
            if last:
                tinv[p] = tinv[p] + _dot(tinv[p].astype(BF16), pw)
            else:
                both = _dot(jnp.concatenate([tinv[p].astype(BF16), pw], axis=0), pw)
                tinv[p] = tinv[p] + both[:c]
                power[p] = both[c:]
    size = INV_BASE
    while size < c:
        off = same_block(2 * size) & jnp.logical_not(same_block(size))
        ct = {p: _dot(jnp.where(off, lmats[p], 0.0).astype(BF16), tinv[p].astype(BF16)) for p in probs}
        for p in probs:
            tinv[p] = tinv[p] - _dot(tinv[p].astype(BF16), ct[p].astype(BF16))
        size *= 2

    ret_inter = {}
    for h in range(HEADS):
        lg = RET_LOG_GAMMA[h]
        state = sret_ref[h]
        for blk in range(nblk):
            ret_inter[(blk, h)] = _dot(cols(ret_ref, 0, blk, h), state.astype(BF16))
            state = state * math.exp(lg * c) + ret_iv[(blk, h)][c:]
        sret_ref[h] = state

    uw = {p: _dot(tinv[p].astype(BF16), rhs[p]).astype(BF16) for p in probs}
    apn = {}
    for p in probs:
        wu = jnp.concatenate([uw[p][:, HEAD_DIM:], uw[p][:, :HEAD_DIM]], axis=1)
        apn[p] = _dot(jnp.concatenate([attns[p], kts[p]], axis=0), wu)

    for p in probs:
        blk, h = p
        lg = RET_LOG_GAMMA[h]
        o = ret_iv[p][:c] + ret_inter[p] * jnp.exp(lg * (rowf + 1.0))
        y = _rms(o) * _silu(cols(ret_ref, 3, blk, h).astype(F32))
        ybuf_ref[blk * c:(blk + 1) * c, h * HEAD_DIM:(h + 1) * HEAD_DIM] = y.astype(BF16)

    states = [sgdn_ref[h] for h in range(HEADS)]
    for blk in range(nblk):
        for h in range(HEADS):
            p = (blk, h)
            q_eff = (cols(gdn_ref, 0, blk, h).astype(F32) * egcs[p] - apn[p][:c, :HEAD_DIM]).astype(BF16)
            pq = jnp.concatenate([apn[p][c:, :HEAD_DIM].astype(BF16), q_eff], axis=0)
            ps = _dot(pq, states[h].astype(BF16))
            o = ps[c:] + apn[p][:c, HEAD_DIM:]
            states[h] = jnp.exp(glasts[p]) * states[h] - ps[:c] + apn[p][c:, HEAD_DIM:]
            y = _rms(o) * ogain * _silu(cols(gdn_ref, 3, blk, h).astype(F32))
            ybuf_ref[blk * c:(blk + 1) * c, width + h * HEAD_DIM:width + (h + 1) * HEAD_DIM] = y.astype(BF16)
    for h in range(HEADS):
        sgdn_ref[h] = states[h]

    out_ref[0] = x_ref[0] + _dot(ybuf_ref[...], wout_ref[...])


def _rglru_kernel(h_ref, gain_ref, win_hbm, convw_ref, convb_ref, wa_hbm, wx_hbm, ba_ref, bx_ref,
                  lam_ref, wout_hbm, out_ref, state_ref, a_ref, u_ref, hs_ref, xr_ref, hb_ref,
                  hin0_ref, hin1_ref, gate0_ref, gate1_ref, xp0_ref, xp1_ref,
                  win_ref, wout_ref, wax_ref, stage_ref, sem_ref):
    s = pl.program_id(0)
    rows = h_ref.shape[0]
    d_rnn = lam_ref.shape[1]
    steps = rows // SUBLANES
    hist = (LRU_CONV - 1) * SUBLANES
    nblock = d_rnn // LRU_BLOCK

    @pl.when(s == 0)
    def _():
        state_ref[...] = jnp.zeros(state_ref.shape, F32)
        hin1_ref[...] = jnp.zeros(hin1_ref.shape, F32)
        gate1_ref[...] = jnp.zeros(gate1_ref.shape, F32)
        xp1_ref[...] = jnp.zeros(xp1_ref.shape, F32)
        _stage_bf16(win_hbm.at[0], win_ref, stage_ref, sem_ref, STAGE_ROWS)
        _stage_bf16(wout_hbm.at[0], wout_ref, stage_ref, sem_ref, STAGE_ROWS)
        for n in range(nblock):
            _stage_bf16(wa_hbm.at[0, n], wax_ref.at[n, :, pl.ds(0, LRU_BLOCK)], stage_ref, sem_ref, LRU_BLOCK)
            _stage_bf16(wx_hbm.at[0, n], wax_ref.at[n, :, pl.ds(LRU_BLOCK, LRU_BLOCK)], stage_ref, sem_ref, LRU_BLOCK)

    def step(hin_a, gate_a, xp_a, hin_b, gate_b, xp_b):
        hin_a[...] = h_ref[...]
        for r in range(rows // ROW_CHUNK):
            rs = slice(r * ROW_CHUNK, (r + 1) * ROW_CHUNK)
            hb_ref[rs, :] = (_rms(h_ref[rs, :]) * gain_ref[...]).astype(BF16)

        ris = []
        slab = 2 * d_rnn // nblock
        for n in range(nblock):
            c0 = n * LRU_BLOCK
            cw = convw_ref[:, c0:c0 + LRU_BLOCK]
            for r in range(rows // ROW_CHUNK):
                lo = hist + r * ROW_CHUNK
                xr = xp_b[lo:lo + ROW_CHUNK, c0:c0 + LRU_BLOCK] * cw[LRU_CONV - 1:LRU_CONV, :] + convb_ref[:, c0:c0 + LRU_BLOCK]
                for j in range(LRU_CONV - 1):
                    back = (LRU_CONV - 1 - j) * SUBLANES
                    xr = xr + xp_b[lo - back:lo - back + ROW_CHUNK, c0:c0 + LRU_BLOCK] * cw[j:j + 1, :]
                xr_ref[r * ROW_CHUNK:(r + 1) * ROW_CHUNK, c0:c0 + LRU_BLOCK] = xr
            ris.append(_dot(xr_ref[:, c0:c0 + LRU_BLOCK].astype(BF16), wax_ref[n]))
            p0 = n * slab
            proj = _dot(hb_ref[...], win_ref[:, p0:p0 + slab])
            if p0 < d_rnn:
                gate_a[:, p0:p0 + slab] = proj
            else:
                xp_a[hist:hist + rows, p0 - d_rnn:p0 - d_rnn + slab] = proj
        xp_a[0:hist, :] = xp_b[rows:rows + hist, :]

        sp = _softplus(-lam_ref[...]) * (-LRU_C)
        for n in range(nblock):
            c0 = n * LRU_BLOCK
            ri = ris[n]
            r = _sigmoid(ri[:, :LRU_BLOCK] + ba_ref[:, c0:c0 + LRU_BLOCK])
            i = _sigmoid(ri[:, LRU_BLOCK:] + bx_ref[:, c0:c0 + LRU_BLOCK])
            a = jnp.exp(r * sp[:, c0:c0 + LRU_BLOCK])
            a_ref[:, c0:c0 + LRU_BLOCK] = a
            u_ref[:, c0:c0 + LRU_BLOCK] = jnp.sqrt(1.0 - a * a) * (i * xr_ref[:, c0:c0 + LRU_BLOCK])

        state = jnp.where(s <= 1, 0.0, state_ref[...])
        for t in range(steps):
            sl = slice(t * SUBLANES, (t + 1) * SUBLANES)
            state = a_ref[sl, :] * state + u_ref[sl, :]
            hs_ref[sl, :] = state
        state_ref[...] = state

        y = (_gelu_tanh(gate_b[...]) * hs_ref[...]).astype(BF16)
        out_ref[...] = hin_b[...] + _dot(y, wout_ref[...])

    parity = lax.rem(s, 2)

    @pl.when(parity == 0)
    def _():
        step(hin0_ref, gate0_ref, xp0_ref, hin1_ref, gate1_ref, xp1_ref)

    @pl.when(parity == 1)
    def _():
        step(hin1_ref, gate1_ref, xp1_ref, hin0_ref, gate0_ref, xp0_ref)


def _ffn_kernel(h_ref, gain_ref, wup_hbm, convw_ref, convb_ref, wdown_hbm, fgain_ref, out_ref,
                halo_ref, slab_ref, hin0_ref, hin1_ref, hb0_ref, hb1_ref, wup_ref, wdown_ref, stage_ref, sem_ref,
                *, layer, d_ff, col_chunk, batch_major_in, final):
    s = pl.program_id(0)
    halo = (FFN_CONV - 1) * SUBLANES
    rows = hin0_ref.shape[0]

    @pl.when(s == 0)
    def _():
        halo_ref[...] = jnp.zeros(halo_ref.shape, F32)
        hin1_ref[...] = jnp.zeros(hin1_ref.shape, F32)
        hb1_ref[...] = jnp.zeros(hb1_ref.shape, BF16)
        _stage_bf16(wup_hbm.at[layer], wup_ref, stage_ref, sem_ref, STAGE_ROWS)
        _stage_bf16(wdown_hbm.at[layer], wdown_ref, stage_ref, sem_ref, STAGE_ROWS)

    def step(hin_a, hb_a, hin_b, hb_b):
        if batch_major_in:
            steps = h_ref.shape[1]
            nslab = h_ref.shape[2] // LANES
            for i in range(nslab):
                for b in range(SUBLANES):
                    slab_ref[i, pl.ds(b, steps, stride=SUBLANES), :] = h_ref[b, :, i * LANES:(i + 1) * LANES]
            for i in range(nslab):
                hin_a[:, i * LANES:(i + 1) * LANES] = slab_ref[i]
        else:
            hin_a[...] = h_ref[...]
        for r in range(rows // ROW_CHUNK):
            rs = slice(r * ROW_CHUNK, (r + 1) * ROW_CHUNK)
            hb_a[rs, :] = (_rms(hin_a[rs, :]) * gain_ref[...]).astype(BF16)

        def conv(cur, c0):
            prev = halo_ref[:, c0:c0 + col_chunk]
            cw = convw_ref[:, c0:c0 + col_chunk]
            out = cur * cw[FFN_CONV - 1:FFN_CONV, :] + convb_ref[:, c0:c0 + col_chunk]
            for j in range(FFN_CONV - 1):
                back = (FFN_CONV - 1 - j) * SUBLANES
                shifted = jnp.concatenate([prev[halo - back:, :], cur[:rows - back, :]], axis=0)
                out = out + shifted * cw[j:j + 1, :]
            halo_ref[:, c0:c0 + col_chunk] = cur[rows - halo:, :]
            return out

        def up(j):
            g0 = j * col_chunk
            v0 = d_ff + j * col_chunk
            return _dot(hb_b[...], wup_ref[:, g0:g0 + col_chunk]), _dot(hb_b[...], wup_ref[:, v0:v0 + col_chunk])

        nchunk = d_ff // col_chunk
        acc = jnp.zeros((rows, hin_b.shape[1]), F32)
        queue = [up(j) for j in range(min(UP_AHEAD, nchunk))]
        for j in range(nchunk):
            if j + UP_AHEAD < nchunk:
                queue.append(up(j + UP_AHEAD))
            cur = queue.pop(0)
            g0 = j * col_chunk
            gate = conv(cur[0], g0)
            val = conv(cur[1], d_ff + g0)
            act = (_silu(gate) * val).astype(BF16)
            acc = acc + _dot(act, wdown_ref[g0:g0 + col_chunk, :])
        res = hin_b[...] + acc
        if not final:
            out_ref[...] = res
        else:
            fin = _rms(res) * fgain_ref[...]
            steps = rows // SUBLANES
            for i in range(fin.shape[1] // LANES):
                slab_ref[i] = fin[:, i * LANES:(i + 1) * LANES]
            for i in range(fin.shape[1] // LANES):
                for b in range(SUBLANES):
                    out_ref[b, :, i * LANES:(i + 1) * LANES] = slab_ref[i, pl.ds(b, steps, stride=SUBLANES), :]

    parity = lax.rem(s, 2)

    @pl.when(parity == 0)
    def _():
        step(hin0_ref, hb0_ref, hin1_ref, hb1_ref)

    @pl.when(parity == 1)
    def _():
        step(hin1_ref, hb1_ref, hin0_ref, hb0_ref)


def _const_spec(shape):
    zeros = (0,) * len(shape)
    return pl.BlockSpec(shape, lambda *_: zeros)


def _params(sem, flags=None):
    return pltpu.CompilerParams(dimension_semantics=sem, vmem_limit_bytes=VMEM_LIMIT, flags=flags)


def _rope_tables(t):
    half = HEAD_DIM // 2
    inv_freq = ROPE_BASE ** (-jnp.arange(half, dtype=F32) / half)
    ang = jnp.arange(t, dtype=F32)[:, None] * inv_freq[None, :]
    cos = jnp.cos(ang)
    sin = jnp.sin(ang)
    return jnp.concatenate([cos, cos], axis=-1), jnp.concatenate([-sin, sin], axis=-1)


def _layer0_inproj(x, gain, w_in_all, conv_w, a_log, dt_bias, tile):
    b, t, d = x.shape
    width = HEADS * HEAD_DIM
    wsmall = jnp.zeros((d, LANES), F32).at[:, :2 * HEADS].set(w_in_all[0, :, 8 * width:]).astype(BF16)
    alog = jnp.zeros((1, LANES), F32).at[0, HEADS:2 * HEADS].set(a_log)
    dtb = jnp.zeros((1, LANES), F32).at[0, HEADS:2 * HEADS].set(dt_bias)
    cos, sin = _rope_tables(t)
    tiles_per_seq = t // tile
    ntiles = b * tiles_per_seq

    def in_tile(s):
        s = jnp.minimum(s, ntiles - 1)
        return s // tiles_per_seq, s % tiles_per_seq

    def out_tile(s):
        s = jnp.maximum(s - 1, 0)
        return s // tiles_per_seq, s % tiles_per_seq

    out_map = lambda s: (*out_tile(s), 0)
    pbuf = pltpu.VMEM((SUBLANES + tile, 8 * width + LANES), F32)
    return pl.pallas_call(
        functools.partial(_inproj_kernel, tiles_per_seq=tiles_per_seq),
        grid=(ntiles + 1,),
        in_specs=[
            pl.BlockSpec((1, tile, d), lambda s: (*in_tile(s), 0)),
            _const_spec((1, d)),
            pl.BlockSpec(memory_space=pl.ANY),
            _const_spec((d, LANES)),
            pl.BlockSpec((tile, HEAD_DIM), lambda s: (out_tile(s)[1], 0)),
            pl.BlockSpec((tile, HEAD_DIM), lambda s: (out_tile(s)[1], 0)),
            _const_spec((GDN_CONV, 3 * width)),
            _const_spec((1, LANES)),
            _const_spec((1, LANES)),
        ],
        out_specs=[
            pl.BlockSpec((1, tile, 4 * width), out_map),
            pl.BlockSpec((1, tile, 4 * width), out_map),
            pl.BlockSpec((1, tile, LANES), out_map),
        ],
        out_shape=[
            jax.ShapeDtypeStruct((b, t, 4 * width), BF16),
            jax.ShapeDtypeStruct((b, t, 4 * width), BF16),
            jax.ShapeDtypeStruct((b, t, LANES), F32),
        ],
        scratch_shapes=[pbuf, pbuf, pltpu.VMEM((tile, d), BF16),
                        pltpu.VMEM((d, 8 * width), BF16),
                        pltpu.VMEM((2, STAGE_ROWS, 8 * width), F32),
                        pltpu.SemaphoreType.DMA((2,))],
        compiler_params=_params(("arbitrary",)),
        name="l0_inproj",
    )(x, gain.reshape(1, d), w_in_all, wsmall, cos, sin, conv_w, alog, dtb)


def _layer0_mixer(ret, gdn, small, x, w_out_all, out_gain, tile):
    b, t, d = x.shape
    width = HEADS * HEAD_DIM
    return pl.pallas_call(
        _mixer_kernel,
        grid=(b, t // tile),
        in_specs=[
            pl.BlockSpec((1, tile, 4 * width), lambda i, j: (i, j, 0)),
            pl.BlockSpec((1, tile, 4 * width), lambda i, j: (i, j, 0)),
            pl.BlockSpec((1, tile, LANES), lambda i, j: (i, j, 0)),
            pl.BlockSpec((1, tile, d), lambda i, j: (i, j, 0)),
            pl.BlockSpec(memory_space=pl.ANY),
            _const_spec((1, HEAD_DIM)),
        ],
        out_specs=pl.BlockSpec((1, tile, d), lambda i, j: (i, j, 0)),
        out_shape=jax.ShapeDtypeStruct((b, t, d), F32),
        scratch_shapes=[
            pltpu.VMEM((HEADS, HEAD_DIM, HEAD_DIM), F32),
            pltpu.VMEM((HEADS, HEAD_DIM, HEAD_DIM), F32),
            pltpu.VMEM((tile, 2 * width), BF16),
            pltpu.VMEM((2 * width, d), BF16),
            pltpu.VMEM((2, STAGE_ROWS, d), F32),
            pltpu.SemaphoreType.DMA((2,)),
        ],
        compiler_params=_params(("arbitrary", "arbitrary")),
        name="l0_mixer",
    )(ret, gdn, small, x, w_out_all, out_gain.reshape(1, HEAD_DIM))


def _rglru(h_tm, gain, w_in_all, conv_w, conv_b, w_a_all, b_a, w_x_all, b_x, lam, w_out_all, steps):
    n, d = h_tm.shape
    d_rnn = lam.shape[0]
    rows = steps * SUBLANES
    ntiles = n // rows
    nblock = d_rnn // LRU_BLOCK
    hist = (LRU_CONV - 1) * SUBLANES
    any_spec = pl.BlockSpec(memory_space=pl.ANY)
    return pl.pallas_call(
        _rglru_kernel,
        grid=(ntiles + 1,),
        in_specs=[
            pl.BlockSpec((rows, d), lambda s: (jnp.minimum(s, ntiles - 1), 0)),
            _const_spec((1, d)),
            any_spec,
            _const_spec((LRU_CONV, d_rnn)),
            _const_spec((1, d_rnn)),
            any_spec,
            any_spec,
            _const_spec((1, d_rnn)),
            _const_spec((1, d_rnn)),
            _const_spec((1, d_rnn)),
            any_spec,
        ],
        out_specs=pl.BlockSpec((rows, d), lambda s: (jnp.maximum(s - 1, 0), 0)),
        out_shape=jax.ShapeDtypeStruct((n, d), F32),
        scratch_shapes=[
            pltpu.VMEM((SUBLANES, d_rnn), F32),
            pltpu.VMEM((rows, d_rnn), F32),
            pltpu.VMEM((rows, d_rnn), F32),
            pltpu.VMEM((rows, d_rnn), F32),
            pltpu.VMEM((rows, d_rnn), F32),
            pltpu.VMEM((rows, d), BF16),
            pltpu.VMEM((rows, d), F32), pltpu.VMEM((rows, d), F32),
            pltpu.VMEM((rows, d_rnn), F32), pltpu.VMEM((rows, d_rnn), F32),
            pltpu.VMEM((hist + rows, d_rnn), F32), pltpu.VMEM((hist + rows, d_rnn), F32),
            pltpu.VMEM((d, 2 * d_rnn), BF16),
            pltpu.VMEM((d_rnn, d), BF16),
            pltpu.VMEM((nblock, LRU_BLOCK, 2 * LRU_BLOCK), BF16),
            pltpu.VMEM((2, STAGE_ROWS, 2 * d_rnn), F32),
            pltpu.SemaphoreType.DMA((2,)),
        ],
        compiler_params=_params(("arbitrary",)),
        name="l1_rglru",
    )(h_tm, gain.reshape(1, d), w_in_all, conv_w, conv_b.reshape(1, d_rnn), w_a_all, w_x_all,
      b_a.reshape(1, d_rnn), b_x.reshape(1, d_rnn), lam.reshape(1, d_rnn), w_out_all)


def _ffn(h, gain, w_up_all, conv_w, conv_b, w_down_all, final_gain, layer, steps, batch_major_in, final):
    d = h.shape[-1]
    n = h.size // d
    d_ff = w_down_all.shape[1]
    rows = steps * SUBLANES
    col_chunk = 2 * LANES
    kern = functools.partial(_ffn_kernel, layer=layer, d_ff=d_ff, col_chunk=col_chunk,
                             batch_major_in=batch_major_in, final=final)
    ntiles = n // rows
    scratch = [pltpu.VMEM(((FFN_CONV - 1) * SUBLANES, 2 * d_ff), F32),
               pltpu.VMEM((d // LANES, rows, LANES), F32),
               pltpu.VMEM((rows, d), F32), pltpu.VMEM((rows, d), F32),
               pltpu.VMEM((rows, d), BF16), pltpu.VMEM((rows, d), BF16),
               pltpu.VMEM((d, 2 * d_ff), BF16),
               pltpu.VMEM((d_ff, d), BF16),
               pltpu.VMEM((2, STAGE_ROWS, 2 * d_ff), F32),
               pltpu.SemaphoreType.DMA((2,))]
    in_tile = lambda s: jnp.minimum(s, ntiles - 1)
    out_tile = lambda s: jnp.maximum(s - 1, 0)
    bm_spec = lambda tile_of: pl.BlockSpec((SUBLANES, steps, d), lambda s: (0, tile_of(s), 0))
    tm_spec = lambda tile_of: pl.BlockSpec((rows, d), lambda s: (tile_of(s), 0))
    if final:
        out_spec = bm_spec(out_tile)
        out_shape = jax.ShapeDtypeStruct((SUBLANES, n // SUBLANES, d), F32)
    else:
        out_spec = tm_spec(out_tile)
        out_shape = jax.ShapeDtypeStruct((n, d), F32)
    return pl.pallas_call(
        kern,
        grid=(ntiles + 1,),
        in_specs=[
            bm_spec(in_tile) if batch_major_in else tm_spec(in_tile),
            _const_spec((1, d)),
            pl.BlockSpec(memory_space=pl.ANY),
            _const_spec((FFN_CONV, 2 * d_ff)),
            _const_spec((1, 2 * d_ff)),
            pl.BlockSpec(memory_space=pl.ANY),
            _const_spec((1, d)),
        ],
        out_specs=out_spec,
        out_shape=out_shape,
        scratch_shapes=scratch,
        compiler_params=_params(("arbitrary",)),
        name="ffn_final" if final else "ffn",
    )(h, gain.reshape(1, d), w_up_all, conv_w, conv_b.reshape(1, 2 * d_ff),
      w_down_all, final_gain.reshape(1, d))


def kernel(x, norm_mix, norm_ffn, ret_gdn_w_in, gdn_conv_w, gdn_a_log, gdn_dt_bias, gdn_out_gain,
           ret_gdn_w_out, lru_w_in, lru_conv_w, lru_conv_b, lru_w_a, lru_b_a, lru_w_x, lru_b_x,
           lru_lambda, lru_w_out, ffn_w_up, ffn_conv_w, ffn_conv_b, ffn_w_down, norm_final):
    b, t, d = x.shape
    assert b == SUBLANES and norm_mix.shape[0] == 2
    tile_in = 512
    tile_mix = 512
    steps = 32
    steps_lru = 64

    ret, gdn, small = _layer0_inproj(x, norm_mix[0], ret_gdn_w_in, gdn_conv_w[0], gdn_a_log[0],
                                     gdn_dt_bias[0], tile_in)
    h = _layer0_mixer(ret, gdn, small, x, ret_gdn_w_out, gdn_out_gain[0], tile_mix)
    h = _ffn(h, norm_ffn[0], ffn_w_up, ffn_conv_w[0], ffn_conv_b[0], ffn_w_down, norm_final,
             0, steps, batch_major_in=True, final=False)
    h = _rglru(h, norm_mix[1], lru_w_in, lru_conv_w[0], lru_conv_b[0], lru_w_a, lru_b_a[0],
               lru_w_x, lru_b_x[0], lru_lambda[0], lru_w_out, steps_lru)
    return _ffn(h, norm_ffn[1], ffn_w_up, ffn_conv_w[1], ffn_conv_b[1], ffn_w_down, norm_final,
                1, steps, batch_major_in=False, final=True)
```

```python
import functools
import math

import numpy as np
import jax
import jax.numpy as jnp
from jax import lax
from jax.experimental import pallas as pl
from jax.experimental.pallas import tpu as pltpu

F32 = jnp.float32
BF16 = jnp.bfloat16

EPS = 1e-6
ROPE_BASE = 10000.0
HEADS = 4
HEAD_DIM = 128
CHUNK = 128
UP_AHEAD = 2
STAGE_ROWS = 128
ROW_CHUNK = 64
INV_BASE = 16
GDN_CONV = 4
LRU_CONV = 4
FFN_CONV = 3
LRU_C = 8.0
LRU_BLOCK = 128
SUBLANES = 8
LANES = 128
NEG_BIG = -1e30
VMEM_LIMIT = 56 * 1024 * 1024

RET_LOG_GAMMA = tuple(math.log1p(-2.0 ** (-5.0 - h)) for h in range(HEADS))


def _rms(x):
    return x * lax.rsqrt(jnp.mean(x * x, axis=-1, keepdims=True) + EPS)


def _sigmoid(x):
    return 1.0 / (1.0 + jnp.exp(-x))


def _silu(x):
    return x * _sigmoid(x)


def _softplus(x):
    return jnp.maximum(x, 0.0) + jnp.log1p(jnp.exp(-jnp.abs(x)))


def _gelu_tanh(x):
    c = 2.0 * math.sqrt(2.0 / math.pi)
    return x * _sigmoid(x * (c + (0.044715 * c) * (x * x)))


def _dot(a, b):
    return jnp.dot(a, b, preferred_element_type=F32)


def _dot_nt(a, b):
    return lax.dot_general(a, b, (((1,), (1,)), ((), ())), preferred_element_type=F32)


def _mm(a, b):
    return _dot(a.astype(BF16), b.astype(BF16))


def _split3(x):
    hi = x.astype(BF16)
    r1 = x - hi.astype(F32)
    mid = r1.astype(BF16)
    lo = (r1 - mid.astype(F32)).astype(BF16)
    return hi, mid, lo


def _run_pipeline(s, nsteps, set0, set1, step):
    last = nsteps - 1
    parity = lax.rem(s, 2)

    @pl.when(s == 0)
    def _():
        step(set0, set1, True, False)

    @pl.when((s > 0) & (s < last) & (parity == 0))
    def _():
        step(set0, set1, True, True)

    @pl.when((s < last) & (parity == 1))
    def _():
        step(set1, set0, True, True)

    @pl.when(s == last)
    def _():
        if last % 2 == 0:
            step(set0, set1, False, True)
        else:
            step(set1, set0, False, True)


def _stage_bf16(src, dst_ref, stage_ref, sem_ref, row_chunk):
    rows, cols = src.shape
    nchunk = rows // row_chunk

    def copy(i):
        return pltpu.make_async_copy(src.at[pl.ds(i * row_chunk, row_chunk), :],
                                     stage_ref.at[i % 2, pl.ds(0, row_chunk), pl.ds(0, cols)],
                                     sem_ref.at[i % 2])

    copy(0).start()
    for i in range(nchunk):
        if i + 1 < nchunk:
            copy(i + 1).start()
        copy(i).wait()
        dst_ref[pl.ds(i * row_chunk, row_chunk), :] = stage_ref[i % 2, 0:row_chunk, 0:cols].astype(BF16)


def _inproj_kernel(x_ref, gain_ref, win_hbm, wsmall_ref, cos_ref, sin_ref, convw_ref,
                   alog_ref, dtb_ref, ret_ref, gdn_ref, small_ref, pbuf0_ref, pbuf1_ref, hb_ref,
                   wmain_ref, stage_ref, sem_ref, *, tiles_per_seq, nsteps):
    s = pl.program_id(0)
    tile = x_ref.shape[1]
    width = HEADS * HEAD_DIM
    nmain = 8 * width
    body = slice(SUBLANES, SUBLANES + tile)
    qkv = slice(4 * width, 7 * width)

    @pl.when(s == 0)
    def _():
        pbuf0_ref[0:SUBLANES, qkv] = jnp.zeros((SUBLANES, 3 * width), F32)
        _stage_bf16(win_hbm.at[0, :, pl.ds(0, 8 * HEADS * HEAD_DIM)], wmain_ref, stage_ref, sem_ref, STAGE_ROWS)

    def stage_a(pa_ref):
        for r in range(tile // ROW_CHUNK):
            rows = slice(r * ROW_CHUNK, (r + 1) * ROW_CHUNK)
            hb_ref[rows, :] = (_rms(x_ref[0, rows, :]) * gain_ref[...]).astype(BF16)
        for c0, c1 in ((0, 2 * width), (2 * width, 4 * width), (4 * width, 7 * width), (7 * width, nmain)):
            pa_ref[body, c0:c1] = _dot(hb_ref[...], wmain_ref[:, c0:c1])
        pa_ref[body, nmain:nmain + LANES] = _dot(hb_ref[...], wsmall_ref[...])

    def stage_b(pa_ref, pb_ref):
        first_of_seq = lax.rem(jnp.maximum(s - 1, 0), tiles_per_seq) == 0
        pb_ref[0:SUBLANES, qkv] = jnp.where(first_of_seq, 0.0, pb_ref[0:SUBLANES, qkv])
        for r in range(tile // ROW_CHUNK):
            out_rows = slice(r * ROW_CHUNK, (r + 1) * ROW_CHUNK)
            rows = slice(SUBLANES + r * ROW_CHUNK, SUBLANES + (r + 1) * ROW_CHUNK)
            cos = cos_ref[out_rows, :]
            sin = sin_ref[out_rows, :]
            for part in range(2):
                for h in range(HEADS):
                    c0 = part * width + h * HEAD_DIM
                    ph = pb_ref[rows, c0:c0 + HEAD_DIM]
                    rot = ph * cos + pltpu.roll(ph, HEAD_DIM // 2, 1) * sin
                    if part == 1:
                        rot = rot * (HEAD_DIM ** -0.5)
                    ret_ref[0, out_rows, c0:c0 + HEAD_DIM] = rot.astype(BF16)
            for c0 in range(2 * width, 4 * width, LANES):
                ret_ref[0, out_rows, c0:c0 + LANES] = pb_ref[rows, c0:c0 + LANES].astype(BF16)
            for c0 in range(0, width, LANES):
                gdn_ref[0, out_rows, 3 * width + c0:3 * width + c0 + LANES] = (
                    pb_ref[rows, 7 * width + c0:7 * width + c0 + LANES].astype(BF16))
            for part in range(3):
                for h in range(HEADS):
                    c0 = part * width + h * HEAD_DIM
                    src_cols = slice(4 * width + c0, 4 * width + c0 + HEAD_DIM)
                    cw = convw_ref[:, c0:c0 + HEAD_DIM]
                    acc = pb_ref[rows, src_cols] * cw[GDN_CONV - 1:GDN_CONV, :]
                    for j in range(GDN_CONV - 1):
                        back = GDN_CONV - 1 - j
                        shifted = pb_ref[rows.start - back:rows.stop - back, src_cols]
                        acc = acc + shifted * cw[j:j + 1, :]
                    act = _silu(acc)
                    if part < 2:
                        act = act * lax.rsqrt(jnp.sum(act * act, axis=-1, keepdims=True) + EPS)
                    if part == 0:
                        act = act * (HEAD_DIM ** -0.5)
                    gdn_ref[0, out_rows, c0:c0 + HEAD_DIM] = act.astype(BF16)
            sm = pb_ref[rows, nmain:nmain + LANES]
            col = lax.broadcasted_iota(jnp.int32, sm.shape, 1)
            beta = _sigmoid(sm)
            g = -jnp.exp(alog_ref[...]) * _softplus(sm + dtb_ref[...])
            small_ref[0, out_rows, :] = jnp.where(col < HEADS, beta, g)
        pa_ref[0:SUBLANES, qkv] = pb_ref[tile:tile + SUBLANES, qkv]

    def step(set_a, set_b, do_a, do_b):
        if do_a:
            stage_a(*set_a)
        if do_b:
            stage_b(*set_a, *set_b)

    _run_pipeline(s, nsteps, (pbuf0_ref,), (pbuf1_ref,), step)


def _mixer_kernel(ret_ref, gdn_ref, small_ref, x_ref, wout_hbm, ogain_ref, out_ref,
                  sret_ref, sgdn_ref, ybuf_ref, wout_ref, stage_ref, sem_ref):
    ti = pl.program_id(1)

    @pl.when((pl.program_id(0) == 0) & (ti == 0))
    def _():
        _stage_bf16(wout_hbm.at[0], wout_ref, stage_ref, sem_ref, STAGE_ROWS)

    tile = x_ref.shape[1]
    width = HEADS * HEAD_DIM
    c = CHUNK

    @pl.when(ti == 0)
    def _():
        sret_ref[...] = jnp.zeros(sret_ref.shape, F32)
        sgdn_ref[...] = jnp.zeros(sgdn_ref.shape, F32)

    row = lax.broadcasted_iota(jnp.int32, (c, c), 0)
    col = lax.broadcasted_iota(jnp.int32, (c, c), 1)
    tril = row >= col
    strict = row > col
    rowf = row.astype(F32)
    relf = (row - col).astype(F32)
    eye = jnp.where(row == col, 1.0, 0.0).astype(F32)
    tri_b = jnp.where(tril, 1.0, 0.0).astype(BF16)
    ogain = ogain_ref[...]
    nblk = tile // c
    probs = [(blk, h) for blk in range(nblk) for h in range(HEADS)]

    def cols(ref, part, blk, h):
        c0 = part * width + h * HEAD_DIM
        return ref[0, blk * c:(blk + 1) * c, c0:c0 + HEAD_DIM]


    sms, gcums, gcum_ts = [], [], []
    for blk in range(nblk):
        sm = small_ref[0, blk * c:(blk + 1) * c, :]
        parts = _dot(tri_b, jnp.concatenate(_split3(sm), axis=1))
        gcum = parts[:, :LANES] + parts[:, LANES:2 * LANES] + parts[:, 2 * LANES:]
        sms.append(sm)
        gcums.append(gcum)
        gcum_ts.append(gcum.T)
    ret_qk = {}
    gdn_kq = {}
    for p in probs:
        blk, h = p
        ret_qk[p] = _dot_nt(cols(ret_ref, 0, blk, h), cols(ret_ref, 1, blk, h))
        k = cols(gdn_ref, 1, blk, h)
        gdn_kq[p] = _dot_nt(jnp.concatenate([k, cols(gdn_ref, 0, blk, h)], axis=0), k)

    ret_iv = {}
    for p in probs:
        blk, h = p
        lg = RET_LOG_GAMMA[h]
        dmask = jnp.where(tril, jnp.exp(lg * jnp.where(tril, relf, 0.0)), 0.0)
        scores = (ret_qk[p] * dmask).astype(BF16)
        k_tail_t = (cols(ret_ref, 1, blk, h).astype(F32) * jnp.exp(lg * (c - 1.0 - rowf))).T.astype(BF16)
        ret_iv[p] = _dot(jnp.concatenate([scores, k_tail_t], axis=0), cols(ret_ref, 2, blk, h))

    lmats, attns, rhs, kts, egcs, glasts = {}, {}, {}, {}, {}, {}
    for p in probs:
        blk, h = p
        gcum, sm = gcums[blk], sms[blk]
        gc_col = jnp.broadcast_to(gcum[:, HEADS + h:HEADS + h + 1], (c, c))
        gc_row = jnp.broadcast_to(gcum_ts[blk][HEADS + h:HEADS + h + 1, :], (c, c))
        g_last = jnp.broadcast_to(gcum[c - 1:c, HEADS + h:HEADS + h + 1], (c, c))
        beta_col = jnp.broadcast_to(sm[:, h:h + 1], (c, c))
        decay = jnp.exp(jnp.where(tril, gc_col - gc_row, NEG_BIG))
        kf = cols(gdn_ref, 1, blk, h).astype(F32)
        egc = jnp.exp(gc_col)
        lmats[p] = jnp.where(strict, gdn_kq[p][:c] * beta_col * decay, 0.0)
        attns[p] = jnp.where(tril, gdn_kq[p][c:] * decay, 0.0).astype(BF16)
        rhs[p] = jnp.concatenate([(cols(gdn_ref, 2, blk, h).astype(F32) * beta_col).astype(BF16),
                                  (kf * (beta_col * egc)).astype(BF16)], axis=1)
        kts[p] = (kf * jnp.exp(g_last - gc_col)).T.astype(BF16)
        egcs[p] = egc
        glasts[p] = g_last

    def same_block(size):
        return (row // size) == (col // size)

    diag = same_block(INV_BASE)
    tinv, power = {}, {}
    for p in probs:
        ld = jnp.where(diag, lmats[p], 0.0)
        tinv[p] = eye - ld
        power[p] = _mm(ld, ld)
    factors = int(math.log2(INV_BASE)) - 1
    for f in range(factors):
        last = f == factors - 1
        for p in probs:
            pw = power[p].astype(BF16)
            if last:
                tinv[p] = tinv[p] + _dot(tinv[p].astype(BF16), pw)
            else:
                both = _dot(jnp.concatenate([tinv[p].astype(BF16), pw], axis=0), pw)
                tinv[p] = tinv[p] + both[:c]
                power[p] = both[c:]
    size = INV_BASE
    while size < c:
        off = same_block(2 * size) & jnp.logical_not(same_block(size))
        ct = {p: _dot(jnp.where(off, lmats[p], 0.0).astype(BF16), tinv[p].astype(BF16)) for p in probs}
        for p in probs:
            tinv[p] = tinv[p] - _dot(tinv[p].astype(BF16), ct[p].astype(BF16))
        size *= 2

    ret_inter = {}
    for h in range(HEADS):
        lg = RET_LOG_GAMMA[h]
        state = sret_ref[h]
        for blk in range(nblk):
            ret_inter[(blk, h)] = _dot(cols(ret_ref, 0, blk, h), state.astype(BF16))
            state = state * math.exp(lg * c) + ret_iv[(blk, h)][c:]
        sret_ref[h] = state

    uw = {p: _dot(tinv[p].astype(BF16), rhs[p]).astype(BF16) for p in probs}
    apn = {}
    for p in probs:
        wu = jnp.concatenate([uw[p][:, HEAD_DIM:], uw[p][:, :HEAD_DIM]], axis=1)
        apn[p] = _dot(jnp.concatenate([attns[p], kts[p]], axis=0), wu)

    for p in probs:
        blk, h = p
        lg = RET_LOG_GAMMA[h]
        o = ret_iv[p][:c] + ret_inter[p] * jnp.exp(lg * (rowf + 1.0))
        y = _rms(o) * _silu(cols(ret_ref, 3, blk, h).astype(F32))
        ybuf_ref[blk * c:(blk + 1) * c, h * HEAD_DIM:(h + 1) * HEAD_DIM] = y.astype(BF16)

    states = [sgdn_ref[h] for h in range(HEADS)]
    for blk in range(nblk):
        for h in range(HEADS):
            p = (blk, h)
            q_eff = (cols(gdn_ref, 0, blk, h).astype(F32) * egcs[p] - apn[p][:c, :HEAD_DIM]).astype(BF16)
            pq = jnp.concatenate([apn[p][c:, :HEAD_DIM].astype(BF16), q_eff], axis=0)
            ps = _dot(pq, states[h].astype(BF16))
            o = ps[c:] + apn[p][:c, HEAD_DIM:]
            states[h] = jnp.exp(glasts[p]) * states[h] - ps[:c] + apn[p][c:, HEAD_DIM:]
            y = _rms(o) * ogain * _silu(cols(gdn_ref, 3, blk, h).astype(F32))
            ybuf_ref[blk * c:(blk + 1) * c, width + h * HEAD_DIM:width + (h + 1) * HEAD_DIM] = y.astype(BF16)
    for h in range(HEADS):
        sgdn_ref[h] = states[h]

    out_ref[0] = x_ref[0] + _dot(ybuf_ref[...], wout_ref[...])


def _rglru_kernel(h_ref, gain_ref, win_hbm, convw_ref, convb_ref, wa_hbm, wx_hbm, ba_ref, bx_ref,
                  lam_ref, wout_hbm, out_ref, state_ref, a_ref, u_ref, hs_ref, xr_ref, hb_ref,
                  hin0_ref, hin1_ref, gate0_ref, gate1_ref, xp0_ref, xp1_ref,
                  win_ref, wout_ref, wax_ref, stage_ref, sem_ref, *, nsteps):
    s = pl.program_id(0)
    rows = h_ref.shape[0]
    d_rnn = lam_ref.shape[1]
    steps = rows // SUBLANES
    hist = (LRU_CONV - 1) * SUBLANES
    nblock = d_rnn // LRU_BLOCK

    @pl.when(s == 0)
    def _():
        state_ref[...] = jnp.zeros(state_ref.shape, F32)
        xp0_ref[0:hist, :] = jnp.zeros((hist, d_rnn), F32)
        _stage_bf16(win_hbm.at[0], win_ref, stage_ref, sem_ref, STAGE_ROWS)
        _stage_bf16(wout_hbm.at[0], wout_ref, stage_ref, sem_ref, STAGE_ROWS)
        for n in range(nblock):
            _stage_bf16(wa_hbm.at[0, n], wax_ref.at[n, :, pl.ds(0, LRU_BLOCK)], stage_ref, sem_ref, LRU_BLOCK)
            _stage_bf16(wx_hbm.at[0, n], wax_ref.at[n, :, pl.ds(LRU_BLOCK, LRU_BLOCK)], stage_ref, sem_ref, LRU_BLOCK)

    def step(set_a, set_b, do_a, do_b):
        hin_a, gate_a, xp_a = set_a
        hin_b, gate_b, xp_b = set_b
        if do_a:
            hin_a[...] = h_ref[...]
            for r in range(rows // ROW_CHUNK):
                rs = slice(r * ROW_CHUNK, (r + 1) * ROW_CHUNK)
                hb_ref[rs, :] = (_rms(h_ref[rs, :]) * gain_ref[...]).astype(BF16)

        ris = []
        slab = 2 * d_rnn // nblock
        for n in range(nblock):
            c0 = n * LRU_BLOCK
            if do_b:
                cw = convw_ref[:, c0:c0 + LRU_BLOCK]
                for r in range(rows // ROW_CHUNK):
                    lo = hist + r * ROW_CHUNK
                    xr = (xp_b[lo:lo + ROW_CHUNK, c0:c0 + LRU_BLOCK] * cw[LRU_CONV - 1:LRU_CONV, :]
                          + convb_ref[:, c0:c0 + LRU_BLOCK])
                    for j in range(LRU_CONV - 1):
                        back = (LRU_CONV - 1 - j) * SUBLANES
                        xr = xr + xp_b[lo - back:lo - back + ROW_CHUNK, c0:c0 + LRU_BLOCK] * cw[j:j + 1, :]
                    xr_ref[r * ROW_CHUNK:(r + 1) * ROW_CHUNK, c0:c0 + LRU_BLOCK] = xr
                ris.append(_dot(xr_ref[:, c0:c0 + LRU_BLOCK].astype(BF16), wax_ref[n]))
            if do_a:
                p0 = n * slab
                proj = _dot(hb_ref[...], win_ref[:, p0:p0 + slab])
                if p0 < d_rnn:
                    gate_a[:, p0:p0 + slab] = proj
                else:
                    xp_a[hist:hist + rows, p0 - d_rnn:p0 - d_rnn + slab] = proj
        if not do_b:
            return
        xp_a[0:hist, :] = xp_b[rows:rows + hist, :]

        sp = _softplus(-lam_ref[...]) * (-LRU_C)
        for n in range(nblock):
            c0 = n * LRU_BLOCK
            ri = ris[n]
            r = _sigmoid(ri[:, :LRU_BLOCK] + ba_ref[:, c0:c0 + LRU_BLOCK])
            i = _sigmoid(ri[:, LRU_BLOCK:] + bx_ref[:, c0:c0 + LRU_BLOCK])
            a = jnp.exp(r * sp[:, c0:c0 + LRU_BLOCK])
            a_ref[:, c0:c0 + LRU_BLOCK] = a
            u_ref[:, c0:c0 + LRU_BLOCK] = jnp.sqrt(1.0 - a * a) * (i * xr_ref[:, c0:c0 + LRU_BLOCK])

        state = state_ref[...]
        for t in range(steps):
            sl = slice(t * SUBLANES, (t + 1) * SUBLANES)
            state = a_ref[sl, :] * state + u_ref[sl, :]
            hs_ref[sl, :] = state
        state_ref[...] = state

        y = (_gelu_tanh(gate_b[...]) * hs_ref[...]).astype(BF16)
        out_ref[...] = hin_b[...] + _dot(y, wout_ref[...])

    _run_pipeline(s, nsteps, (hin0_ref, gate0_ref, xp0_ref), (hin1_ref, gate1_ref, xp1_ref), step)


def _ffn_kernel(h_ref, gain_ref, wup_hbm, convw_ref, convb_ref, wdown_hbm, fgain_ref, out_ref,
                halo_ref, slab_ref, hin0_ref, hin1_ref, hb0_ref, hb1_ref, wup_ref, wdown_ref, stage_ref, sem_ref,
                *, layer, nsteps, d_ff, col_chunk, batch_major_in, final):
    s = pl.program_id(0)
    halo = (FFN_CONV - 1) * SUBLANES
    rows = hin0_ref.shape[0]

    @pl.when(s == 0)
    def _():
        halo_ref[...] = jnp.zeros(halo_ref.shape, F32)
        _stage_bf16(wup_hbm.at[layer], wup_ref, stage_ref, sem_ref, STAGE_ROWS)
        _stage_bf16(wdown_hbm.at[layer], wdown_ref, stage_ref, sem_ref, STAGE_ROWS)

    def stage_a(hin_a, hb_a):
        if batch_major_in:
            steps = h_ref.shape[1]
            nslab = h_ref.shape[2] // LANES
            for i in range(nslab):
                for b in range(SUBLANES):
                    slab_ref[i, pl.ds(b, steps, stride=SUBLANES), :] = h_ref[b, :, i * LANES:(i + 1) * LANES]
            for i in range(nslab):
                hin_a[:, i * LANES:(i + 1) * LANES] = slab_ref[i]
        else:
            hin_a[...] = h_ref[...]
        for r in range(rows // ROW_CHUNK):
            rs = slice(r * ROW_CHUNK, (r + 1) * ROW_CHUNK)
            hb_a[rs, :] = (_rms(hin_a[rs, :]) * gain_ref[...]).astype(BF16)

    def stage_b(hin_b, hb_b):
        def conv(cur, c0):
            prev = halo_ref[:, c0:c0 + col_chunk]
            cw = convw_ref[:, c0:c0 + col_chunk]
            out = cur * cw[FFN_CONV - 1:FFN_CONV, :] + convb_ref[:, c0:c0 + col_chunk]
            for j in range(FFN_CONV - 1):
                back = (FFN_CONV - 1 - j) * SUBLANES
                shifted = jnp.concatenate([prev[halo - back:, :], cur[:rows - back, :]], axis=0)
                out = out + shifted * cw[j:j + 1, :]
            halo_ref[:, c0:c0 + col_chunk] = cur[rows - halo:, :]
            return out

        def up(j):
            g0 = j * col_chunk
            v0 = d_ff + j * col_chunk
            return _dot(hb_b[...], wup_ref[:, g0:g0 + col_chunk]), _dot(hb_b[...], wup_ref[:, v0:v0 + col_chunk])

        nchunk = d_ff // col_chunk
        acc = jnp.zeros((rows, hin_b.shape[1]), F32)
        queue = [up(j) for j in range(min(UP_AHEAD, nchunk))]
        for j in range(nchunk):
            if j + UP_AHEAD < nchunk:
                queue.append(up(j + UP_AHEAD))
            cur = queue.pop(0)
            g0 = j * col_chunk
            gate = conv(cur[0], g0)
            val = conv(cur[1], d_ff + g0)
            act = (_silu(gate) * val).astype(BF16)
            acc = acc + _dot(act, wdown_ref[g0:g0 + col_chunk, :])
        res = hin_b[...] + acc
        if not final:
            out_ref[...] = res
        else:
            fin = _rms(res) * fgain_ref[...]
            steps = rows // SUBLANES
            for i in range(fin.shape[1] // LANES):
                slab_ref[i] = fin[:, i * LANES:(i + 1) * LANES]
            for i in range(fin.shape[1] // LANES):
                for b in range(SUBLANES):
                    out_ref[b, :, i * LANES:(i + 1) * LANES] = slab_ref[i, pl.ds(b, steps, stride=SUBLANES), :]

    def step(set_a, set_b, do_a, do_b):
        if do_a:
            stage_a(*set_a)
        if do_b:
            stage_b(*set_b)

    _run_pipeline(s, nsteps, (hin0_ref, hb0_ref), (hin1_ref, hb1_ref), step)


def _const_spec(shape):
    zeros = (0,) * len(shape)
    return pl.BlockSpec(shape, lambda *_: zeros)


def _params(sem, flags=None):
    return pltpu.CompilerParams(dimension_semantics=sem, vmem_limit_bytes=VMEM_LIMIT, flags=flags)


def _rope_tables(t):
    half = HEAD_DIM // 2
    inv_freq = ROPE_BASE ** (-jnp.arange(half, dtype=F32) / half)
    ang = jnp.arange(t, dtype=F32)[:, None] * inv_freq[None, :]
    cos = jnp.cos(ang)
    sin = jnp.sin(ang)
    return jnp.concatenate([cos, cos], axis=-1), jnp.concatenate([-sin, sin], axis=-1)


def _layer0_inproj(x, gain, w_in_all, conv_w, a_log, dt_bias, tile):
    b, t, d = x.shape
    width = HEADS * HEAD_DIM
    wsmall = jnp.zeros((d, LANES), F32).at[:, :2 * HEADS].set(w_in_all[0, :, 8 * width:]).astype(BF16)
    alog = jnp.zeros((1, LANES), F32).at[0, HEADS:2 * HEADS].set(a_log)
    dtb = jnp.zeros((1, LANES), F32).at[0, HEADS:2 * HEADS].set(dt_bias)
    cos, sin = _rope_tables(t)
    tiles_per_seq = t // tile
    ntiles = b * tiles_per_seq

    def in_tile(s):
        s = jnp.minimum(s, ntiles - 1)
        return s // tiles_per_seq, s % tiles_per_seq

    def out_tile(s):
        s = jnp.maximum(s - 1, 0)
        return s // tiles_per_seq, s % tiles_per_seq

    out_map = lambda s: (*out_tile(s), 0)
    pbuf = pltpu.VMEM((SUBLANES + tile, 8 * width + LANES), F32)
    return pl.pallas_call(
        functools.partial(_inproj_kernel, tiles_per_seq=tiles_per_seq, nsteps=ntiles + 1),
        grid=(ntiles + 1,),
        in_specs=[
            pl.BlockSpec((1, tile, d), lambda s: (*in_tile(s), 0)),
            _const_spec((1, d)),
            pl.BlockSpec(memory_space=pl.ANY),
            _const_spec((d, LANES)),
            pl.BlockSpec((tile, HEAD_DIM), lambda s: (out_tile(s)[1], 0)),
            pl.BlockSpec((tile, HEAD_DIM), lambda s: (out_tile(s)[1], 0)),
            _const_spec((GDN_CONV, 3 * width)),
            _const_spec((1, LANES)),
            _const_spec((1, LANES)),
        ],
        out_specs=[
            pl.BlockSpec((1, tile, 4 * width), out_map),
            pl.BlockSpec((1, tile, 4 * width), out_map),
            pl.BlockSpec((1, tile, LANES), out_map),
        ],
        out_shape=[
            jax.ShapeDtypeStruct((b, t, 4 * width), BF16),
            jax.ShapeDtypeStruct((b, t, 4 * width), BF16),
            jax.ShapeDtypeStruct((b, t, LANES), F32),
        ],
        scratch_shapes=[pbuf, pbuf, pltpu.VMEM((tile, d), BF16),
                        pltpu.VMEM((d, 8 * width), BF16),
                        pltpu.VMEM((2, STAGE_ROWS, 8 * width), F32),
                        pltpu.SemaphoreType.DMA((2,))],
        compiler_params=_params(("arbitrary",)),
        name="l0_inproj",
    )(x, gain.reshape(1, d), w_in_all, wsmall, cos, sin, conv_w, alog, dtb)


def _layer0_mixer(ret, gdn, small, x, w_out_all, out_gain, tile):
    b, t, d = x.shape
    width = HEADS * HEAD_DIM
    return pl.pallas_call(
        _mixer_kernel,
        grid=(b, t // tile),
        in_specs=[
            pl.BlockSpec((1, tile, 4 * width), lambda i, j: (i, j, 0)),
            pl.BlockSpec((1, tile, 4 * width), lambda i, j: (i, j, 0)),
            pl.BlockSpec((1, tile, LANES), lambda i, j: (i, j, 0)),
            pl.BlockSpec((1, tile, d), lambda i, j: (i, j, 0)),
            pl.BlockSpec(memory_space=pl.ANY),
            _const_spec((1, HEAD_DIM)),
        ],
        out_specs=pl.BlockSpec((1, tile, d), lambda i, j: (i, j, 0)),
        out_shape=jax.ShapeDtypeStruct((b, t, d), F32),
        scratch_shapes=[
            pltpu.VMEM((HEADS, HEAD_DIM, HEAD_DIM), F32),
            pltpu.VMEM((HEADS, HEAD_DIM, HEAD_DIM), F32),
            pltpu.VMEM((tile, 2 * width), BF16),
            pltpu.VMEM((2 * width, d), BF16),
            pltpu.VMEM((2, STAGE_ROWS, d), F32),
            pltpu.SemaphoreType.DMA((2,)),
        ],
        compiler_params=_params(("arbitrary", "arbitrary")),
        name="l0_mixer",
    )(ret, gdn, small, x, w_out_all, out_gain.reshape(1, HEAD_DIM))


def _rglru(h_tm, gain, w_in_all, conv_w, conv_b, w_a_all, b_a, w_x_all, b_x, lam, w_out_all, steps):
    n, d = h_tm.shape
    d_rnn = lam.shape[0]
    rows = steps * SUBLANES
    ntiles = n // rows
    nblock = d_rnn // LRU_BLOCK
    hist = (LRU_CONV - 1) * SUBLANES
    any_spec = pl.BlockSpec(memory_space=pl.ANY)
    return pl.pallas_call(
        functools.partial(_rglru_kernel, nsteps=ntiles + 1),
        grid=(ntiles + 1,),
        in_specs=[
            pl.BlockSpec((rows, d), lambda s: (jnp.minimum(s, ntiles - 1), 0)),
            _const_spec((1, d)),
            any_spec,
            _const_spec((LRU_CONV, d_rnn)),
            _const_spec((1, d_rnn)),
            any_spec,
            any_spec,
            _const_spec((1, d_rnn)),
            _const_spec((1, d_rnn)),
            _const_spec((1, d_rnn)),
            any_spec,
        ],
        out_specs=pl.BlockSpec((rows, d), lambda s: (jnp.maximum(s - 1, 0), 0)),
        out_shape=jax.ShapeDtypeStruct((n, d), F32),
        scratch_shapes=[
            pltpu.VMEM((SUBLANES, d_rnn), F32),
            pltpu.VMEM((rows, d_rnn), F32),
            pltpu.VMEM((rows, d_rnn), F32),
            pltpu.VMEM((rows, d_rnn), F32),
            pltpu.VMEM((rows, d_rnn), F32),
            pltpu.VMEM((rows, d), BF16),
            pltpu.VMEM((rows, d), F32), pltpu.VMEM((rows, d), F32),
            pltpu.VMEM((rows, d_rnn), F32), pltpu.VMEM((rows, d_rnn), F32),
            pltpu.VMEM((hist + rows, d_rnn), F32), pltpu.VMEM((hist + rows, d_rnn), F32),
            pltpu.VMEM((d, 2 * d_rnn), BF16),
            pltpu.VMEM((d_rnn, d), BF16),
            pltpu.VMEM((nblock, LRU_BLOCK, 2 * LRU_BLOCK), BF16),
            pltpu.VMEM((2, STAGE_ROWS, 2 * d_rnn), F32),
            pltpu.SemaphoreType.DMA((2,)),
        ],
        compiler_params=_params(("arbitrary",)),
        name="l1_rglru",
    )(h_tm, gain.reshape(1, d), w_in_all, conv_w, conv_b.reshape(1, d_rnn), w_a_all, w_x_all,
      b_a.reshape(1, d_rnn), b_x.reshape(1, d_rnn), lam.reshape(1, d_rnn), w_out_all)


def _ffn(h, gain, w_up_all, conv_w, conv_b, w_down_all, final_gain, layer, steps, batch_major_in, final):
    d = h.shape[-1]
    n = h.size // d
    d_ff = w_down_all.shape[1]
    rows = steps * SUBLANES
    col_chunk = 2 * LANES
    ntiles = n // rows
    kern = functools.partial(_ffn_kernel, layer=layer, nsteps=ntiles + 1, d_ff=d_ff, col_chunk=col_chunk,
                             batch_major_in=batch_major_in, final=final)
    scratch = [pltpu.VMEM(((FFN_CONV - 1) * SUBLANES, 2 * d_ff), F32),
               pltpu.VMEM((d // LANES, rows, LANES), F32),
               pltpu.VMEM((rows, d), F32), pltpu.VMEM((rows, d), F32),
               pltpu.VMEM((rows, d), BF16), pltpu.VMEM((rows, d), BF16),
               pltpu.VMEM((d, 2 * d_ff), BF16),
               pltpu.VMEM((d_ff, d), BF16),
               pltpu.VMEM((2, STAGE_ROWS, 2 * d_ff), F32),
               pltpu.SemaphoreType.DMA((2,))]
    in_tile = lambda s: jnp.minimum(s, ntiles - 1)
    out_tile = lambda s: jnp.maximum(s - 1, 0)
    bm_spec = lambda tile_of: pl.BlockSpec((SUBLANES, steps, d), lambda s: (0, tile_of(s), 0))
    tm_spec = lambda tile_of: pl.BlockSpec((rows, d), lambda s: (tile_of(s), 0))
    if final:
        out_spec = bm_spec(out_tile)
        out_shape = jax.ShapeDtypeStruct((SUBLANES, n // SUBLANES, d), F32)
    else:
        out_spec = tm_spec(out_tile)
        out_shape = jax.ShapeDtypeStruct((n, d), F32)
    return pl.pallas_call(
        kern,
        grid=(ntiles + 1,),
        in_specs=[
            bm_spec(in_tile) if batch_major_in else tm_spec(in_tile),
            _const_spec((1, d)),
            pl.BlockSpec(memory_space=pl.ANY),
            _const_spec((FFN_CONV, 2 * d_ff)),
            _const_spec((1, 2 * d_ff)),
            pl.BlockSpec(memory_space=pl.ANY),
            _const_spec((1, d)),
        ],
        out_specs=out_spec,
        out_shape=out_shape,
        scratch_shapes=scratch,
        compiler_params=_params(("arbitrary",)),
        name="ffn_final" if final else "ffn",
    )(h, gain.reshape(1, d), w_up_all, conv_w, conv_b.reshape(1, 2 * d_ff),
      w_down_all, final_gain.reshape(1, d))


def kernel(x, norm_mix, norm_ffn, ret_gdn_w_in, gdn_conv_w, gdn_a_log, gdn_dt_bias, gdn_out_gain,
           ret_gdn_w_out, lru_w_in, lru_conv_w, lru_conv_b, lru_w_a, lru_b_a, lru_w_x, lru_b_x,
           lru_lambda, lru_w_out, ffn_w_up, ffn_conv_w, ffn_conv_b, ffn_w_down, norm_final):
    b, t, d = x.shape
    assert b == SUBLANES and norm_mix.shape[0] == 2
    tile_in = 512
    tile_mix = 512
    steps = 32
    steps_lru = 64

    ret, gdn, small = _layer0_inproj(x, norm_mix[0], ret_gdn_w_in, gdn_conv_w[0], gdn_a_log[0],
                                     gdn_dt_bias[0], tile_in)
    h = _layer0_mixer(ret, gdn, small, x, ret_gdn_w_out, gdn_out_gain[0], tile_mix)
    h = _ffn(h, norm_ffn[0], ffn_w_up, ffn_conv_w[0], ffn_conv_b[0], ffn_w_down, norm_final,
             0, steps, batch_major_in=True, final=False)
    h = _rglru(h, norm_mix[1], lru_w_in, lru_conv_w[0], lru_conv_b[0], lru_w_a, lru_b_a[0],
               lru_w_x, lru_b_x[0], lru_lambda[0], lru_w_out, steps_lru)
    return _ffn(h, norm_ffn[1], ffn_w_up, ffn_conv_w[1], ffn_conv_b[1], ffn_w_down, norm_final,
                1, steps, batch_major_in=False, final=True)
```

```python
import functools
import math

import jax
import jax.numpy as jnp
from jax import lax
from jax.experimental import pallas as pl
from jax.experimental.pallas import tpu as pltpu

F32 = jnp.float32
BF16 = jnp.bfloat16

EPS = 1e-6
ROPE_BASE = 10000.0
HEADS = 4
HEAD_DIM = 128
CHUNK = 128
UP_AHEAD = 2
STAGE_BYTES = 3 * 1024 * 1024
ROW_CHUNK = 64
INV_BASE = 16
GDN_CONV = 4
LRU_CONV = 4
FFN_CONV = 3
LRU_C = 8.0
LRU_BLOCK = 128
SUBLANES = 8
LANES = 128
NEG_BIG = -1e30
VMEM_LIMIT = 56 * 1024 * 1024

RET_LOG_GAMMA = tuple(math.log1p(-2.0 ** (-5.0 - h)) for h in range(HEADS))


def _rms(x):
    return x * lax.rsqrt(jnp.mean(x * x, axis=-1, keepdims=True) + EPS)


def _sigmoid(x):
    return 1.0 / (1.0 + jnp.exp(-x))


def _silu(x):
    return x * _sigmoid(x)


def _softplus(x):
    return jnp.maximum(x, 0.0) + jnp.log1p(jnp.exp(-jnp.abs(x)))


def _gelu_tanh(x):
    c = 2.0 * math.sqrt(2.0 / math.pi)
    return x * _sigmoid(x * (c + (0.044715 * c) * (x * x)))


def _dot(a, b):
    return jnp.dot(a, b, preferred_element_type=F32)


def _dot_nt(a, b):
    return lax.dot_general(a, b, (((1,), (1,)), ((), ())), preferred_element_type=F32)


def _mm(a, b):
    return _dot(a.astype(BF16), b.astype(BF16))


def _split3(x):
    hi = x.astype(BF16)
    r1 = x - hi.astype(F32)
    mid = r1.astype(BF16)
    lo = (r1 - mid.astype(F32)).astype(BF16)
    return hi, mid, lo


def _run_pipeline(s, nsteps, set0, set1, step):
    last = nsteps - 1
    parity = lax.rem(s, 2)

    @pl.when(s == 0)
    def _():
        step(set0, set1, True, False)

    @pl.when((s > 0) & (s < last) & (parity == 0))
    def _():
        step(set0, set1, True, True)

    @pl.when((s < last) & (parity == 1))
    def _():
        step(set1, set0, True, True)

    @pl.when(s == last)
    def _():
        if last % 2 == 0:
            step(set0, set1, False, True)
        else:
            step(set1, set0, False, True)


def _stage_rows(rows, cols):
    fits = [r for r in range(SUBLANES, rows + 1, SUBLANES) if rows % r == 0 and r * cols * 4 <= STAGE_BYTES]
    return fits[-1]


def _stage_scratch(rows, cols):
    return pltpu.VMEM((2, _stage_rows(rows, cols), cols), F32)


def _stage_bf16(src, dst_ref, stage_ref, sem_ref):
    rows, cols = src.shape
    row_chunk = stage_ref.shape[1]
    nchunk = rows // row_chunk

    def copy(i):
        return pltpu.make_async_copy(src.at[pl.ds(i * row_chunk, row_chunk), :], stage_ref.at[i % 2],
                                     sem_ref.at[i % 2])

    copy(0).start()
    for i in range(nchunk):
        if i + 1 < nchunk:
            copy(i + 1).start()
        copy(i).wait()
        dst_ref[pl.ds(i * row_chunk, row_chunk), :] = stage_ref[i % 2].astype(BF16)


def _stage_gate_blocks(wa, wx, wax_ref, stage_ref, sem_ref):
    copies = [pltpu.make_async_copy(w, stage_ref.at[i], sem_ref.at[i]) for i, w in enumerate((wa, wx))]
    for c in copies:
        c.start()
    for c in copies:
        c.wait()
    for n in range(wax_ref.shape[0]):
        wax_ref[n, :, 0:LRU_BLOCK] = stage_ref[0, n].astype(BF16)
        wax_ref[n, :, LRU_BLOCK:2 * LRU_BLOCK] = stage_ref[1, n].astype(BF16)


def _inproj_kernel(x_ref, gain_ref, win_hbm, wsmall_ref, cos_ref, sin_ref, convw_ref,
                   alog_ref, dtb_ref, ret_ref, gdn_ref, small_ref, pbuf0_ref, pbuf1_ref, hb_ref,
                   wmain_ref, stage_ref, sem_ref, *, tiles_per_seq, nsteps):
    s = pl.program_id(0)
    tile = x_ref.shape[1]
    width = HEADS * HEAD_DIM
    nmain = 8 * width
    body = slice(SUBLANES, SUBLANES + tile)
    qkv = slice(4 * width, 7 * width)

    @pl.when(s == 0)
    def _():
        pbuf0_ref[0:SUBLANES, qkv] = jnp.zeros((SUBLANES, 3 * width), F32)
        _stage_bf16(win_hbm.at[0, :, pl.ds(0, 8 * HEADS * HEAD_DIM)], wmain_ref, stage_ref, sem_ref)

    def stage_a(pa_ref):
        for r in range(tile // ROW_CHUNK):
            rows = slice(r * ROW_CHUNK, (r + 1) * ROW_CHUNK)
            hb_ref[rows, :] = (_rms(x_ref[0, rows, :]) * gain_ref[...]).astype(BF16)
        for c0, c1 in ((0, 2 * width), (2 * width, 4 * width), (4 * width, 7 * width), (7 * width, nmain)):
            pa_ref[body, c0:c1] = _dot(hb_ref[...], wmain_ref[:, c0:c1])
        pa_ref[body, nmain:nmain + LANES] = _dot(hb_ref[...], wsmall_ref[...])

    def stage_b(pa_ref, pb_ref):
        first_of_seq = lax.rem(jnp.maximum(s - 1, 0), tiles_per_seq) == 0
        pb_ref[0:SUBLANES, qkv] = jnp.where(first_of_seq, 0.0, pb_ref[0:SUBLANES, qkv])
        for r in range(tile // ROW_CHUNK):
            out_rows = slice(r * ROW_CHUNK, (r + 1) * ROW_CHUNK)
            rows = slice(SUBLANES + r * ROW_CHUNK, SUBLANES + (r + 1) * ROW_CHUNK)
            cos = cos_ref[out_rows, :]
            sin = sin_ref[out_rows, :]
            for part in range(2):
                for h in range(HEADS):
                    c0 = part * width + h * HEAD_DIM
                    ph = pb_ref[rows, c0:c0 + HEAD_DIM]
                    rot = ph * cos + pltpu.roll(ph, HEAD_DIM // 2, 1) * sin
                    if part == 1:
                        rot = rot * (HEAD_DIM ** -0.5)
                    ret_ref[0, out_rows, c0:c0 + HEAD_DIM] = rot.astype(BF16)
            for c0 in range(2 * width, 4 * width, LANES):
                ret_ref[0, out_rows, c0:c0 + LANES] = pb_ref[rows, c0:c0 + LANES].astype(BF16)
            for c0 in range(0, width, LANES):
                gdn_ref[0, out_rows, 3 * width + c0:3 * width + c0 + LANES] = (
                    pb_ref[rows, 7 * width + c0:7 * width + c0 + LANES].astype(BF16))
            for part in range(3):
                for h in range(HEADS):
                    c0 = part * width + h * HEAD_DIM
                    src_cols = slice(4 * width + c0, 4 * width + c0 + HEAD_DIM)
                    cw = convw_ref[:, c0:c0 + HEAD_DIM]
                    acc = pb_ref[rows, src_cols] * cw[GDN_CONV - 1:GDN_CONV, :]
                    for j in range(GDN_CONV - 1):
                        back = GDN_CONV - 1 - j
                        shifted = pb_ref[rows.start - back:rows.stop - back, src_cols]
                        acc = acc + shifted * cw[j:j + 1, :]
                    act = _silu(acc)
                    if part < 2:
                        act = act * lax.rsqrt(jnp.sum(act * act, axis=-1, keepdims=True) + EPS)
                    if part == 0:
                        act = act * (HEAD_DIM ** -0.5)
                    gdn_ref[0, out_rows, c0:c0 + HEAD_DIM] = act.astype(BF16)
            sm = pb_ref[rows, nmain:nmain + LANES]
            col = lax.broadcasted_iota(jnp.int32, sm.shape, 1)
            beta = _sigmoid(sm)
            g = -jnp.exp(alog_ref[...]) * _softplus(sm + dtb_ref[...])
            small_ref[0, out_rows, :] = jnp.where(col < HEADS, beta, g)
        pa_ref[0:SUBLANES, qkv] = pb_ref[tile:tile + SUBLANES, qkv]

    def step(set_a, set_b, do_a, do_b):
        if do_a:
            stage_a(*set_a)
        if do_b:
            stage_b(*set_a, *set_b)

    _run_pipeline(s, nsteps, (pbuf0_ref,), (pbuf1_ref,), step)


def _mixer_kernel(ret_ref, gdn_ref, small_ref, x_ref, wout_hbm, ogain_ref, out_ref,
                  sret_ref, sgdn_ref, ybuf_ref, wout_ref, stage_ref, sem_ref):
    ti = pl.program_id(1)

    @pl.when((pl.program_id(0) == 0) & (ti == 0))
    def _():
        _stage_bf16(wout_hbm.at[0], wout_ref, stage_ref, sem_ref)

    tile = x_ref.shape[1]
    width = HEADS * HEAD_DIM
    c = CHUNK

    @pl.when(ti == 0)
    def _():
        sret_ref[...] = jnp.zeros(sret_ref.shape, F32)
        sgdn_ref[...] = jnp.zeros(sgdn_ref.shape, F32)

    row = lax.broadcasted_iota(jnp.int32, (c, c), 0)
    col = lax.broadcasted_iota(jnp.int32, (c, c), 1)
    tril = row >= col
    strict = row > col
    rowf = row.astype(F32)
    relf = (row - col).astype(F32)
    eye = jnp.where(row == col, 1.0, 0.0).astype(F32)
    tri_b = jnp.where(tril, 1.0, 0.0).astype(BF16)
    ogain = ogain_ref[...]
    nblk = tile // c
    probs = [(blk, h) for blk in range(nblk) for h in range(HEADS)]

    def cols(ref, part, blk, h):
        c0 = part * width + h * HEAD_DIM
        return ref[0, blk * c:(blk + 1) * c, c0:c0 + HEAD_DIM]


    sms, gcums, gcum_ts = [], [], []
    for blk in range(nblk):
        sm = small_ref[0, blk * c:(blk + 1) * c, :]
        parts = _dot(tri_b, jnp.concatenate(_split3(sm), axis=1))
        gcum = parts[:, :LANES] + parts[:, LANES:2 * LANES] + parts[:, 2 * LANES:]
        sms.append(sm)
        gcums.append(gcum)
        gcum_ts.append(gcum.T)
    ret_qk = {}
    gdn_kq = {}
    for p in probs:
        blk, h = p
        ret_qk[p] = _dot_nt(cols(ret_ref, 0, blk, h), cols(ret_ref, 1, blk, h))
        k = cols(gdn_ref, 1, blk, h)
        gdn_kq[p] = _dot_nt(jnp.concatenate([k, cols(gdn_ref, 0, blk, h)], axis=0), k)

    ret_iv = {}
    for p in probs:
        blk, h = p
        lg = RET_LOG_GAMMA[h]
        dmask = jnp.where(tril, jnp.exp(lg * jnp.where(tril, relf, 0.0)), 0.0)
        scores = (ret_qk[p] * dmask).astype(BF16)
        k_tail_t = (cols(ret_ref, 1, blk, h).astype(F32) * jnp.exp(lg * (c - 1.0 - rowf))).T.astype(BF16)
        ret_iv[p] = _dot(jnp.concatenate([scores, k_tail_t], axis=0), cols(ret_ref, 2, blk, h))

    lmats, attns, rhs, kts, egcs, glasts = {}, {}, {}, {}, {}, {}
    for p in probs:
        blk, h = p
        gcum, sm = gcums[blk], sms[blk]
        gc_col = jnp.broadcast_to(gcum[:, HEADS + h:HEADS + h + 1], (c, c))
        gc_row = jnp.broadcast_to(gcum_ts[blk][HEADS + h:HEADS + h + 1, :], (c, c))
        g_last = jnp.broadcast_to(gcum[c - 1:c, HEADS + h:HEADS + h + 1], (c, c))
        beta_col = jnp.broadcast_to(sm[:, h:h + 1], (c, c))
        decay = jnp.exp(jnp.where(tril, gc_col - gc_row, NEG_BIG))
        kf = cols(gdn_ref, 1, blk, h).astype(F32)
        egc = jnp.exp(gc_col)
        lmats[p] = jnp.where(strict, gdn_kq[p][:c] * beta_col * decay, 0.0)
        attns[p] = jnp.where(tril, gdn_kq[p][c:] * decay, 0.0).astype(BF16)
        rhs[p] = jnp.concatenate([(cols(gdn_ref, 2, blk, h).astype(F32) * beta_col).astype(BF16),
                                  (kf * (beta_col * egc)).astype(BF16)], axis=1)
        kts[p] = (kf * jnp.exp(g_last - gc_col)).T.astype(BF16)
        egcs[p] = egc
        glasts[p] = g_last

    def same_block(size):
        return (row // size) == (col // size)

    diag = same_block(INV_BASE)
    tinv, power = {}, {}
    for p in probs:
        ld = jnp.where(diag, lmats[p], 0.0)
        tinv[p] = eye - ld
        power[p] = _mm(ld, ld)
    factors = int(math.log2(INV_BASE)) - 1
    for f in range(factors):
        last = f == factors - 1
        for p in probs:
            pw = power[p].astype(BF16)
            if last:
                tinv[p] = tinv[p] + _dot(tinv[p].astype(BF16), pw)
            else:
                both = _dot(jnp.concatenate([tinv[p].astype(BF16), pw], axis=0), pw)
                tinv[p] = tinv[p] + both[:c]
                power[p] = both[c:]
    size = INV_BASE
    while size < c:
        off = same_block(2 * size) & jnp.logical_not(same_block(size))
        ct = {p: _dot(jnp.where(off, lmats[p], 0.0).astype(BF16), tinv[p].astype(BF16)) for p in probs}
        for p in probs:
            tinv[p] = tinv[p] - _dot(tinv[p].astype(BF16), ct[p].astype(BF16))
        size *= 2

    ret_inter = {}
    for h in range(HEADS):
        lg = RET_LOG_GAMMA[h]
        state = sret_ref[h]
        for blk in range(nblk):
            ret_inter[(blk, h)] = _dot(cols(ret_ref, 0, blk, h), state.astype(BF16))
            state = state * math.exp(lg * c) + ret_iv[(blk, h)][c:]
        sret_ref[h] = state

    uw = {p: _dot(tinv[p].astype(BF16), rhs[p]).astype(BF16) for p in probs}
    apn = {}
    for p in probs:
        wu = jnp.concatenate([uw[p][:, HEAD_DIM:], uw[p][:, :HEAD_DIM]], axis=1)
        apn[p] = _dot(jnp.concatenate([attns[p], kts[p]], axis=0), wu)

    for p in probs:
        blk, h = p
        lg = RET_LOG_GAMMA[h]
        o = ret_iv[p][:c] + ret_inter[p] * jnp.exp(lg * (rowf + 1.0))
        y = _rms(o) * _silu(cols(ret_ref, 3, blk, h).astype(F32))
        ybuf_ref[blk * c:(blk + 1) * c, h * HEAD_DIM:(h + 1) * HEAD_DIM] = y.astype(BF16)

    states = [sgdn_ref[h] for h in range(HEADS)]
    for blk in range(nblk):
        for h in range(HEADS):
            p = (blk, h)
            q_eff = (cols(gdn_ref, 0, blk, h).astype(F32) * egcs[p] - apn[p][:c, :HEAD_DIM]).astype(BF16)
            pq = jnp.concatenate([apn[p][c:, :HEAD_DIM].astype(BF16), q_eff], axis=0)
            ps = _dot(pq, states[h].astype(BF16))
            o = ps[c:] + apn[p][:c, HEAD_DIM:]
            states[h] = jnp.exp(glasts[p]) * states[h] - ps[:c] + apn[p][c:, HEAD_DIM:]
            y = _rms(o) * ogain * _silu(cols(gdn_ref, 3, blk, h).astype(F32))
            ybuf_ref[blk * c:(blk + 1) * c, width + h * HEAD_DIM:width + (h + 1) * HEAD_DIM] = y.astype(BF16)
    for h in range(HEADS):
        sgdn_ref[h] = states[h]

    out_ref[0] = x_ref[0] + _dot(ybuf_ref[...], wout_ref[...])


def _rglru_kernel(h_ref, gain_ref, win_hbm, convw_ref, convb_ref, wa_hbm, wx_hbm, ba_ref, bx_ref,
                  lam_ref, wout_hbm, out_ref, state_ref, a_ref, u_ref, hs_ref, xr_ref, hb_ref,
                  hin0_ref, hin1_ref, gate0_ref, gate1_ref, xp0_ref, xp1_ref,
                  win_ref, wout_ref, wax_ref, stage_in_ref, stage_out_ref, stage_blk_ref, sem_ref, *, nsteps):
    s = pl.program_id(0)
    rows = h_ref.shape[0]
    d_rnn = lam_ref.shape[1]
    steps = rows // SUBLANES
    hist = (LRU_CONV - 1) * SUBLANES
    nblock = d_rnn // LRU_BLOCK

    @pl.when(s == 0)
    def _():
        state_ref[...] = jnp.zeros(state_ref.shape, F32)
        xp0_ref[0:hist, :] = jnp.zeros((hist, d_rnn), F32)
        _stage_bf16(win_hbm.at[0], win_ref, stage_in_ref, sem_ref)
        _stage_bf16(wout_hbm.at[0], wout_ref, stage_out_ref, sem_ref)
        _stage_gate_blocks(wa_hbm.at[0], wx_hbm.at[0], wax_ref, stage_blk_ref, sem_ref)

    def step(set_a, set_b, do_a, do_b):
        hin_a, gate_a, xp_a = set_a
        hin_b, gate_b, xp_b = set_b
        if do_a:
            hin_a[...] = h_ref[...]
            for r in range(rows // ROW_CHUNK):
                rs = slice(r * ROW_CHUNK, (r + 1) * ROW_CHUNK)
                hb_ref[rs, :] = (_rms(h_ref[rs, :]) * gain_ref[...]).astype(BF16)

        ris = []
        slab = 2 * d_rnn // nblock
        for n in range(nblock):
            c0 = n * LRU_BLOCK
            if do_b:
                cw = convw_ref[:, c0:c0 + LRU_BLOCK]
                for r in range(rows // ROW_CHUNK):
                    lo = hist + r * ROW_CHUNK
                    xr = (xp_b[lo:lo + ROW_CHUNK, c0:c0 + LRU_BLOCK] * cw[LRU_CONV - 1:LRU_CONV, :]
                          + convb_ref[:, c0:c0 + LRU_BLOCK])
                    for j in range(LRU_CONV - 1):
                        back = (LRU_CONV - 1 - j) * SUBLANES
                        xr = xr + xp_b[lo - back:lo - back + ROW_CHUNK, c0:c0 + LRU_BLOCK] * cw[j:j + 1, :]
                    xr_ref[r * ROW_CHUNK:(r + 1) * ROW_CHUNK, c0:c0 + LRU_BLOCK] = xr
                ris.append(_dot(xr_ref[:, c0:c0 + LRU_BLOCK].astype(BF16), wax_ref[n]))
            if do_a:
                p0 = n * slab
                proj = _dot(hb_ref[...], win_ref[:, p0:p0 + slab])
                if p0 < d_rnn:
                    gate_a[:, p0:p0 + slab] = proj
                else:
                    xp_a[hist:hist + rows, p0 - d_rnn:p0 - d_rnn + slab] = proj
        if not do_b:
            return
        xp_a[0:hist, :] = xp_b[rows:rows + hist, :]

        sp = _softplus(-lam_ref[...]) * (-LRU_C)
        for n in range(nblock):
            c0 = n * LRU_BLOCK
            ri = ris[n]
            r = _sigmoid(ri[:, :LRU_BLOCK] + ba_ref[:, c0:c0 + LRU_BLOCK])
            i = _sigmoid(ri[:, LRU_BLOCK:] + bx_ref[:, c0:c0 + LRU_BLOCK])
            a = jnp.exp(r * sp[:, c0:c0 + LRU_BLOCK])
            a_ref[:, c0:c0 + LRU_BLOCK] = a
            u_ref[:, c0:c0 + LRU_BLOCK] = jnp.sqrt(1.0 - a * a) * (i * xr_ref[:, c0:c0 + LRU_BLOCK])

        state = state_ref[...]
        for t in range(steps):
            sl = slice(t * SUBLANES, (t + 1) * SUBLANES)
            state = a_ref[sl, :] * state + u_ref[sl, :]
            hs_ref[sl, :] = state
        state_ref[...] = state

        y = (_gelu_tanh(gate_b[...]) * hs_ref[...]).astype(BF16)
        out_ref[...] = hin_b[...] + _dot(y, wout_ref[...])

    _run_pipeline(s, nsteps, (hin0_ref, gate0_ref, xp0_ref), (hin1_ref, gate1_ref, xp1_ref), step)


def _ffn_kernel(h_ref, gain_ref, wup_hbm, convw_ref, convb_ref, wdown_hbm, fgain_ref, out_ref,
                halo_ref, slab_ref, hin0_ref, hin1_ref, hb0_ref, hb1_ref, wup_ref, wdown_ref,
                stage_up_ref, stage_down_ref, sem_ref,
                *, layer, nsteps, d_ff, col_chunk, batch_major_in, final):
    s = pl.program_id(0)
    halo = (FFN_CONV - 1) * SUBLANES
    rows = hin0_ref.shape[0]

    @pl.when(s == 0)
    def _():
        halo_ref[...] = jnp.zeros(halo_ref.shape, F32)
        _stage_bf16(wup_hbm.at[layer], wup_ref, stage_up_ref, sem_ref)
        _stage_bf16(wdown_hbm.at[layer], wdown_ref, stage_down_ref, sem_ref)

    def stage_a(hin_a, hb_a):
        if batch_major_in:
            steps = h_ref.shape[1]
            nslab = h_ref.shape[2] // LANES
            for i in range(nslab):
                for b in range(SUBLANES):
                    slab_ref[i, pl.ds(b, steps, stride=SUBLANES), :] = h_ref[b, :, i * LANES:(i + 1) * LANES]
            for i in range(nslab):
                hin_a[:, i * LANES:(i + 1) * LANES] = slab_ref[i]
        else:
            hin_a[...] = h_ref[...]
        for r in range(rows // ROW_CHUNK):
            rs = slice(r * ROW_CHUNK, (r + 1) * ROW_CHUNK)
            hb_a[rs, :] = (_rms(hin_a[rs, :]) * gain_ref[...]).astype(BF16)

    def stage_b(hin_b, hb_b):
        def conv(cur, c0):
            prev = halo_ref[:, c0:c0 + col_chunk]
            cw = convw_ref[:, c0:c0 + col_chunk]
            out = cur * cw[FFN_CONV - 1:FFN_CONV, :] + convb_ref[:, c0:c0 + col_chunk]
            for j in range(FFN_CONV - 1):
                back = (FFN_CONV - 1 - j) * SUBLANES
                shifted = jnp.concatenate([prev[halo - back:, :], cur[:rows - back, :]], axis=0)
                out = out + shifted * cw[j:j + 1, :]
            halo_ref[:, c0:c0 + col_chunk] = cur[rows - halo:, :]
            return out

        def up(j):
            g0 = j * col_chunk
            v0 = d_ff + j * col_chunk
            return _dot(hb_b[...], wup_ref[:, g0:g0 + col_chunk]), _dot(hb_b[...], wup_ref[:, v0:v0 + col_chunk])

        nchunk = d_ff // col_chunk
        acc = jnp.zeros((rows, hin_b.shape[1]), F32)
        queue = [up(j) for j in range(min(UP_AHEAD, nchunk))]
        for j in range(nchunk):
            if j + UP_AHEAD < nchunk:
                queue.append(up(j + UP_AHEAD))
            cur = queue.pop(0)
            g0 = j * col_chunk
            gate = conv(cur[0], g0)
            val = conv(cur[1], d_ff + g0)
            act = (_silu(gate) * val).astype(BF16)
            acc = acc + _dot(act, wdown_ref[g0:g0 + col_chunk, :])
        res = hin_b[...] + acc
        if not final:
            out_ref[...] = res
        else:
            fin = _rms(res) * fgain_ref[...]
            steps = rows // SUBLANES
            for i in range(fin.shape[1] // LANES):
                slab_ref[i] = fin[:, i * LANES:(i + 1) * LANES]
            for i in range(fin.shape[1] // LANES):
                for b in range(SUBLANES):
                    out_ref[b, :, i * LANES:(i + 1) * LANES] = slab_ref[i, pl.ds(b, steps, stride=SUBLANES), :]

    def step(set_a, set_b, do_a, do_b):
        if do_a:
            stage_a(*set_a)
        if do_b:
            stage_b(*set_b)

    _run_pipeline(s, nsteps, (hin0_ref, hb0_ref), (hin1_ref, hb1_ref), step)


def _const_spec(shape):
    zeros = (0,) * len(shape)
    return pl.BlockSpec(shape, lambda *_: zeros)


def _params(sem, flags=None):
    return pltpu.CompilerParams(dimension_semantics=sem, vmem_limit_bytes=VMEM_LIMIT, flags=flags)


def _rope_tables(t):
    half = HEAD_DIM // 2
    inv_freq = ROPE_BASE ** (-jnp.arange(half, dtype=F32) / half)
    ang = jnp.arange(t, dtype=F32)[:, None] * inv_freq[None, :]
    cos = jnp.cos(ang)
    sin = jnp.sin(ang)
    return jnp.concatenate([cos, cos], axis=-1), jnp.concatenate([-sin, sin], axis=-1)


def _layer0_inproj(x, gain, w_in_all, conv_w, a_log, dt_bias, tile):
    b, t, d = x.shape
    width = HEADS * HEAD_DIM
    wsmall = jnp.zeros((d, LANES), F32).at[:, :2 * HEADS].set(w_in_all[0, :, 8 * width:]).astype(BF16)
    alog = jnp.zeros((1, LANES), F32).at[0, HEADS:2 * HEADS].set(a_log)
    dtb = jnp.zeros((1, LANES), F32).at[0, HEADS:2 * HEADS].set(dt_bias)
    cos, sin = _rope_tables(t)
    tiles_per_seq = t // tile
    ntiles = b * tiles_per_seq

    def in_tile(s):
        s = jnp.minimum(s, ntiles - 1)
        return s // tiles_per_seq, s % tiles_per_seq

    def out_tile(s):
        s = jnp.maximum(s - 1, 0)
        return s // tiles_per_seq, s % tiles_per_seq

    out_map = lambda s: (*out_tile(s), 0)
    pbuf = pltpu.VMEM((SUBLANES + tile, 8 * width + LANES), F32)
    return pl.pallas_call(
        functools.partial(_inproj_kernel, tiles_per_seq=tiles_per_seq, nsteps=ntiles + 1),
        grid=(ntiles + 1,),
        in_specs=[
            pl.BlockSpec((1, tile, d), lambda s: (*in_tile(s), 0)),
            _const_spec((1, d)),
            pl.BlockSpec(memory_space=pl.ANY),
            _const_spec((d, LANES)),
            pl.BlockSpec((tile, HEAD_DIM), lambda s: (out_tile(s)[1], 0)),
            pl.BlockSpec((tile, HEAD_DIM), lambda s: (out_tile(s)[1], 0)),
            _const_spec((GDN_CONV, 3 * width)),
            _const_spec((1, LANES)),
            _const_spec((1, LANES)),
        ],
        out_specs=[
            pl.BlockSpec((1, tile, 4 * width), out_map),
            pl.BlockSpec((1, tile, 4 * width), out_map),
            pl.BlockSpec((1, tile, LANES), out_map),
        ],
        out_shape=[
            jax.ShapeDtypeStruct((b, t, 4 * width), BF16),
            jax.ShapeDtypeStruct((b, t, 4 * width), BF16),
            jax.ShapeDtypeStruct((b, t, LANES), F32),
        ],
        scratch_shapes=[pbuf, pbuf, pltpu.VMEM((tile, d), BF16),
                        pltpu.VMEM((d, 8 * width), BF16),
                        _stage_scratch(d, 8 * width),
                        pltpu.SemaphoreType.DMA((2,))],
        compiler_params=_params(("arbitrary",)),
        name="l0_inproj",
    )(x, gain.reshape(1, d), w_in_all, wsmall, cos, sin, conv_w, alog, dtb)


def _layer0_mixer(ret, gdn, small, x, w_out_all, out_gain, tile):
    b, t, d = x.shape
    width = HEADS * HEAD_DIM
    return pl.pallas_call(
        _mixer_kernel,
        grid=(b, t // tile),
        in_specs=[
            pl.BlockSpec((1, tile, 4 * width), lambda i, j: (i, j, 0)),
            pl.BlockSpec((1, tile, 4 * width), lambda i, j: (i, j, 0)),
            pl.BlockSpec((1, tile, LANES), lambda i, j: (i, j, 0)),
            pl.BlockSpec((1, tile, d), lambda i, j: (i, j, 0)),
            pl.BlockSpec(memory_space=pl.ANY),
            _const_spec((1, HEAD_DIM)),
        ],
        out_specs=pl.BlockSpec((1, tile, d), lambda i, j: (i, j, 0)),
        out_shape=jax.ShapeDtypeStruct((b, t, d), F32),
        scratch_shapes=[
            pltpu.VMEM((HEADS, HEAD_DIM, HEAD_DIM), F32),
            pltpu.VMEM((HEADS, HEAD_DIM, HEAD_DIM), F32),
            pltpu.VMEM((tile, 2 * width), BF16),
            pltpu.VMEM((2 * width, d), BF16),
            _stage_scratch(2 * width, d),
            pltpu.SemaphoreType.DMA((2,)),
        ],
        compiler_params=_params(("arbitrary", "arbitrary")),
        name="l0_mixer",
    )(ret, gdn, small, x, w_out_all, out_gain.reshape(1, HEAD_DIM))


def _rglru(h_tm, gain, w_in_all, conv_w, conv_b, w_a_all, b_a, w_x_all, b_x, lam, w_out_all, steps):
    n, d = h_tm.shape
    d_rnn = lam.shape[0]
    rows = steps * SUBLANES
    ntiles = n // rows
    nblock = d_rnn // LRU_BLOCK
    hist = (LRU_CONV - 1) * SUBLANES
    any_spec = pl.BlockSpec(memory_space=pl.ANY)
    return pl.pallas_call(
        functools.partial(_rglru_kernel, nsteps=ntiles + 1),
        grid=(ntiles + 1,),
        in_specs=[
            pl.BlockSpec((rows, d), lambda s: (jnp.minimum(s, ntiles - 1), 0)),
            _const_spec((1, d)),
            any_spec,
            _const_spec((LRU_CONV, d_rnn)),
            _const_spec((1, d_rnn)),
            any_spec,
            any_spec,
            _const_spec((1, d_rnn)),
            _const_spec((1, d_rnn)),
            _const_spec((1, d_rnn)),
            any_spec,
        ],
        out_specs=pl.BlockSpec((rows, d), lambda s: (jnp.maximum(s - 1, 0), 0)),
        out_shape=jax.ShapeDtypeStruct((n, d), F32),
        scratch_shapes=[
            pltpu.VMEM((SUBLANES, d_rnn), F32),
            pltpu.VMEM((rows, d_rnn), F32),
            pltpu.VMEM((rows, d_rnn), F32),
            pltpu.VMEM((rows, d_rnn), F32),
            pltpu.VMEM((rows, d_rnn), F32),
            pltpu.VMEM((rows, d), BF16),
            pltpu.VMEM((rows, d), F32), pltpu.VMEM((rows, d), F32),
            pltpu.VMEM((rows, d_rnn), F32), pltpu.VMEM((rows, d_rnn), F32),
            pltpu.VMEM((hist + rows, d_rnn), F32), pltpu.VMEM((hist + rows, d_rnn), F32),
            pltpu.VMEM((d, 2 * d_rnn), BF16),
            pltpu.VMEM((d_rnn, d), BF16),
            pltpu.VMEM((nblock, LRU_BLOCK, 2 * LRU_BLOCK), BF16),
            _stage_scratch(d, 2 * d_rnn),
            _stage_scratch(d_rnn, d),
            pltpu.VMEM((2, nblock, LRU_BLOCK, LRU_BLOCK), F32),
            pltpu.SemaphoreType.DMA((2,)),
        ],
        compiler_params=_params(("arbitrary",)),
        name="l1_rglru",
    )(h_tm, gain.reshape(1, d), w_in_all, conv_w, conv_b.reshape(1, d_rnn), w_a_all, w_x_all,
      b_a.reshape(1, d_rnn), b_x.reshape(1, d_rnn), lam.reshape(1, d_rnn), w_out_all)


def _ffn(h, gain, w_up_all, conv_w, conv_b, w_down_all, final_gain, layer, steps, batch_major_in, final):
    d = h.shape[-1]
    n = h.size // d
    d_ff = w_down_all.shape[1]
    rows = steps * SUBLANES
    col_chunk = 2 * LANES
    ntiles = n // rows
    kern = functools.partial(_ffn_kernel, layer=layer, nsteps=ntiles + 1, d_ff=d_ff, col_chunk=col_chunk,
                             batch_major_in=batch_major_in, final=final)
    scratch = [pltpu.VMEM(((FFN_CONV - 1) * SUBLANES, 2 * d_ff), F32),
               pltpu.VMEM((d // LANES, rows, LANES), F32),
               pltpu.VMEM((rows, d), F32), pltpu.VMEM((rows, d), F32),
               pltpu.VMEM((rows, d), BF16), pltpu.VMEM((rows, d), BF16),
               pltpu.VMEM((d, 2 * d_ff), BF16),
               pltpu.VMEM((d_ff, d), BF16),
               _stage_scratch(d, 2 * d_ff),
               _stage_scratch(d_ff, d),
               pltpu.SemaphoreType.DMA((2,))]
    in_tile = lambda s: jnp.minimum(s, ntiles - 1)
    out_tile = lambda s: jnp.maximum(s - 1, 0)
    bm_spec = lambda tile_of: pl.BlockSpec((SUBLANES, steps, d), lambda s: (0, tile_of(s), 0))
    tm_spec = lambda tile_of: pl.BlockSpec((rows, d), lambda s: (tile_of(s), 0))
    if final:
        out_spec = bm_spec(out_tile)
        out_shape = jax.ShapeDtypeStruct((SUBLANES, n // SUBLANES, d), F32)
    else:
        out_spec = tm_spec(out_tile)
        out_shape = jax.ShapeDtypeStruct((n, d), F32)
    return pl.pallas_call(
        kern,
        grid=(ntiles + 1,),
        in_specs=[
            bm_spec(in_tile) if batch_major_in else tm_spec(in_tile),
            _const_spec((1, d)),
            pl.BlockSpec(memory_space=pl.ANY),
            _const_spec((FFN_CONV, 2 * d_ff)),
            _const_spec((1, 2 * d_ff)),
            pl.BlockSpec(memory_space=pl.ANY),
            _const_spec((1, d)),
        ],
        out_specs=out_spec,
        out_shape=out_shape,
        scratch_shapes=scratch,
        compiler_params=_params(("arbitrary",)),
        name="ffn_final" if final else "ffn",
    )(h, gain.reshape(1, d), w_up_all, conv_w, conv_b.reshape(1, 2 * d_ff),
      w_down_all, final_gain.reshape(1, d))


def kernel(x, norm_mix, norm_ffn, ret_gdn_w_in, gdn_conv_w, gdn_a_log, gdn_dt_bias, gdn_out_gain,
           ret_gdn_w_out, lru_w_in, lru_conv_w, lru_conv_b, lru_w_a, lru_b_a, lru_w_x, lru_b_x,
           lru_lambda, lru_w_out, ffn_w_up, ffn_conv_w, ffn_conv_b, ffn_w_down, norm_final):
    b, t, d = x.shape
    assert b == SUBLANES and norm_mix.shape[0] == 2
    tile_in = 512
    tile_mix = 512
    steps = 32
    steps_lru = 64

    ret, gdn, small = _layer0_inproj(x, norm_mix[0], ret_gdn_w_in, gdn_conv_w[0], gdn_a_log[0],
                                     gdn_dt_bias[0], tile_in)
    h = _layer0_mixer(ret, gdn, small, x, ret_gdn_w_out, gdn_out_gain[0], tile_mix)
    h = _ffn(h, norm_ffn[0], ffn_w_up, ffn_conv_w[0], ffn_conv_b[0], ffn_w_down, norm_final,
             0, steps, batch_major_in=True, final=False)
    h = _rglru(h, norm_mix[1], lru_w_in, lru_conv_w[0], lru_conv_b[0], lru_w_a, lru_b_a[0],
               lru_w_x, lru_b_x[0], lru_lambda[0], lru_w_out, steps_lru)
    return _ffn(h, norm_ffn[1], ffn_w_up, ffn_conv_w[1], ffn_conv_b[1], ffn_w_down, norm_final,
                1, steps, batch_major_in=False, final=True)
```

```python
import functools
import math

import jax
import jax.numpy as jnp
from jax import lax
from jax.experimental import pallas as pl
from jax.experimental.pallas import tpu as pltpu

F32 = jnp.float32
BF16 = jnp.bfloat16

EPS = 1e-6
ROPE_BASE = 10000.0
HEADS = 4
HEAD_DIM = 128
CHUNK = 128
UP_AHEAD = 2
STAGE_BYTES = 3 * 1024 * 1024
ROW_CHUNK = 64
INV_BASE = 16
GDN_CONV = 4
LRU_CONV = 4
FFN_CONV = 3
LRU_C = 8.0
LRU_BLOCK = 128
SUBLANES = 8
LANES = 128
NEG_BIG = -1e30
VMEM_LIMIT = 56 * 1024 * 1024

RET_LOG_GAMMA = tuple(math.log1p(-2.0 ** (-5.0 - h)) for h in range(HEADS))


def _rms(x):
    return x * lax.rsqrt(jnp.mean(x * x, axis=-1, keepdims=True) + EPS)


def _sigmoid(x):
    return 1.0 / (1.0 + jnp.exp(-x))


def _silu(x):
    return x * _sigmoid(x)


def _softplus(x):
    return jnp.maximum(x, 0.0) + jnp.log1p(jnp.exp(-jnp.abs(x)))


def _gelu_tanh(x):
    c = 2.0 * math.sqrt(2.0 / math.pi)
    return x * _sigmoid(x * (c + (0.044715 * c) * (x * x)))


def _dot(a, b):
    return jnp.dot(a, b, preferred_element_type=F32)


def _dot_nt(a, b):
    return lax.dot_general(a, b, (((1,), (1,)), ((), ())), preferred_element_type=F32)


def _mm(a, b):
    return _dot(a.astype(BF16), b.astype(BF16))


def _split3(x):
    hi = x.astype(BF16)
    r1 = x - hi.astype(F32)
    mid = r1.astype(BF16)
    lo = (r1 - mid.astype(F32)).astype(BF16)
    return hi, mid, lo


def _run_pipeline(s, nsteps, set0, set1, step):
    last = nsteps - 1
    parity = lax.rem(s, 2)

    @pl.when(s == 0)
    def _():
        step(set0, set1, True, False)

    @pl.when((s > 0) & (s < last) & (parity == 0))
    def _():
        step(set0, set1, True, True)

    @pl.when((s < last) & (parity == 1))
    def _():
        step(set1, set0, True, True)

    @pl.when(s == last)
    def _():
        if last % 2 == 0:
            step(set0, set1, False, True)
        else:
            step(set1, set0, False, True)


def _stage_rows(rows, cols):
    fits = [r for r in range(SUBLANES, rows + 1, SUBLANES) if rows % r == 0 and r * cols * 4 <= STAGE_BYTES]
    return fits[-1]


def _stage_scratch(rows, cols):
    return pltpu.VMEM((2, _stage_rows(rows, cols), cols), F32)


def _stage_bf16(src, dst_ref, stage_ref, sem_ref):
    rows, cols = src.shape
    row_chunk = stage_ref.shape[1]
    nchunk = rows // row_chunk

    def copy(i):
        return pltpu.make_async_copy(src.at[pl.ds(i * row_chunk, row_chunk), :], stage_ref.at[i % 2],
                                     sem_ref.at[i % 2])

    copy(0).start()
    for i in range(nchunk):
        if i + 1 < nchunk:
            copy(i + 1).start()
        copy(i).wait()
        dst_ref[pl.ds(i * row_chunk, row_chunk), :] = stage_ref[i % 2].astype(BF16)


def _stage_gate_blocks(wa, wx, wax_ref, stage_ref, sem_ref):
    copies = [pltpu.make_async_copy(w, stage_ref.at[i], sem_ref.at[i]) for i, w in enumerate((wa, wx))]
    for c in copies:
        c.start()
    for c in copies:
        c.wait()
    for n in range(wax_ref.shape[0]):
        wax_ref[n, :, 0:LRU_BLOCK] = stage_ref[0, n].astype(BF16)
        wax_ref[n, :, LRU_BLOCK:2 * LRU_BLOCK] = stage_ref[1, n].astype(BF16)


def _inproj_kernel(x_ref, gain_ref, win_hbm, wsmall_ref, cos_ref, sin_ref, convw_ref,
                   alog_ref, dtb_ref, ret_ref, gdn_ref, small_ref, pbuf0_ref, pbuf1_ref, hb_ref,
                   wmain_ref, stage_ref, sem_ref, *, tiles_per_seq, nsteps):
    s = pl.program_id(0)
    tile = x_ref.shape[1]
    width = HEADS * HEAD_DIM
    nmain = 8 * width
    body = slice(SUBLANES, SUBLANES + tile)
    qkv = slice(4 * width, 7 * width)

    @pl.when(s == 0)
    def _():
        pbuf0_ref[0:SUBLANES, qkv] = jnp.zeros((SUBLANES, 3 * width), F32)
        _stage_bf16(win_hbm.at[0, :, pl.ds(0, 8 * HEADS * HEAD_DIM)], wmain_ref, stage_ref, sem_ref)

    def stage_a(pa_ref):
        for r in range(tile // ROW_CHUNK):
            rows = slice(r * ROW_CHUNK, (r + 1) * ROW_CHUNK)
            hb_ref[rows, :] = (_rms(x_ref[0, rows, :]) * gain_ref[...]).astype(BF16)
        for c0, c1 in ((0, 2 * width), (2 * width, 4 * width), (4 * width, 7 * width), (7 * width, nmain)):
            pa_ref[body, c0:c1] = _dot(hb_ref[...], wmain_ref[:, c0:c1])
        pa_ref[body, nmain:nmain + LANES] = _dot(hb_ref[...], wsmall_ref[...])

    def stage_b(pa_ref, pb_ref):
        first_of_seq = lax.rem(jnp.maximum(s - 1, 0), tiles_per_seq) == 0
        pb_ref[0:SUBLANES, qkv] = jnp.where(first_of_seq, 0.0, pb_ref[0:SUBLANES, qkv])
        for r in range(tile // ROW_CHUNK):
            out_rows = slice(r * ROW_CHUNK, (r + 1) * ROW_CHUNK)
            rows = slice(SUBLANES + r * ROW_CHUNK, SUBLANES + (r + 1) * ROW_CHUNK)
            cos = cos_ref[out_rows, :]
            sin = sin_ref[out_rows, :]
            for part in range(2):
                for h in range(HEADS):
                    c0 = part * width + h * HEAD_DIM
                    ph = pb_ref[rows, c0:c0 + HEAD_DIM]
                    rot = ph * cos + pltpu.roll(ph, HEAD_DIM // 2, 1) * sin
                    if part == 1:
                        rot = rot * (HEAD_DIM ** -0.5)
                    ret_ref[0, out_rows, c0:c0 + HEAD_DIM] = rot.astype(BF16)
            for c0 in range(2 * width, 4 * width, LANES):
                ret_ref[0, out_rows, c0:c0 + LANES] = pb_ref[rows, c0:c0 + LANES].astype(BF16)
            for c0 in range(0, width, LANES):
                gdn_ref[0, out_rows, 3 * width + c0:3 * width + c0 + LANES] = (
                    pb_ref[rows, 7 * width + c0:7 * width + c0 + LANES].astype(BF16))
            for part in range(3):
                for h in range(HEADS):
                    c0 = part * width + h * HEAD_DIM
                    src_cols = slice(4 * width + c0, 4 * width + c0 + HEAD_DIM)
                    cw = convw_ref[:, c0:c0 + HEAD_DIM]
                    acc = pb_ref[rows, src_cols] * cw[GDN_CONV - 1:GDN_CONV, :]
                    for j in range(GDN_CONV - 1):
                        back = GDN_CONV - 1 - j
                        shifted = pb_ref[rows.start - back:rows.stop - back, src_cols]
                        acc = acc + shifted * cw[j:j + 1, :]
                    act = _silu(acc)
                    if part < 2:
                        act = act * lax.rsqrt(jnp.sum(act * act, axis=-1, keepdims=True) + EPS)
                    if part == 0:
                        act = act * (HEAD_DIM ** -0.5)
                    gdn_ref[0, out_rows, c0:c0 + HEAD_DIM] = act.astype(BF16)
            sm = pb_ref[rows, nmain:nmain + LANES]
            col = lax.broadcasted_iota(jnp.int32, sm.shape, 1)
            beta = _sigmoid(sm)
            g = -jnp.exp(alog_ref[...]) * _softplus(sm + dtb_ref[...])
            small_ref[0, out_rows, :] = jnp.where(col < HEADS, beta, g)
        pa_ref[0:SUBLANES, qkv] = pb_ref[tile:tile + SUBLANES, qkv]

    def step(set_a, set_b, do_a, do_b):
        if do_a:
            stage_a(*set_a)
        if do_b:
            stage_b(*set_a, *set_b)

    _run_pipeline(s, nsteps, (pbuf0_ref,), (pbuf1_ref,), step)


def _mixer_kernel(ret_ref, gdn_ref, small_ref, x_ref, wout_hbm, ogain_ref, out_ref,
                  sret_ref, sgdn_ref, ybuf_ref, wout_ref, stage_ref, sem_ref):
    ti = pl.program_id(1)

    @pl.when((pl.program_id(0) == 0) & (ti == 0))
    def _():
        _stage_bf16(wout_hbm.at[0], wout_ref, stage_ref, sem_ref)

    tile = x_ref.shape[1]
    width = HEADS * HEAD_DIM
    c = CHUNK

    @pl.when(ti == 0)
    def _():
        sret_ref[...] = jnp.zeros(sret_ref.shape, F32)
        sgdn_ref[...] = jnp.zeros(sgdn_ref.shape, F32)

    row = lax.broadcasted_iota(jnp.int32, (c, c), 0)
    col = lax.broadcasted_iota(jnp.int32, (c, c), 1)
    tril = row >= col
    strict = row > col
    rowf = row.astype(F32)
    relf = (row - col).astype(F32)
    eye = jnp.where(row == col, 1.0, 0.0).astype(F32)
    tri_b = jnp.where(tril, 1.0, 0.0).astype(BF16)
    ogain = ogain_ref[...]
    nblk = tile // c
    probs = [(blk, h) for blk in range(nblk) for h in range(HEADS)]

    def cols(ref, part, blk, h):
        c0 = part * width + h * HEAD_DIM
        return ref[0, blk * c:(blk + 1) * c, c0:c0 + HEAD_DIM]


    sms, gcums, gcum_ts = [], [], []
    for blk in range(nblk):
        sm = small_ref[0, blk * c:(blk + 1) * c, :]
        parts = _dot(tri_b, jnp.concatenate(_split3(sm), axis=1))
        gcum = parts[:, :LANES] + parts[:, LANES:2 * LANES] + parts[:, 2 * LANES:]
        sms.append(sm)
        gcums.append(gcum)
        gcum_ts.append(gcum.T)
    ret_qk = {}
    gdn_kq = {}
    for p in probs:
        blk, h = p
        ret_qk[p] = _dot_nt(cols(ret_ref, 0, blk, h), cols(ret_ref, 1, blk, h))
        k = cols(gdn_ref, 1, blk, h)
        gdn_kq[p] = _dot_nt(jnp.concatenate([k, cols(gdn_ref, 0, blk, h)], axis=0), k)

    ret_iv = {}
    for p in probs:
        blk, h = p
        lg = RET_LOG_GAMMA[h]
        dmask = jnp.where(tril, jnp.exp(lg * jnp.where(tril, relf, 0.0)), 0.0)
        scores = (ret_qk[p] * dmask).astype(BF16)
        k_tail_t = (cols(ret_ref, 1, blk, h).astype(F32) * jnp.exp(lg * (c - 1.0 - rowf))).T.astype(BF16)
        ret_iv[p] = _dot(jnp.concatenate([scores, k_tail_t], axis=0), cols(ret_ref, 2, blk, h))

    lmats, attns, rhs, kts, egcs, glasts = {}, {}, {}, {}, {}, {}
    for p in probs:
        blk, h = p
        gcum, sm = gcums[blk], sms[blk]
        gc_col = jnp.broadcast_to(gcum[:, HEADS + h:HEADS + h + 1], (c, c))
        gc_row = jnp.broadcast_to(gcum_ts[blk][HEADS + h:HEADS + h + 1, :], (c, c))
        g_last = jnp.broadcast_to(gcum[c - 1:c, HEADS + h:HEADS + h + 1], (c, c))
        beta_col = jnp.broadcast_to(sm[:, h:h + 1], (c, c))
        decay = jnp.exp(jnp.where(tril, gc_col - gc_row, NEG_BIG))
        kf = cols(gdn_ref, 1, blk, h).astype(F32)
        egc = jnp.exp(gc_col)
        lmats[p] = jnp.where(strict, gdn_kq[p][:c] * beta_col * decay, 0.0)
        attns[p] = jnp.where(tril, gdn_kq[p][c:] * decay, 0.0).astype(BF16)
        rhs[p] = jnp.concatenate([(cols(gdn_ref, 2, blk, h).astype(F32) * beta_col).astype(BF16),
                                  (kf * (beta_col * egc)).astype(BF16)], axis=1)
        kts[p] = (kf * jnp.exp(g_last - gc_col)).T.astype(BF16)
        egcs[p] = egc
        glasts[p] = g_last

    def same_block(size):
        return (row // size) == (col // size)

    diag = same_block(INV_BASE)
    tinv, power = {}, {}
    for p in probs:
        ld = jnp.where(diag, lmats[p], 0.0)
        tinv[p] = eye - ld
        power[p] = _mm(ld, ld)
    factors = int(math.log2(INV_BASE)) - 1
    for f in range(factors):
        last = f == factors - 1
        for p in probs:
            pw = power[p].astype(BF16)
            if last:
                tinv[p] = tinv[p] + _dot(tinv[p].astype(BF16), pw)
            else:
                both = _dot(jnp.concatenate([tinv[p].astype(BF16), pw], axis=0), pw)
                tinv[p] = tinv[p] + both[:c]
                power[p] = both[c:]
    size = INV_BASE
    while size < c:
        off = same_block(2 * size) & jnp.logical_not(same_block(size))
        ct = {p: _dot(jnp.where(off, lmats[p], 0.0).astype(BF16), tinv[p].astype(BF16)) for p in probs}
        for p in probs:
            tinv[p] = tinv[p] - _dot(tinv[p].astype(BF16), ct[p].astype(BF16))
        size *= 2

    ret_inter = {}
    for h in range(HEADS):
        lg = RET_LOG_GAMMA[h]
        state = sret_ref[h]
        for blk in range(nblk):
            ret_inter[(blk, h)] = _dot(cols(ret_ref, 0, blk, h), state.astype(BF16))
            state = state * math.exp(lg * c) + ret_iv[(blk, h)][c:]
        sret_ref[h] = state

    uw = {p: _dot(tinv[p].astype(BF16), rhs[p]).astype(BF16) for p in probs}
    awp = {p: _dot(jnp.concatenate([attns[p], kts[p]], axis=0), uw[p][:, HEAD_DIM:]) for p in probs}

    for p in probs:
        blk, h = p
        lg = RET_LOG_GAMMA[h]
        o = ret_iv[p][:c] + ret_inter[p] * jnp.exp(lg * (rowf + 1.0))
        y = _rms(o) * _silu(cols(ret_ref, 3, blk, h).astype(F32))
        ybuf_ref[blk * c:(blk + 1) * c, h * HEAD_DIM:(h + 1) * HEAD_DIM] = y.astype(BF16)

    states = [sgdn_ref[h] for h in range(HEADS)]
    for blk in range(nblk):
        for h in range(HEADS):
            p = (blk, h)
            q_eff = (cols(gdn_ref, 0, blk, h).astype(F32) * egcs[p] - awp[p][:c]).astype(BF16)
            lhs = jnp.concatenate([jnp.concatenate([q_eff, attns[p]], axis=1),
                                   jnp.concatenate([(-awp[p][c:]).astype(BF16), kts[p]], axis=1)], axis=0)
            res = _dot(lhs, jnp.concatenate([states[h].astype(BF16), uw[p][:, :HEAD_DIM]], axis=0))
            o = res[:c]
            states[h] = jnp.exp(glasts[p]) * states[h] + res[c:]
            y = _rms(o) * ogain * _silu(cols(gdn_ref, 3, blk, h).astype(F32))
            ybuf_ref[blk * c:(blk + 1) * c, width + h * HEAD_DIM:width + (h + 1) * HEAD_DIM] = y.astype(BF16)
    for h in range(HEADS):
        sgdn_ref[h] = states[h]

    out_ref[0] = x_ref[0] + _dot(ybuf_ref[...], wout_ref[...])


def _rglru_kernel(h_ref, gain_ref, win_hbm, convw_ref, convb_ref, wa_hbm, wx_hbm, ba_ref, bx_ref,
                  lam_ref, wout_hbm, out_ref, state_ref, a_ref, u_ref, hs_ref, xr_ref, hb_ref,
                  hin0_ref, hin1_ref, gate0_ref, gate1_ref, xp0_ref, xp1_ref,
                  win_ref, wout_ref, wax_ref, stage_in_ref, stage_out_ref, stage_blk_ref, sem_ref, *, nsteps):
    s = pl.program_id(0)
    rows = h_ref.shape[0]
    d_rnn = lam_ref.shape[1]
    steps = rows // SUBLANES
    hist = (LRU_CONV - 1) * SUBLANES
    nblock = d_rnn // LRU_BLOCK

    @pl.when(s == 0)
    def _():
        state_ref[...] = jnp.zeros(state_ref.shape, F32)
        xp0_ref[0:hist, :] = jnp.zeros((hist, d_rnn), F32)
        _stage_bf16(win_hbm.at[0], win_ref, stage_in_ref, sem_ref)
        _stage_bf16(wout_hbm.at[0], wout_ref, stage_out_ref, sem_ref)
        _stage_gate_blocks(wa_hbm.at[0], wx_hbm.at[0], wax_ref, stage_blk_ref, sem_ref)

    def step(set_a, set_b, do_a, do_b):
        hin_a, gate_a, xp_a = set_a
        hin_b, gate_b, xp_b = set_b
        if do_a:
            hin_a[...] = h_ref[...]
            for r in range(rows // ROW_CHUNK):
                rs = slice(r * ROW_CHUNK, (r + 1) * ROW_CHUNK)
                hb_ref[rs, :] = (_rms(h_ref[rs, :]) * gain_ref[...]).astype(BF16)

        ris = []
        slab = 2 * d_rnn // nblock
        for n in range(nblock):
            c0 = n * LRU_BLOCK
            if do_b:
                cw = convw_ref[:, c0:c0 + LRU_BLOCK]
                for r in range(rows // ROW_CHUNK):
                    lo = hist + r * ROW_CHUNK
                    xr = (xp_b[lo:lo + ROW_CHUNK, c0:c0 + LRU_BLOCK] * cw[LRU_CONV - 1:LRU_CONV, :]
                          + convb_ref[:, c0:c0 + LRU_BLOCK])
                    for j in range(LRU_CONV - 1):
                        back = (LRU_CONV - 1 - j) * SUBLANES
                        xr = xr + xp_b[lo - back:lo - back + ROW_CHUNK, c0:c0 + LRU_BLOCK] * cw[j:j + 1, :]
                    xr_ref[r * ROW_CHUNK:(r + 1) * ROW_CHUNK, c0:c0 + LRU_BLOCK] = xr
                ris.append(_dot(xr_ref[:, c0:c0 + LRU_BLOCK].astype(BF16), wax_ref[n]))
            if do_a:
                p0 = n * slab
                proj = _dot(hb_ref[...], win_ref[:, p0:p0 + slab])
                if p0 < d_rnn:
                    gate_a[:, p0:p0 + slab] = proj
                else:
                    xp_a[hist:hist + rows, p0 - d_rnn:p0 - d_rnn + slab] = proj
        if not do_b:
            return
        xp_a[0:hist, :] = xp_b[rows:rows + hist, :]

        sp = _softplus(-lam_ref[...]) * (-LRU_C)
        for n in range(nblock):
            c0 = n * LRU_BLOCK
            ri = ris[n]
            r = _sigmoid(ri[:, :LRU_BLOCK] + ba_ref[:, c0:c0 + LRU_BLOCK])
            i = _sigmoid(ri[:, LRU_BLOCK:] + bx_ref[:, c0:c0 + LRU_BLOCK])
            a = jnp.exp(r * sp[:, c0:c0 + LRU_BLOCK])
            a_ref[:, c0:c0 + LRU_BLOCK] = a
            x = 1.0 - a * a
            root = jnp.where(x > 0.0, x * lax.rsqrt(x), 0.0)
            u_ref[:, c0:c0 + LRU_BLOCK] = root * (i * xr_ref[:, c0:c0 + LRU_BLOCK])

        state = state_ref[...]
        for t in range(steps):
            sl = slice(t * SUBLANES, (t + 1) * SUBLANES)
            state = a_ref[sl, :] * state + u_ref[sl, :]
            hs_ref[sl, :] = state
        state_ref[...] = state

        y = (_gelu_tanh(gate_b[...]) * hs_ref[...]).astype(BF16)
        out_ref[...] = hin_b[...] + _dot(y, wout_ref[...])

    _run_pipeline(s, nsteps, (hin0_ref, gate0_ref, xp0_ref), (hin1_ref, gate1_ref, xp1_ref), step)


def _ffn_kernel(h_ref, gain_ref, wup_hbm, convw_ref, convb_ref, wdown_hbm, fgain_ref, out_ref,
                halo_ref, slab_ref, hin0_ref, hin1_ref, hb0_ref, hb1_ref, wup_ref, wdown_ref,
                stage_up_ref, stage_down_ref, sem_ref,
                *, layer, nsteps, d_ff, col_chunk, batch_major_in, final):
    s = pl.program_id(0)
    halo = (FFN_CONV - 1) * SUBLANES
    rows = hin0_ref.shape[0]

    @pl.when(s == 0)
    def _():
        halo_ref[...] = jnp.zeros(halo_ref.shape, F32)
        _stage_bf16(wup_hbm.at[layer], wup_ref, stage_up_ref, sem_ref)
        _stage_bf16(wdown_hbm.at[layer], wdown_ref, stage_down_ref, sem_ref)

    def stage_a(hin_a, hb_a):
        if batch_major_in:
            steps = h_ref.shape[1]
            nslab = h_ref.shape[2] // LANES
            for i in range(nslab):
                for b in range(SUBLANES):
                    slab_ref[i, pl.ds(b, steps, stride=SUBLANES), :] = h_ref[b, :, i * LANES:(i + 1) * LANES]
            for i in range(nslab):
                hin_a[:, i * LANES:(i + 1) * LANES] = slab_ref[i]
        else:
            hin_a[...] = h_ref[...]
        for r in range(rows // ROW_CHUNK):
            rs = slice(r * ROW_CHUNK, (r + 1) * ROW_CHUNK)
            hb_a[rs, :] = (_rms(hin_a[rs, :]) * gain_ref[...]).astype(BF16)

    def stage_b(hin_b, hb_b):
        def conv(cur, c0):
            prev = halo_ref[:, c0:c0 + col_chunk]
            cw = convw_ref[:, c0:c0 + col_chunk]
            out = cur * cw[FFN_CONV - 1:FFN_CONV, :] + convb_ref[:, c0:c0 + col_chunk]
            for j in range(FFN_CONV - 1):
                back = (FFN_CONV - 1 - j) * SUBLANES
                shifted = jnp.concatenate([prev[halo - back:, :], cur[:rows - back, :]], axis=0)
                out = out + shifted * cw[j:j + 1, :]
            halo_ref[:, c0:c0 + col_chunk] = cur[rows - halo:, :]
            return out

        def up(j):
            g0 = j * col_chunk
            v0 = d_ff + j * col_chunk
            return _dot(hb_b[...], wup_ref[:, g0:g0 + col_chunk]), _dot(hb_b[...], wup_ref[:, v0:v0 + col_chunk])

        nchunk = d_ff // col_chunk
        acc = jnp.zeros((rows, hin_b.shape[1]), F32)
        queue = [up(j) for j in range(min(UP_AHEAD, nchunk))]
        for j in range(nchunk):
            if j + UP_AHEAD < nchunk:
                queue.append(up(j + UP_AHEAD))
            cur = queue.pop(0)
            g0 = j * col_chunk
            gate = conv(cur[0], g0)
            val = conv(cur[1], d_ff + g0)
            act = (_silu(gate) * val).astype(BF16)
            acc = acc + _dot(act, wdown_ref[g0:g0 + col_chunk, :])
        res = hin_b[...] + acc
        if not final:
            out_ref[...] = res
        else:
            fin = _rms(res) * fgain_ref[...]
            steps = rows // SUBLANES
            for i in range(fin.shape[1] // LANES):
                slab_ref[i] = fin[:, i * LANES:(i + 1) * LANES]
            for i in range(fin.shape[1] // LANES):
                for b in range(SUBLANES):
                    out_ref[b, :, i * LANES:(i + 1) * LANES] = slab_ref[i, pl.ds(b, steps, stride=SUBLANES), :]

    def step(set_a, set_b, do_a, do_b):
        if do_a:
            stage_a(*set_a)
        if do_b:
            stage_b(*set_b)

    _run_pipeline(s, nsteps, (hin0_ref, hb0_ref), (hin1_ref, hb1_ref), step)


def _const_spec(shape):
    zeros = (0,) * len(shape)
    return pl.BlockSpec(shape, lambda *_: zeros)


def _params(sem, flags=None):
    return pltpu.CompilerParams(dimension_semantics=sem, vmem_limit_bytes=VMEM_LIMIT, flags=flags)


def _rope_tables(t):
    half = HEAD_DIM // 2
    inv_freq = ROPE_BASE ** (-jnp.arange(half, dtype=F32) / half)
    ang = jnp.arange(t, dtype=F32)[:, None] * inv_freq[None, :]
    cos = jnp.cos(ang)
    sin = jnp.sin(ang)
    return jnp.concatenate([cos, cos], axis=-1), jnp.concatenate([-sin, sin], axis=-1)


def _layer0_inproj(x, gain, w_in_all, conv_w, a_log, dt_bias, tile):
    b, t, d = x.shape
    width = HEADS * HEAD_DIM
    wsmall = jnp.zeros((d, LANES), F32).at[:, :2 * HEADS].set(w_in_all[0, :, 8 * width:]).astype(BF16)
    alog = jnp.zeros((1, LANES), F32).at[0, HEADS:2 * HEADS].set(a_log)
    dtb = jnp.zeros((1, LANES), F32).at[0, HEADS:2 * HEADS].set(dt_bias)
    cos, sin = _rope_tables(t)
    tiles_per_seq = t // tile
    ntiles = b * tiles_per_seq

    def in_tile(s):
        s = jnp.minimum(s, ntiles - 1)
        return s // tiles_per_seq, s % tiles_per_seq

    def out_tile(s):
        s = jnp.maximum(s - 1, 0)
        return s // tiles_per_seq, s % tiles_per_seq

    out_map = lambda s: (*out_tile(s), 0)
    pbuf = pltpu.VMEM((SUBLANES + tile, 8 * width + LANES), F32)
    return pl.pallas_call(
        functools.partial(_inproj_kernel, tiles_per_seq=tiles_per_seq, nsteps=ntiles + 1),
        grid=(ntiles + 1,),
        in_specs=[
            pl.BlockSpec((1, tile, d), lambda s: (*in_tile(s), 0)),
            _const_spec((1, d)),
            pl.BlockSpec(memory_space=pl.ANY),
            _const_spec((d, LANES)),
            pl.BlockSpec((tile, HEAD_DIM), lambda s: (out_tile(s)[1], 0)),
            pl.BlockSpec((tile, HEAD_DIM), lambda s: (out_tile(s)[1], 0)),
            _const_spec((GDN_CONV, 3 * width)),
            _const_spec((1, LANES)),
            _const_spec((1, LANES)),
        ],
        out_specs=[
            pl.BlockSpec((1, tile, 4 * width), out_map),
            pl.BlockSpec((1, tile, 4 * width), out_map),
            pl.BlockSpec((1, tile, LANES), out_map),
        ],
        out_shape=[
            jax.ShapeDtypeStruct((b, t, 4 * width), BF16),
            jax.ShapeDtypeStruct((b, t, 4 * width), BF16),
            jax.ShapeDtypeStruct((b, t, LANES), F32),
        ],
        scratch_shapes=[pbuf, pbuf, pltpu.VMEM((tile, d), BF16),
                        pltpu.VMEM((d, 8 * width), BF16),
                        _stage_scratch(d, 8 * width),
                        pltpu.SemaphoreType.DMA((2,))],
        compiler_params=_params(("arbitrary",)),
        name="l0_inproj",
    )(x, gain.reshape(1, d), w_in_all, wsmall, cos, sin, conv_w, alog, dtb)


def _layer0_mixer(ret, gdn, small, x, w_out_all, out_gain, tile):
    b, t, d = x.shape
    width = HEADS * HEAD_DIM
    return pl.pallas_call(
        _mixer_kernel,
        grid=(b, t // tile),
        in_specs=[
            pl.BlockSpec((1, tile, 4 * width), lambda i, j: (i, j, 0)),
            pl.BlockSpec((1, tile, 4 * width), lambda i, j: (i, j, 0)),
            pl.BlockSpec((1, tile, LANES), lambda i, j: (i, j, 0)),
            pl.BlockSpec((1, tile, d), lambda i, j: (i, j, 0)),
            pl.BlockSpec(memory_space=pl.ANY),
            _const_spec((1, HEAD_DIM)),
        ],
        out_specs=pl.BlockSpec((1, tile, d), lambda i, j: (i, j, 0)),
        out_shape=jax.ShapeDtypeStruct((b, t, d), F32),
        scratch_shapes=[
            pltpu.VMEM((HEADS, HEAD_DIM, HEAD_DIM), F32),
            pltpu.VMEM((HEADS, HEAD_DIM, HEAD_DIM), F32),
            pltpu.VMEM((tile, 2 * width), BF16),
            pltpu.VMEM((2 * width, d), BF16),
            _stage_scratch(2 * width, d),
            pltpu.SemaphoreType.DMA((2,)),
        ],
        compiler_params=_params(("arbitrary", "arbitrary")),
        name="l0_mixer",
    )(ret, gdn, small, x, w_out_all, out_gain.reshape(1, HEAD_DIM))


def _rglru(h_tm, gain, w_in_all, conv_w, conv_b, w_a_all, b_a, w_x_all, b_x, lam, w_out_all, steps):
    n, d = h_tm.shape
    d_rnn = lam.shape[0]
    rows = steps * SUBLANES
    ntiles = n // rows
    nblock = d_rnn // LRU_BLOCK
    hist = (LRU_CONV - 1) * SUBLANES
    any_spec = pl.BlockSpec(memory_space=pl.ANY)
    return pl.pallas_call(
        functools.partial(_rglru_kernel, nsteps=ntiles + 1),
        grid=(ntiles + 1,),
        in_specs=[
            pl.BlockSpec((rows, d), lambda s: (jnp.minimum(s, ntiles - 1), 0)),
            _const_spec((1, d)),
            any_spec,
            _const_spec((LRU_CONV, d_rnn)),
            _const_spec((1, d_rnn)),
            any_spec,
            any_spec,
            _const_spec((1, d_rnn)),
            _const_spec((1, d_rnn)),
            _const_spec((1, d_rnn)),
            any_spec,
        ],
        out_specs=pl.BlockSpec((rows, d), lambda s: (jnp.maximum(s - 1, 0), 0)),
        out_shape=jax.ShapeDtypeStruct((n, d), F32),
        scratch_shapes=[
            pltpu.VMEM((SUBLANES, d_rnn), F32),
            pltpu.VMEM((rows, d_rnn), F32),
            pltpu.VMEM((rows, d_rnn), F32),
            pltpu.VMEM((rows, d_rnn), F32),
            pltpu.VMEM((rows, d_rnn), F32),
            pltpu.VMEM((rows, d), BF16),
            pltpu.VMEM((rows, d), F32), pltpu.VMEM((rows, d), F32),
            pltpu.VMEM((rows, d_rnn), F32), pltpu.VMEM((rows, d_rnn), F32),
            pltpu.VMEM((hist + rows, d_rnn), F32), pltpu.VMEM((hist + rows, d_rnn), F32),
            pltpu.VMEM((d, 2 * d_rnn), BF16),
            pltpu.VMEM((d_rnn, d), BF16),
            pltpu.VMEM((nblock, LRU_BLOCK, 2 * LRU_BLOCK), BF16),
            _stage_scratch(d, 2 * d_rnn),
            _stage_scratch(d_rnn, d),
            pltpu.VMEM((2, nblock, LRU_BLOCK, LRU_BLOCK), F32),
            pltpu.SemaphoreType.DMA((2,)),
        ],
        compiler_params=_params(("arbitrary",)),
        name="l1_rglru",
    )(h_tm, gain.reshape(1, d), w_in_all, conv_w, conv_b.reshape(1, d_rnn), w_a_all, w_x_all,
      b_a.reshape(1, d_rnn), b_x.reshape(1, d_rnn), lam.reshape(1, d_rnn), w_out_all)


def _ffn(h, gain, w_up_all, conv_w, conv_b, w_down_all, final_gain, layer, steps, batch_major_in, final):
    d = h.shape[-1]
    n = h.size // d
    d_ff = w_down_all.shape[1]
    rows = steps * SUBLANES
    col_chunk = 2 * LANES
    ntiles = n // rows
    kern = functools.partial(_ffn_kernel, layer=layer, nsteps=ntiles + 1, d_ff=d_ff, col_chunk=col_chunk,
                             batch_major_in=batch_major_in, final=final)
    scratch = [pltpu.VMEM(((FFN_CONV - 1) * SUBLANES, 2 * d_ff), F32),
               pltpu.VMEM((d // LANES, rows, LANES), F32),
               pltpu.VMEM((rows, d), F32), pltpu.VMEM((rows, d), F32),
               pltpu.VMEM((rows, d), BF16), pltpu.VMEM((rows, d), BF16),
               pltpu.VMEM((d, 2 * d_ff), BF16),
               pltpu.VMEM((d_ff, d), BF16),
               _stage_scratch(d, 2 * d_ff),
               _stage_scratch(d_ff, d),
               pltpu.SemaphoreType.DMA((2,))]
    in_tile = lambda s: jnp.minimum(s, ntiles - 1)
    out_tile = lambda s: jnp.maximum(s - 1, 0)
    bm_spec = lambda tile_of: pl.BlockSpec((SUBLANES, steps, d), lambda s: (0, tile_of(s), 0))
    tm_spec = lambda tile_of: pl.BlockSpec((rows, d), lambda s: (tile_of(s), 0))
    if final:
        out_spec = bm_spec(out_tile)
        out_shape = jax.ShapeDtypeStruct((SUBLANES, n // SUBLANES, d), F32)
    else:
        out_spec = tm_spec(out_tile)
        out_shape = jax.ShapeDtypeStruct((n, d), F32)
    return pl.pallas_call(
        kern,
        grid=(ntiles + 1,),
        in_specs=[
            bm_spec(in_tile) if batch_major_in else tm_spec(in_tile),
            _const_spec((1, d)),
            pl.BlockSpec(memory_space=pl.ANY),
            _const_spec((FFN_CONV, 2 * d_ff)),
            _const_spec((1, 2 * d_ff)),
            pl.BlockSpec(memory_space=pl.ANY),
            _const_spec((1, d)),
        ],
        out_specs=out_spec,
        out_shape=out_shape,
        scratch_shapes=scratch,
        compiler_params=_params(("arbitrary",)),
        name="ffn_final" if final else "ffn",
    )(h, gain.reshape(1, d), w_up_all, conv_w, conv_b.reshape(1, 2 * d_ff),
      w_down_all, final_gain.reshape(1, d))


def kernel(x, norm_mix, norm_ffn, ret_gdn_w_in, gdn_conv_w, gdn_a_log, gdn_dt_bias, gdn_out_gain,
           ret_gdn_w_out, lru_w_in, lru_conv_w, lru_conv_b, lru_w_a, lru_b_a, lru_w_x, lru_b_x,
           lru_lambda, lru_w_out, ffn_w_up, ffn_conv_w, ffn_conv_b, ffn_w_down, norm_final):
    b, t, d = x.shape
    assert b == SUBLANES and norm_mix.shape[0] == 2
    tile_in = 512
    tile_mix = 512
    steps = 32
    steps_lru = 64

    ret, gdn, small = _layer0_inproj(x, norm_mix[0], ret_gdn_w_in, gdn_conv_w[0], gdn_a_log[0],
                                     gdn_dt_bias[0], tile_in)
    h = _layer0_mixer(ret, gdn, small, x, ret_gdn_w_out, gdn_out_gain[0], tile_mix)
    h = _ffn(h, norm_ffn[0], ffn_w_up, ffn_conv_w[0], ffn_conv_b[0], ffn_w_down, norm_final,
             0, steps, batch_major_in=True, final=False)
    h = _rglru(h, norm_mix[1], lru_w_in, lru_conv_w[0], lru_conv_b[0], lru_w_a, lru_b_a[0],
               lru_w_x, lru_b_x[0], lru_lambda[0], lru_w_out, steps_lru)
    return _ffn(h, norm_ffn[1], ffn_w_up, ffn_conv_w[1], ffn_conv_b[1], ffn_w_down, norm_final,
                1, steps, batch_major_in=False, final=True)
```

```python
import functools
import math

import jax
import jax.numpy as jnp
from jax import lax
from jax.experimental import pallas as pl
from jax.experimental.pallas import tpu as pltpu

F32 = jnp.float32
BF16 = jnp.bfloat16

EPS = 1e-6
ROPE_BASE = 10000.0
HEADS = 4
HEAD_DIM = 128
CHUNK = 128
UP_AHEAD = 2
STAGE_BYTES = 3 * 1024 * 1024
ROW_CHUNK = 64
INV_BASE = 16
GDN_CONV = 4
LRU_CONV = 4
FFN_CONV = 3
LRU_C = 8.0
LRU_BLOCK = 128
SUBLANES = 8
LANES = 128
NEG_BIG = -1e30
VMEM_LIMIT = 56 * 1024 * 1024

RET_LOG_GAMMA = tuple(math.log1p(-2.0 ** (-5.0 - h)) for h in range(HEADS))


def _rms(x):
    return x * lax.rsqrt(jnp.mean(x * x, axis=-1, keepdims=True) + EPS)


def _sigmoid(x):
    return 1.0 / (1.0 + jnp.exp(-x))


def _silu(x):
    return x * _sigmoid(x)


def _softplus(x):
    return jnp.maximum(x, 0.0) + jnp.log1p(jnp.exp(-jnp.abs(x)))


def _gelu_tanh(x):
    c = 2.0 * math.sqrt(2.0 / math.pi)
    return x * _sigmoid(x * (c + (0.044715 * c) * (x * x)))


def _dot(a, b):
    return jnp.dot(a, b, preferred_element_type=F32)


def _dot_nt(a, b):
    return lax.dot_general(a, b, (((1,), (1,)), ((), ())), preferred_element_type=F32)


def _mm(a, b):
    return _dot(a.astype(BF16), b.astype(BF16))


def _split3(x):
    hi = x.astype(BF16)
    r1 = x - hi.astype(F32)
    mid = r1.astype(BF16)
    lo = (r1 - mid.astype(F32)).astype(BF16)
    return hi, mid, lo


def _run_pipeline(s, nsteps, set0, set1, step):
    last = nsteps - 1
    parity = lax.rem(s, 2)

    @pl.when(s == 0)
    def _():
        step(set0, set1, True, False)

    @pl.when((s > 0) & (s < last) & (parity == 0))
    def _():
        step(set0, set1, True, True)

    @pl.when((s < last) & (parity == 1))
    def _():
        step(set1, set0, True, True)

    @pl.when(s == last)
    def _():
        if last % 2 == 0:
            step(set0, set1, False, True)
        else:
            step(set1, set0, False, True)


def _stage_rows(rows, cols):
    fits = [r for r in range(SUBLANES, rows + 1, SUBLANES) if rows % r == 0 and r * cols * 4 <= STAGE_BYTES]
    return fits[-1]


def _stage_scratch(rows, cols):
    return pltpu.VMEM((2, _stage_rows(rows, cols), cols), F32)


def _stage_bf16(src, dst_ref, stage_ref, sem_ref):
    rows, cols = src.shape
    row_chunk = stage_ref.shape[1]
    nchunk = rows // row_chunk

    def copy(i):
        return pltpu.make_async_copy(src.at[pl.ds(i * row_chunk, row_chunk), :], stage_ref.at[i % 2],
                                     sem_ref.at[i % 2])

    copy(0).start()
    for i in range(nchunk):
        if i + 1 < nchunk:
            copy(i + 1).start()
        copy(i).wait()
        dst_ref[pl.ds(i * row_chunk, row_chunk), :] = stage_ref[i % 2].astype(BF16)


def _stage_gate_blocks(wa, wx, wax_ref, stage_ref, sem_ref):
    copies = [pltpu.make_async_copy(w, stage_ref.at[i], sem_ref.at[i]) for i, w in enumerate((wa, wx))]
    for c in copies:
        c.start()
    for c in copies:
        c.wait()
    for n in range(wax_ref.shape[0]):
        wax_ref[n, :, 0:LRU_BLOCK] = stage_ref[0, n].astype(BF16)
        wax_ref[n, :, LRU_BLOCK:2 * LRU_BLOCK] = stage_ref[1, n].astype(BF16)


def _inproj_kernel(x_ref, gain_ref, win_hbm, wsmall_ref, cos_ref, sin_ref, convw_ref,
                   alog_ref, dtb_ref, ret_ref, gdn_ref, small_ref, pbuf0_ref, pbuf1_ref, hb_ref,
                   wmain_ref, stage_ref, sem_ref, *, tiles_per_seq, nsteps):
    s = pl.program_id(0)
    tile = x_ref.shape[1]
    width = HEADS * HEAD_DIM
    nmain = 8 * width
    body = slice(SUBLANES, SUBLANES + tile)
    qkv = slice(4 * width, 7 * width)

    @pl.when(s == 0)
    def _():
        pbuf0_ref[0:SUBLANES, qkv] = jnp.zeros((SUBLANES, 3 * width), F32)
        _stage_bf16(win_hbm.at[0, :, pl.ds(0, 8 * HEADS * HEAD_DIM)], wmain_ref, stage_ref, sem_ref)

    def stage_a(pa_ref):
        for r in range(tile // ROW_CHUNK):
            rows = slice(r * ROW_CHUNK, (r + 1) * ROW_CHUNK)
            hb_ref[rows, :] = (_rms(x_ref[0, rows, :]) * gain_ref[...]).astype(BF16)
        for c0, c1 in ((0, 2 * width), (2 * width, 4 * width), (4 * width, 7 * width), (7 * width, nmain)):
            pa_ref[body, c0:c1] = _dot(hb_ref[...], wmain_ref[:, c0:c1])
        pa_ref[body, nmain:nmain + LANES] = _dot(hb_ref[...], wsmall_ref[...])

    def stage_b(pa_ref, pb_ref):
        first_of_seq = lax.rem(jnp.maximum(s - 1, 0), tiles_per_seq) == 0
        pb_ref[0:SUBLANES, qkv] = jnp.where(first_of_seq, 0.0, pb_ref[0:SUBLANES, qkv])
        for r in range(tile // ROW_CHUNK):
            out_rows = slice(r * ROW_CHUNK, (r + 1) * ROW_CHUNK)
            rows = slice(SUBLANES + r * ROW_CHUNK, SUBLANES + (r + 1) * ROW_CHUNK)
            cos = cos_ref[out_rows, :]
            sin = sin_ref[out_rows, :]
            for part in range(2):
                for h in range(HEADS):
                    c0 = part * width + h * HEAD_DIM
                    ph = pb_ref[rows, c0:c0 + HEAD_DIM]
                    rot = ph * cos + pltpu.roll(ph, HEAD_DIM // 2, 1) * sin
                    if part == 1:
                        rot = rot * (HEAD_DIM ** -0.5)
                    ret_ref[0, out_rows, c0:c0 + HEAD_DIM] = rot.astype(BF16)
            for c0 in range(2 * width, 4 * width, LANES):
                ret_ref[0, out_rows, c0:c0 + LANES] = pb_ref[rows, c0:c0 + LANES].astype(BF16)
            for c0 in range(0, width, LANES):
                gdn_ref[0, out_rows, 3 * width + c0:3 * width + c0 + LANES] = (
                    pb_ref[rows, 7 * width + c0:7 * width + c0 + LANES].astype(BF16))
            for part in range(3):
                for h in range(HEADS):
                    c0 = part * width + h * HEAD_DIM
                    src_cols = slice(4 * width + c0, 4 * width + c0 + HEAD_DIM)
                    cw = convw_ref[:, c0:c0 + HEAD_DIM]
                    acc = pb_ref[rows, src_cols] * cw[GDN_CONV - 1:GDN_CONV, :]
                    for j in range(GDN_CONV - 1):
                        back = GDN_CONV - 1 - j
                        shifted = pb_ref[rows.start - back:rows.stop - back, src_cols]
                        acc = acc + shifted * cw[j:j + 1, :]
                    act = _silu(acc)
                    if part < 2:
                        act = act * lax.rsqrt(jnp.sum(act * act, axis=-1, keepdims=True) + EPS)
                    if part == 0:
                        act = act * (HEAD_DIM ** -0.5)
                    gdn_ref[0, out_rows, c0:c0 + HEAD_DIM] = act.astype(BF16)
            sm = pb_ref[rows, nmain:nmain + LANES]
            col = lax.broadcasted_iota(jnp.int32, sm.shape, 1)
            beta = _sigmoid(sm)
            g = -jnp.exp(alog_ref[...]) * _softplus(sm + dtb_ref[...])
            small_ref[0, out_rows, :] = jnp.where(col < HEADS, beta, g)
        pa_ref[0:SUBLANES, qkv] = pb_ref[tile:tile + SUBLANES, qkv]

    def step(set_a, set_b, do_a, do_b):
        if do_a:
            stage_a(*set_a)
        if do_b:
            stage_b(*set_a, *set_b)

    _run_pipeline(s, nsteps, (pbuf0_ref,), (pbuf1_ref,), step)


def _mixer_kernel(ret_ref, gdn_ref, small_ref, x_ref, wout_hbm, ogain_ref, out_ref,
                  sret_ref, sgdn_ref, ybuf_ref, wout_ref, stage_ref, sem_ref):
    ti = pl.program_id(1)

    @pl.when((pl.program_id(0) == 0) & (ti == 0))
    def _():
        _stage_bf16(wout_hbm.at[0], wout_ref, stage_ref, sem_ref)

    tile = x_ref.shape[1]
    width = HEADS * HEAD_DIM
    c = CHUNK

    @pl.when(ti == 0)
    def _():
        sret_ref[...] = jnp.zeros(sret_ref.shape, F32)
        sgdn_ref[...] = jnp.zeros(sgdn_ref.shape, F32)

    row = lax.broadcasted_iota(jnp.int32, (c, c), 0)
    col = lax.broadcasted_iota(jnp.int32, (c, c), 1)
    tril = row >= col
    strict = row > col
    rowf = row.astype(F32)
    relf = (row - col).astype(F32)
    eye = jnp.where(row == col, 1.0, 0.0).astype(F32)
    tri_b = jnp.where(tril, 1.0, 0.0).astype(BF16)
    ogain = ogain_ref[...]
    nblk = tile // c
    probs = [(blk, h) for blk in range(nblk) for h in range(HEADS)]

    def cols(ref, part, blk, h):
        c0 = part * width + h * HEAD_DIM
        return ref[0, blk * c:(blk + 1) * c, c0:c0 + HEAD_DIM]


    sms, gcums, gcum_ts = [], [], []
    for blk in range(nblk):
        sm = small_ref[0, blk * c:(blk + 1) * c, :]
        parts = _dot(tri_b, jnp.concatenate(_split3(sm), axis=1))
        gcum = parts[:, :LANES] + parts[:, LANES:2 * LANES] + parts[:, 2 * LANES:]
        sms.append(sm)
        gcums.append(gcum)
        gcum_ts.append(gcum.T)
    ret_qk = {}
    gdn_kq = {}
    for p in probs:
        blk, h = p
        ret_qk[p] = _dot_nt(cols(ret_ref, 0, blk, h), cols(ret_ref, 1, blk, h))
        k = cols(gdn_ref, 1, blk, h)
        gdn_kq[p] = _dot_nt(jnp.concatenate([k, cols(gdn_ref, 0, blk, h)], axis=0), k)

    ret_iv = {}
    for p in probs:
        blk, h = p
        lg = RET_LOG_GAMMA[h]
        dmask = jnp.where(tril, jnp.exp(lg * jnp.where(tril, relf, 0.0)), 0.0)
        scores = (ret_qk[p] * dmask).astype(BF16)
        k_tail_t = (cols(ret_ref, 1, blk, h).astype(F32) * jnp.exp(lg * (c - 1.0 - rowf))).T.astype(BF16)
        ret_iv[p] = _dot(jnp.concatenate([scores, k_tail_t], axis=0), cols(ret_ref, 2, blk, h))

    lmats, attns, rhs, kts, egcs, glasts = {}, {}, {}, {}, {}, {}
    for p in probs:
        blk, h = p
        gcum, sm = gcums[blk], sms[blk]
        gc_col = jnp.broadcast_to(gcum[:, HEADS + h:HEADS + h + 1], (c, c))
        gc_row = jnp.broadcast_to(gcum_ts[blk][HEADS + h:HEADS + h + 1, :], (c, c))
        g_last = jnp.broadcast_to(gcum[c - 1:c, HEADS + h:HEADS + h + 1], (c, c))
        beta_col = jnp.broadcast_to(sm[:, h:h + 1], (c, c))
        decay = jnp.exp(jnp.where(tril, gc_col - gc_row, NEG_BIG))
        kf = cols(gdn_ref, 1, blk, h).astype(F32)
        egc = jnp.exp(gc_col)
        lmats[p] = jnp.where(strict, gdn_kq[p][:c] * beta_col * decay, 0.0)
        attns[p] = jnp.where(tril, gdn_kq[p][c:] * decay, 0.0).astype(BF16)
        rhs[p] = jnp.concatenate([(cols(gdn_ref, 2, blk, h).astype(F32) * beta_col).astype(BF16),
                                  (kf * (beta_col * egc)).astype(BF16)], axis=1)
        kts[p] = (kf * jnp.exp(g_last - gc_col)).T.astype(BF16)
        egcs[p] = egc
        glasts[p] = g_last

    def same_block(size):
        return (row // size) == (col // size)

    diag = same_block(INV_BASE)
    tinv, power = {}, {}
    for p in probs:
        ld = jnp.where(diag, lmats[p], 0.0)
        tinv[p] = eye - ld
        power[p] = _mm(ld, ld)
    factors = int(math.log2(INV_BASE)) - 1
    for f in range(factors):
        last = f == factors - 1
        for p in probs:
            pw = power[p].astype(BF16)
            if last:
                tinv[p] = tinv[p] + _dot(tinv[p].astype(BF16), pw)
            else:
                both = _dot(jnp.concatenate([tinv[p].astype(BF16), pw], axis=0), pw)
                tinv[p] = tinv[p] + both[:c]
                power[p] = both[c:]
    size = INV_BASE
    while size < c:
        off = same_block(2 * size) & jnp.logical_not(same_block(size))
        ct = {p: _dot(jnp.where(off, lmats[p], 0.0).astype(BF16), tinv[p].astype(BF16)) for p in probs}
        for p in probs:
            tinv[p] = tinv[p] - _dot(tinv[p].astype(BF16), ct[p].astype(BF16))
        size *= 2

    ret_inter = {}
    for h in range(HEADS):
        lg = RET_LOG_GAMMA[h]
        state = sret_ref[h]
        for blk in range(nblk):
            ret_inter[(blk, h)] = _dot(cols(ret_ref, 0, blk, h), state.astype(BF16))
            state = state * math.exp(lg * c) + ret_iv[(blk, h)][c:]
        sret_ref[h] = state

    uw = {p: _dot(tinv[p].astype(BF16), rhs[p]).astype(BF16) for p in probs}
    apn = {}
    for p in probs:
        wu = jnp.concatenate([uw[p][:, HEAD_DIM:], uw[p][:, :HEAD_DIM]], axis=1)
        apn[p] = _dot(jnp.concatenate([attns[p], kts[p]], axis=0), wu)

    for p in probs:
        blk, h = p
        lg = RET_LOG_GAMMA[h]
        o = ret_iv[p][:c] + ret_inter[p] * jnp.exp(lg * (rowf + 1.0))
        y = _rms(o) * _silu(cols(ret_ref, 3, blk, h).astype(F32))
        ybuf_ref[blk * c:(blk + 1) * c, h * HEAD_DIM:(h + 1) * HEAD_DIM] = y.astype(BF16)

    states = [sgdn_ref[h] for h in range(HEADS)]
    for blk in range(nblk):
        for h in range(HEADS):
            p = (blk, h)
            q_eff = (cols(gdn_ref, 0, blk, h).astype(F32) * egcs[p] - apn[p][:c, :HEAD_DIM]).astype(BF16)
            pq = jnp.concatenate([apn[p][c:, :HEAD_DIM].astype(BF16), q_eff], axis=0)
            ps = _dot(pq, states[h].astype(BF16))
            o = ps[c:] + apn[p][:c, HEAD_DIM:]
            states[h] = jnp.exp(glasts[p]) * states[h] - ps[:c] + apn[p][c:, HEAD_DIM:]
            y = _rms(o) * ogain * _silu(cols(gdn_ref, 3, blk, h).astype(F32))
            ybuf_ref[blk * c:(blk + 1) * c, width + h * HEAD_DIM:width + (h + 1) * HEAD_DIM] = y.astype(BF16)
    for h in range(HEADS):
        sgdn_ref[h] = states[h]

    out_ref[0] = x_ref[0] + _dot(ybuf_ref[...], wout_ref[...])


def _rglru_kernel(h_ref, gain_ref, win_hbm, convw_ref, convb_ref, wa_hbm, wx_hbm, ba_ref, bx_ref,
                  lam_ref, wout_hbm, out_ref, state_ref, a_ref, u_ref, hs_ref, xr_ref, hb_ref,
                  hin0_ref, hin1_ref, gate0_ref, gate1_ref, xp0_ref, xp1_ref,
                  win_ref, wout_ref, wax_ref, stage_in_ref, stage_out_ref, stage_blk_ref, sem_ref, *, nsteps):
    s = pl.program_id(0)
    rows = h_ref.shape[0]
    d_rnn = lam_ref.shape[1]
    steps = rows // SUBLANES
    hist = (LRU_CONV - 1) * SUBLANES
    nblock = d_rnn // LRU_BLOCK

    @pl.when(s == 0)
    def _():
        state_ref[...] = jnp.zeros(state_ref.shape, F32)
        xp0_ref[0:hist, :] = jnp.zeros((hist, d_rnn), F32)
        _stage_bf16(win_hbm.at[0], win_ref, stage_in_ref, sem_ref)
        _stage_bf16(wout_hbm.at[0], wout_ref, stage_out_ref, sem_ref)
        _stage_gate_blocks(wa_hbm.at[0], wx_hbm.at[0], wax_ref, stage_blk_ref, sem_ref)

    def step(set_a, set_b, do_a, do_b):
        hin_a, gate_a, xp_a = set_a
        hin_b, gate_b, xp_b = set_b
        if do_a:
            hin_a[...] = h_ref[...]
            for r in range(rows // ROW_CHUNK):
                rs = slice(r * ROW_CHUNK, (r + 1) * ROW_CHUNK)
                hb_ref[rs, :] = (_rms(h_ref[rs, :]) * gain_ref[...]).astype(BF16)

        ris = []
        slab = 2 * d_rnn // nblock
        for n in range(nblock):
            c0 = n * LRU_BLOCK
            if do_b:
                cw = convw_ref[:, c0:c0 + LRU_BLOCK]
                for r in range(rows // ROW_CHUNK):
                    lo = hist + r * ROW_CHUNK
                    xr = (xp_b[lo:lo + ROW_CHUNK, c0:c0 + LRU_BLOCK] * cw[LRU_CONV - 1:LRU_CONV, :]
                          + convb_ref[:, c0:c0 + LRU_BLOCK])
                    for j in range(LRU_CONV - 1):
                        back = (LRU_CONV - 1 - j) * SUBLANES
                        xr = xr + xp_b[lo - back:lo - back + ROW_CHUNK, c0:c0 + LRU_BLOCK] * cw[j:j + 1, :]
                    xr_ref[r * ROW_CHUNK:(r + 1) * ROW_CHUNK, c0:c0 + LRU_BLOCK] = xr
                ris.append(_dot(xr_ref[:, c0:c0 + LRU_BLOCK].astype(BF16), wax_ref[n]))
            if do_a:
                p0 = n * slab
                proj = _dot(hb_ref[...], win_ref[:, p0:p0 + slab])
                if p0 < d_rnn:
                    gate_a[:, p0:p0 + slab] = proj
                else:
                    xp_a[hist:hist + rows, p0 - d_rnn:p0 - d_rnn + slab] = proj
        if not do_b:
            return
        xp_a[0:hist, :] = xp_b[rows:rows + hist, :]

        sp = _softplus(-lam_ref[...]) * (-LRU_C)
        for n in range(nblock):
            c0 = n * LRU_BLOCK
            ri = ris[n]
            r = _sigmoid(ri[:, :LRU_BLOCK] + ba_ref[:, c0:c0 + LRU_BLOCK])
            i = _sigmoid(ri[:, LRU_BLOCK:] + bx_ref[:, c0:c0 + LRU_BLOCK])
            a = jnp.exp(r * sp[:, c0:c0 + LRU_BLOCK])
            a_ref[:, c0:c0 + LRU_BLOCK] = a
            x = 1.0 - a * a
            root = jnp.where(x > 0.0, x * lax.rsqrt(x), 0.0)
            u_ref[:, c0:c0 + LRU_BLOCK] = root * (i * xr_ref[:, c0:c0 + LRU_BLOCK])

        state = state_ref[...]
        for t in range(steps):
            sl = slice(t * SUBLANES, (t + 1) * SUBLANES)
            state = a_ref[sl, :] * state + u_ref[sl, :]
            hs_ref[sl, :] = state
        state_ref[...] = state

        y = (_gelu_tanh(gate_b[...]) * hs_ref[...]).astype(BF16)
        out_ref[...] = hin_b[...] + _dot(y, wout_ref[...])

    _run_pipeline(s, nsteps, (hin0_ref, gate0_ref, xp0_ref), (hin1_ref, gate1_ref, xp1_ref), step)


def _ffn_kernel(h_ref, gain_ref, wup_hbm, convw_ref, convb_ref, wdown_hbm, fgain_ref, out_ref,
                halo_ref, slab_ref, hin0_ref, hin1_ref, hb0_ref, hb1_ref, wup_ref, wdown_ref,
                stage_up_ref, stage_down_ref, sem_ref,
                *, layer, nsteps, d_ff, col_chunk, batch_major_in, final):
    s = pl.program_id(0)
    halo = (FFN_CONV - 1) * SUBLANES
    rows = hin0_ref.shape[0]

    @pl.when(s == 0)
    def _():
        halo_ref[...] = jnp.zeros(halo_ref.shape, F32)
        _stage_bf16(wup_hbm.at[layer], wup_ref, stage_up_ref, sem_ref)
        _stage_bf16(wdown_hbm.at[layer], wdown_ref, stage_down_ref, sem_ref)

    def stage_a(hin_a, hb_a):
        if batch_major_in:
            steps = h_ref.shape[1]
            nslab = h_ref.shape[2] // LANES
            for i in range(nslab):
                for b in range(SUBLANES):
                    slab_ref[i, pl.ds(b, steps, stride=SUBLANES), :] = h_ref[b, :, i * LANES:(i + 1) * LANES]
            for i in range(nslab):
                hin_a[:, i * LANES:(i + 1) * LANES] = slab_ref[i]
        else:
            hin_a[...] = h_ref[...]
        for r in range(rows // ROW_CHUNK):
            rs = slice(r * ROW_CHUNK, (r + 1) * ROW_CHUNK)
            hb_a[rs, :] = (_rms(hin_a[rs, :]) * gain_ref[...]).astype(BF16)

    def stage_b(hin_b, hb_b):
        def conv(cur, c0):
            prev = halo_ref[:, c0:c0 + col_chunk]
            cw = convw_ref[:, c0:c0 + col_chunk]
            out = cur * cw[FFN_CONV - 1:FFN_CONV, :] + convb_ref[:, c0:c0 + col_chunk]
            for j in range(FFN_CONV - 1):
                back = (FFN_CONV - 1 - j) * SUBLANES
                shifted = jnp.concatenate([prev[halo - back:, :], cur[:rows - back, :]], axis=0)
                out = out + shifted * cw[j:j + 1, :]
            halo_ref[:, c0:c0 + col_chunk] = cur[rows - halo:, :]
            return out

        def up(j):
            g0 = j * col_chunk
            v0 = d_ff + j * col_chunk
            return _dot(hb_b[...], wup_ref[:, g0:g0 + col_chunk]), _dot(hb_b[...], wup_ref[:, v0:v0 + col_chunk])

        nchunk = d_ff // col_chunk
        acc = jnp.zeros((rows, hin_b.shape[1]), F32)
        queue = [up(j) for j in range(min(UP_AHEAD, nchunk))]
        for j in range(nchunk):
            if j + UP_AHEAD < nchunk:
                queue.append(up(j + UP_AHEAD))
            cur = queue.pop(0)
            g0 = j * col_chunk
            gate = conv(cur[0], g0)
            val = conv(cur[1], d_ff + g0)
            act = (_silu(gate) * val).astype(BF16)
            acc = acc + _dot(act, wdown_ref[g0:g0 + col_chunk, :])
        res = hin_b[...] + acc
        if not final:
            out_ref[...] = res
        else:
            fin = _rms(res) * fgain_ref[...]
            steps = rows // SUBLANES
            for i in range(fin.shape[1] // LANES):
                slab_ref[i] = fin[:, i * LANES:(i + 1) * LANES]
            for i in range(fin.shape[1] // LANES):
                for b in range(SUBLANES):
                    out_ref[b, :, i * LANES:(i + 1) * LANES] = slab_ref[i, pl.ds(b, steps, stride=SUBLANES), :]

    def step(set_a, set_b, do_a, do_b):
        if do_a:
            stage_a(*set_a)
        if do_b:
            stage_b(*set_b)

    _run_pipeline(s, nsteps, (hin0_ref, hb0_ref), (hin1_ref, hb1_ref), step)


def _const_spec(shape):
    zeros = (0,) * len(shape)
    return pl.BlockSpec(shape, lambda *_: zeros)


def _params(sem, flags=None):
    return pltpu.CompilerParams(dimension_semantics=sem, vmem_limit_bytes=VMEM_LIMIT, flags=flags)


def _rope_tables(t):
    half = HEAD_DIM // 2
    inv_freq = ROPE_BASE ** (-jnp.arange(half, dtype=F32) / half)
    ang = jnp.arange(t, dtype=F32)[:, None] * inv_freq[None, :]
    cos = jnp.cos(ang)
    sin = jnp.sin(ang)
    return jnp.concatenate([cos, cos], axis=-1), jnp.concatenate([-sin, sin], axis=-1)


def _layer0_inproj(x, gain, w_in_all, conv_w, a_log, dt_bias, tile):
    b, t, d = x.shape
    width = HEADS * HEAD_DIM
    wsmall = jnp.zeros((d, LANES), F32).at[:, :2 * HEADS].set(w_in_all[0, :, 8 * width:]).astype(BF16)
    alog = jnp.zeros((1, LANES), F32).at[0, HEADS:2 * HEADS].set(a_log)
    dtb = jnp.zeros((1, LANES), F32).at[0, HEADS:2 * HEADS].set(dt_bias)
    cos, sin = _rope_tables(t)
    tiles_per_seq = t // tile
    ntiles = b * tiles_per_seq

    def in_tile(s):
        s = jnp.minimum(s, ntiles - 1)
        return s // tiles_per_seq, s % tiles_per_seq

    def out_tile(s):
        s = jnp.maximum(s - 1, 0)
        return s // tiles_per_seq, s % tiles_per_seq

    out_map = lambda s: (*out_tile(s), 0)
    pbuf = pltpu.VMEM((SUBLANES + tile, 8 * width + LANES), F32)
    return pl.pallas_call(
        functools.partial(_inproj_kernel, tiles_per_seq=tiles_per_seq, nsteps=ntiles + 1),
        grid=(ntiles + 1,),
        in_specs=[
            pl.BlockSpec((1, tile, d), lambda s: (*in_tile(s), 0)),
            _const_spec((1, d)),
            pl.BlockSpec(memory_space=pl.ANY),
            _const_spec((d, LANES)),
            pl.BlockSpec((tile, HEAD_DIM), lambda s: (out_tile(s)[1], 0)),
            pl.BlockSpec((tile, HEAD_DIM), lambda s: (out_tile(s)[1], 0)),
            _const_spec((GDN_CONV, 3 * width)),
            _const_spec((1, LANES)),
            _const_spec((1, LANES)),
        ],
        out_specs=[
            pl.BlockSpec((1, tile, 4 * width), out_map),
            pl.BlockSpec((1, tile, 4 * width), out_map),
            pl.BlockSpec((1, tile, LANES), out_map),
        ],
        out_shape=[
            jax.ShapeDtypeStruct((b, t, 4 * width), BF16),
            jax.ShapeDtypeStruct((b, t, 4 * width), BF16),
            jax.ShapeDtypeStruct((b, t, LANES), F32),
        ],
        scratch_shapes=[pbuf, pbuf, pltpu.VMEM((tile, d), BF16),
                        pltpu.VMEM((d, 8 * width), BF16),
                        _stage_scratch(d, 8 * width),
                        pltpu.SemaphoreType.DMA((2,))],
        compiler_params=_params(("arbitrary",)),
        name="l0_inproj",
    )(x, gain.reshape(1, d), w_in_all, wsmall, cos, sin, conv_w, alog, dtb)


def _layer0_mixer(ret, gdn, small, x, w_out_all, out_gain, tile):
    b, t, d = x.shape
    width = HEADS * HEAD_DIM
    return pl.pallas_call(
        _mixer_kernel,
        grid=(b, t // tile),
        in_specs=[
            pl.BlockSpec((1, tile, 4 * width), lambda i, j: (i, j, 0)),
            pl.BlockSpec((1, tile, 4 * width), lambda i, j: (i, j, 0)),
            pl.BlockSpec((1, tile, LANES), lambda i, j: (i, j, 0)),
            pl.BlockSpec((1, tile, d), lambda i, j: (i, j, 0)),
            pl.BlockSpec(memory_space=pl.ANY),
            _const_spec((1, HEAD_DIM)),
        ],
        out_specs=pl.BlockSpec((1, tile, d), lambda i, j: (i, j, 0)),
        out_shape=jax.ShapeDtypeStruct((b, t, d), F32),
        scratch_shapes=[
            pltpu.VMEM((HEADS, HEAD_DIM, HEAD_DIM), F32),
            pltpu.VMEM((HEADS, HEAD_DIM, HEAD_DIM), F32),
            pltpu.VMEM((tile, 2 * width), BF16),
            pltpu.VMEM((2 * width, d), BF16),
            _stage_scratch(2 * width, d),
            pltpu.SemaphoreType.DMA((2,)),
        ],
        compiler_params=_params(("arbitrary", "arbitrary")),
        name="l0_mixer",
    )(ret, gdn, small, x, w_out_all, out_gain.reshape(1, HEAD_DIM))


def _rglru(h_tm, gain, w_in_all, conv_w, conv_b, w_a_all, b_a, w_x_all, b_x, lam, w_out_all, steps):
    n, d = h_tm.shape
    d_rnn = lam.shape[0]
    rows = steps * SUBLANES
    ntiles = n // rows
    nblock = d_rnn // LRU_BLOCK
    hist = (LRU_CONV - 1) * SUBLANES
    any_spec = pl.BlockSpec(memory_space=pl.ANY)
    return pl.pallas_call(
        functools.partial(_rglru_kernel, nsteps=ntiles + 1),
        grid=(ntiles + 1,),
        in_specs=[
            pl.BlockSpec((rows, d), lambda s: (jnp.minimum(s, ntiles - 1), 0)),
            _const_spec((1, d)),
            any_spec,
            _const_spec((LRU_CONV, d_rnn)),
            _const_spec((1, d_rnn)),
            any_spec,
            any_spec,
            _const_spec((1, d_rnn)),
            _const_spec((1, d_rnn)),
            _const_spec((1, d_rnn)),
            any_spec,
        ],
        out_specs=pl.BlockSpec((rows, d), lambda s: (jnp.maximum(s - 1, 0), 0)),
        out_shape=jax.ShapeDtypeStruct((n, d), F32),
        scratch_shapes=[
            pltpu.VMEM((SUBLANES, d_rnn), F32),
            pltpu.VMEM((rows, d_rnn), F32),
            pltpu.VMEM((rows, d_rnn), F32),
            pltpu.VMEM((rows, d_rnn), F32),
            pltpu.VMEM((rows, d_rnn), F32),
            pltpu.VMEM((rows, d), BF16),
            pltpu.VMEM((rows, d), F32), pltpu.VMEM((rows, d), F32),
            pltpu.VMEM((rows, d_rnn), F32), pltpu.VMEM((rows, d_rnn), F32),
            pltpu.VMEM((hist + rows, d_rnn), F32), pltpu.VMEM((hist + rows, d_rnn), F32),
            pltpu.VMEM((d, 2 * d_rnn), BF16),
            pltpu.VMEM((d_rnn, d), BF16),
            pltpu.VMEM((nblock, LRU_BLOCK, 2 * LRU_BLOCK), BF16),
            _stage_scratch(d, 2 * d_rnn),
            _stage_scratch(d_rnn, d),
            pltpu.VMEM((2, nblock, LRU_BLOCK, LRU_BLOCK), F32),
            pltpu.SemaphoreType.DMA((2,)),
        ],
        compiler_params=_params(("arbitrary",)),
        name="l1_rglru",
    )(h_tm, gain.reshape(1, d), w_in_all, conv_w, conv_b.reshape(1, d_rnn), w_a_all, w_x_all,
      b_a.reshape(1, d_rnn), b_x.reshape(1, d_rnn), lam.reshape(1, d_rnn), w_out_all)


def _ffn(h, gain, w_up_all, conv_w, conv_b, w_down_all, final_gain, layer, steps, batch_major_in, final):
    d = h.shape[-1]
    n = h.size // d
    d_ff = w_down_all.shape[1]
    rows = steps * SUBLANES
    col_chunk = 2 * LANES
    ntiles = n // rows
    kern = functools.partial(_ffn_kernel, layer=layer, nsteps=ntiles + 1, d_ff=d_ff, col_chunk=col_chunk,
                             batch_major_in=batch_major_in, final=final)
    scratch = [pltpu.VMEM(((FFN_CONV - 1) * SUBLANES, 2 * d_ff), F32),
               pltpu.VMEM((d // LANES, rows, LANES), F32),
               pltpu.VMEM((rows, d), F32), pltpu.VMEM((rows, d), F32),
               pltpu.VMEM((rows, d), BF16), pltpu.VMEM((rows, d), BF16),
               pltpu.VMEM((d, 2 * d_ff), BF16),
               pltpu.VMEM((d_ff, d), BF16),
               _stage_scratch(d, 2 * d_ff),
               _stage_scratch(d_ff, d),
               pltpu.SemaphoreType.DMA((2,))]
    in_tile = lambda s: jnp.minimum(s, ntiles - 1)
    out_tile = lambda s: jnp.maximum(s - 1, 0)
    bm_spec = lambda tile_of: pl.BlockSpec((SUBLANES, steps, d), lambda s: (0, tile_of(s), 0))
    tm_spec = lambda tile_of: pl.BlockSpec((rows, d), lambda s: (tile_of(s), 0))
    if final:
        out_spec = bm_spec(out_tile)
        out_shape = jax.ShapeDtypeStruct((SUBLANES, n // SUBLANES, d), F32)
    else:
        out_spec = tm_spec(out_tile)
        out_shape = jax.ShapeDtypeStruct((n, d), F32)
    return pl.pallas_call(
        kern,
        grid=(ntiles + 1,),
        in_specs=[
            bm_spec(in_tile) if batch_major_in else tm_spec(in_tile),
            _const_spec((1, d)),
            pl.BlockSpec(memory_space=pl.ANY),
            _const_spec((FFN_CONV, 2 * d_ff)),
            _const_spec((1, 2 * d_ff)),
            pl.BlockSpec(memory_space=pl.ANY),
            _const_spec((1, d)),
        ],
        out_specs=out_spec,
        out_shape=out_shape,
        scratch_shapes=scratch,
        compiler_params=_params(("arbitrary",)),
        name="ffn_final" if final else "ffn",
    )(h, gain.reshape(1, d), w_up_all, conv_w, conv_b.reshape(1, 2 * d_ff),
      w_down_all, final_gain.reshape(1, d))


def kernel(x, norm_mix, norm_ffn, ret_gdn_w_in, gdn_conv_w, gdn_a_log, gdn_dt_bias, gdn_out_gain,
           ret_gdn_w_out, lru_w_in, lru_conv_w, lru_conv_b, lru_w_a, lru_b_a, lru_w_x, lru_b_x,
           lru_lambda, lru_w_out, ffn_w_up, ffn_conv_w, ffn_conv_b, ffn_w_down, norm_final):
    b, t, d = x.shape
    assert b == SUBLANES and norm_mix.shape[0] == 2
    tile_in = 512
    tile_mix = 512
    steps = 64
    steps_lru = 64

    ret, gdn, small = _layer0_inproj(x, norm_mix[0], ret_gdn_w_in, gdn_conv_w[0], gdn_a_log[0],
                                     gdn_dt_bias[0], tile_in)
    h = _layer0_mixer(ret, gdn, small, x, ret_gdn_w_out, gdn_out_gain[0], tile_mix)
    h = _ffn(h, norm_ffn[0], ffn_w_up, ffn_conv_w[0], ffn_conv_b[0], ffn_w_down, norm_final,
             0, steps, batch_major_in=True, final=False)
    h = _rglru(h, norm_mix[1], lru_w_in, lru_conv_w[0], lru_conv_b[0], lru_w_a, lru_b_a[0],
               lru_w_x, lru_b_x[0], lru_lambda[0], lru_w_out, steps_lru)
    return _ffn(h, norm_ffn[1], ffn_w_up, ffn_conv_w[1], ffn_conv_b[1], ffn_w_down, norm_final,
                1, steps, batch_major_in=False, final=True)
```

```python
import functools
import math

import jax
import jax.numpy as jnp
from jax import lax
from jax.experimental import pallas as pl
from jax.experimental.pallas import tpu as pltpu

F32 = jnp.float32
BF16 = jnp.bfloat16

EPS = 1e-6
ROPE_BASE = 10000.0
HEADS = 4
HEAD_DIM = 128
CHUNK = 128
UP_AHEAD = 1
STAGE_BYTES = 3 * 1024 * 1024
ROW_CHUNK = 64
INV_BASE = 16
GDN_CONV = 4
LRU_CONV = 4
FFN_CONV = 3
LRU_C = 8.0
LRU_BLOCK = 128
SUBLANES = 8
LANES = 128
NEG_BIG = -1e30
VMEM_LIMIT = 56 * 1024 * 1024

RET_LOG_GAMMA = tuple(math.log1p(-2.0 ** (-5.0 - h)) for h in range(HEADS))


def _rms(x):
    return x * lax.rsqrt(jnp.mean(x * x, axis=-1, keepdims=True) + EPS)


def _sigmoid(x):
    return 1.0 / (1.0 + jnp.exp(-x))


def _silu(x):
    return x * _sigmoid(x)


def _softplus(x):
    return jnp.maximum(x, 0.0) + jnp.log1p(jnp.exp(-jnp.abs(x)))


def _gelu_tanh(x):
    c = 2.0 * math.sqrt(2.0 / math.pi)
    return x * _sigmoid(x * (c + (0.044715 * c) * (x * x)))


def _dot(a, b):
    return jnp.dot(a, b, preferred_element_type=F32)


def _dot_nt(a, b):
    return lax.dot_general(a, b, (((1,), (1,)), ((), ())), preferred_element_type=F32)


def _mm(a, b):
    return _dot(a.astype(BF16), b.astype(BF16))


def _split3(x):
    hi = x.astype(BF16)
    r1 = x - hi.astype(F32)
    mid = r1.astype(BF16)
    lo = (r1 - mid.astype(F32)).astype(BF16)
    return hi, mid, lo


def _run_pipeline(s, nsteps, set0, set1, step):
    last = nsteps - 1
    parity = lax.rem(s, 2)

    @pl.when(s == 0)
    def _():
        step(set0, set1, True, False)

    @pl.when((s > 0) & (s < last) & (parity == 0))
    def _():
        step(set0, set1, True, True)

    @pl.when((s < last) & (parity == 1))
    def _():
        step(set1, set0, True, True)

    @pl.when(s == last)
    def _():
        if last % 2 == 0:
            step(set0, set1, False, True)
        else:
            step(set1, set0, False, True)


def _stage_rows(rows, cols):
    fits = [r for r in range(SUBLANES, rows + 1, SUBLANES) if rows % r == 0 and r * cols * 4 <= STAGE_BYTES]
    return fits[-1]


def _stage_scratch(rows, cols):
    return pltpu.VMEM((2, _stage_rows(rows, cols), cols), F32)


def _stage_bf16(src, dst_ref, stage_ref, sem_ref):
    rows, cols = src.shape
    row_chunk = stage_ref.shape[1]
    nchunk = rows // row_chunk

    def copy(i):
        return pltpu.make_async_copy(src.at[pl.ds(i * row_chunk, row_chunk), :], stage_ref.at[i % 2],
                                     sem_ref.at[i % 2])

    copy(0).start()
    for i in range(nchunk):
        if i + 1 < nchunk:
            copy(i + 1).start()
        copy(i).wait()
        dst_ref[pl.ds(i * row_chunk, row_chunk), :] = stage_ref[i % 2].astype(BF16)


def _stage_gate_blocks(wa, wx, wax_ref, stage_ref, sem_ref):
    copies = [pltpu.make_async_copy(w, stage_ref.at[i], sem_ref.at[i]) for i, w in enumerate((wa, wx))]
    for c in copies:
        c.start()
    for c in copies:
        c.wait()
    for n in range(wax_ref.shape[0]):
        wax_ref[n, :, 0:LRU_BLOCK] = stage_ref[0, n].astype(BF16)
        wax_ref[n, :, LRU_BLOCK:2 * LRU_BLOCK] = stage_ref[1, n].astype(BF16)


def _inproj_kernel(x_ref, gain_ref, win_hbm, wsmall_ref, cos_ref, sin_ref, convw_ref,
                   alog_ref, dtb_ref, ret_ref, gdn_ref, small_ref, pbuf0_ref, pbuf1_ref, hb_ref,
                   wmain_ref, stage_ref, sem_ref, *, tiles_per_seq, nsteps):
    s = pl.program_id(0)
    tile = x_ref.shape[1]
    width = HEADS * HEAD_DIM
    nmain = 8 * width
    body = slice(SUBLANES, SUBLANES + tile)
    qkv = slice(4 * width, 7 * width)

    @pl.when(s == 0)
    def _():
        pbuf0_ref[0:SUBLANES, qkv] = jnp.zeros((SUBLANES, 3 * width), F32)
        _stage_bf16(win_hbm.at[0, :, pl.ds(0, 8 * HEADS * HEAD_DIM)], wmain_ref, stage_ref, sem_ref)

    def stage_a(pa_ref):
        for r in range(tile // ROW_CHUNK):
            rows = slice(r * ROW_CHUNK, (r + 1) * ROW_CHUNK)
            hb_ref[rows, :] = (_rms(x_ref[0, rows, :]) * gain_ref[...]).astype(BF16)
        for c0, c1 in ((0, 2 * width), (2 * width, 4 * width), (4 * width, 7 * width), (7 * width, nmain)):
            pa_ref[body, c0:c1] = _dot(hb_ref[...], wmain_ref[:, c0:c1])
        pa_ref[body, nmain:nmain + LANES] = _dot(hb_ref[...], wsmall_ref[...])

    def stage_b(pa_ref, pb_ref):
        first_of_seq = lax.rem(jnp.maximum(s - 1, 0), tiles_per_seq) == 0
        pb_ref[0:SUBLANES, qkv] = jnp.where(first_of_seq, 0.0, pb_ref[0:SUBLANES, qkv])
        for r in range(tile // ROW_CHUNK):
            out_rows = slice(r * ROW_CHUNK, (r + 1) * ROW_CHUNK)
            rows = slice(SUBLANES + r * ROW_CHUNK, SUBLANES + (r + 1) * ROW_CHUNK)
            cos = cos_ref[out_rows, :]
            sin = sin_ref[out_rows, :]
            for part in range(2):
                for h in range(HEADS):
                    c0 = part * width + h * HEAD_DIM
                    ph = pb_ref[rows, c0:c0 + HEAD_DIM]
                    rot = ph * cos + pltpu.roll(ph, HEAD_DIM // 2, 1) * sin
                    if part == 1:
                        rot = rot * (HEAD_DIM ** -0.5)
                    ret_ref[0, out_rows, c0:c0 + HEAD_DIM] = rot.astype(BF16)
            for c0 in range(2 * width, 4 * width, LANES):
                ret_ref[0, out_rows, c0:c0 + LANES] = pb_ref[rows, c0:c0 + LANES].astype(BF16)
            for c0 in range(0, width, LANES):
                gdn_ref[0, out_rows, 3 * width + c0:3 * width + c0 + LANES] = (
                    pb_ref[rows, 7 * width + c0:7 * width + c0 + LANES].astype(BF16))
            for part in range(3):
                for h in range(HEADS):
                    c0 = part * width + h * HEAD_DIM
                    src_cols = slice(4 * width + c0, 4 * width + c0 + HEAD_DIM)
                    cw = convw_ref[:, c0:c0 + HEAD_DIM]
                    acc = pb_ref[rows, src_cols] * cw[GDN_CONV - 1:GDN_CONV, :]
                    for j in range(GDN_CONV - 1):
                        back = GDN_CONV - 1 - j
                        shifted = pb_ref[rows.start - back:rows.stop - back, src_cols]
                        acc = acc + shifted * cw[j:j + 1, :]
                    act = _silu(acc)
                    if part < 2:
                        act = act * lax.rsqrt(jnp.sum(act * act, axis=-1, keepdims=True) + EPS)
                    if part == 0:
                        act = act * (HEAD_DIM ** -0.5)
                    gdn_ref[0, out_rows, c0:c0 + HEAD_DIM] = act.astype(BF16)
            sm = pb_ref[rows, nmain:nmain + LANES]
            col = lax.broadcasted_iota(jnp.int32, sm.shape, 1)
            beta = _sigmoid(sm)
            g = -jnp.exp(alog_ref[...]) * _softplus(sm + dtb_ref[...])
            small_ref[0, out_rows, :] = jnp.where(col < HEADS, beta, g)
        pa_ref[0:SUBLANES, qkv] = pb_ref[tile:tile + SUBLANES, qkv]

    def step(set_a, set_b, do_a, do_b):
        if do_a:
            stage_a(*set_a)
        if do_b:
            stage_b(*set_a, *set_b)

    _run_pipeline(s, nsteps, (pbuf0_ref,), (pbuf1_ref,), step)


def _mixer_kernel(ret_ref, gdn_ref, small_ref, x_ref, wout_hbm, ogain_ref, out_ref,
                  sret_ref, sgdn_ref, ybuf_ref, wout_ref, stage_ref, sem_ref):
    ti = pl.program_id(1)

    @pl.when((pl.program_id(0) == 0) & (ti == 0))
    def _():
        _stage_bf16(wout_hbm.at[0], wout_ref, stage_ref, sem_ref)

    tile = x_ref.shape[1]
    width = HEADS * HEAD_DIM
    c = CHUNK

    @pl.when(ti == 0)
    def _():
        sret_ref[...] = jnp.zeros(sret_ref.shape, F32)
        sgdn_ref[...] = jnp.zeros(sgdn_ref.shape, F32)

    row = lax.broadcasted_iota(jnp.int32, (c, c), 0)
    col = lax.broadcasted_iota(jnp.int32, (c, c), 1)
    tril = row >= col
    strict = row > col
    rowf = row.astype(F32)
    relf = (row - col).astype(F32)
    eye = jnp.where(row == col, 1.0, 0.0).astype(F32)
    tri_b = jnp.where(tril, 1.0, 0.0).astype(BF16)
    ogain = ogain_ref[...]
    nblk = tile // c
    probs = [(blk, h) for blk in range(nblk) for h in range(HEADS)]

    def cols(ref, part, blk, h):
        c0 = part * width + h * HEAD_DIM
        return ref[0, blk * c:(blk + 1) * c, c0:c0 + HEAD_DIM]


    sms, gcums, gcum_ts = [], [], []
    for blk in range(nblk):
        sm = small_ref[0, blk * c:(blk + 1) * c, :]
        parts = _dot(tri_b, jnp.concatenate(_split3(sm), axis=1))
        gcum = parts[:, :LANES] + parts[:, LANES:2 * LANES] + parts[:, 2 * LANES:]
        sms.append(sm)
        gcums.append(gcum)
        gcum_ts.append(gcum.T)
    ret_qk = {}
    gdn_kq = {}
    for p in probs:
        blk, h = p
        ret_qk[p] = _dot_nt(cols(ret_ref, 0, blk, h), cols(ret_ref, 1, blk, h))
        k = cols(gdn_ref, 1, blk, h)
        gdn_kq[p] = _dot_nt(jnp.concatenate([k, cols(gdn_ref, 0, blk, h)], axis=0), k)

    ret_iv = {}
    for p in probs:
        blk, h = p
        lg = RET_LOG_GAMMA[h]
        dmask = jnp.where(tril, jnp.exp(lg * jnp.where(tril, relf, 0.0)), 0.0)
        scores = (ret_qk[p] * dmask).astype(BF16)
        k_tail_t = (cols(ret_ref, 1, blk, h).astype(F32) * jnp.exp(lg * (c - 1.0 - rowf))).T.astype(BF16)
        ret_iv[p] = _dot(jnp.concatenate([scores, k_tail_t], axis=0), cols(ret_ref, 2, blk, h))

    lmats, attns, rhs, kts, egcs, glasts = {}, {}, {}, {}, {}, {}
    for p in probs:
        blk, h = p
        gcum, sm = gcums[blk], sms[blk]
        gc_col = jnp.broadcast_to(gcum[:, HEADS + h:HEADS + h + 1], (c, c))
        gc_row = jnp.broadcast_to(gcum_ts[blk][HEADS + h:HEADS + h + 1, :], (c, c))
        g_last = jnp.broadcast_to(gcum[c - 1:c, HEADS + h:HEADS + h + 1], (c, c))
        beta_col = jnp.broadcast_to(sm[:, h:h + 1], (c, c))
        decay = jnp.exp(jnp.where(tril, gc_col - gc_row, NEG_BIG))
        kf = cols(gdn_ref, 1, blk, h).astype(F32)
        egc = jnp.exp(gc_col)
        lmats[p] = jnp.where(strict, gdn_kq[p][:c] * beta_col * decay, 0.0)
        attns[p] = jnp.where(tril, gdn_kq[p][c:] * decay, 0.0).astype(BF16)
        rhs[p] = jnp.concatenate([(cols(gdn_ref, 2, blk, h).astype(F32) * beta_col).astype(BF16),
                                  (kf * (beta_col * egc)).astype(BF16)], axis=1)
        kts[p] = (kf * jnp.exp(g_last - gc_col)).T.astype(BF16)
        egcs[p] = egc
        glasts[p] = g_last

    def same_block(size):
        return (row // size) == (col // size)

    diag = same_block(INV_BASE)
    tinv, power = {}, {}
    for p in probs:
        ld = jnp.where(diag, lmats[p], 0.0)
        tinv[p] = eye - ld
        power[p] = _mm(ld, ld)
    factors = int(math.log2(INV_BASE)) - 1
    for f in range(factors):
        last = f == factors - 1
        for p in probs:
            pw = power[p].astype(BF16)
            if last:
                tinv[p] = tinv[p] + _dot(tinv[p].astype(BF16), pw)
            else:
                both = _dot(jnp.concatenate([tinv[p].astype(BF16), pw], axis=0), pw)
                tinv[p] = tinv[p] + both[:c]
                power[p] = both[c:]
    size = INV_BASE
    while size < c:
        off = same_block(2 * size) & jnp.logical_not(same_block(size))
        ct = {p: _dot(jnp.where(off, lmats[p], 0.0).astype(BF16), tinv[p].astype(BF16)) for p in probs}
        for p in probs:
            tinv[p] = tinv[p] - _dot(tinv[p].astype(BF16), ct[p].astype(BF16))
        size *= 2

    ret_inter = {}
    for h in range(HEADS):
        lg = RET_LOG_GAMMA[h]
        state = sret_ref[h]
        for blk in range(nblk):
            ret_inter[(blk, h)] = _dot(cols(ret_ref, 0, blk, h), state.astype(BF16))
            state = state * math.exp(lg * c) + ret_iv[(blk, h)][c:]
        sret_ref[h] = state

    uw = {p: _dot(tinv[p].astype(BF16), rhs[p]).astype(BF16) for p in probs}
    apn = {}
    for p in probs:
        wu = jnp.concatenate([uw[p][:, HEAD_DIM:], uw[p][:, :HEAD_DIM]], axis=1)
        apn[p] = _dot(jnp.concatenate([attns[p], kts[p]], axis=0), wu)

    for p in probs:
        blk, h = p
        lg = RET_LOG_GAMMA[h]
        o = ret_iv[p][:c] + ret_inter[p] * jnp.exp(lg * (rowf + 1.0))
        y = _rms(o) * _silu(cols(ret_ref, 3, blk, h).astype(F32))
        ybuf_ref[blk * c:(blk + 1) * c, h * HEAD_DIM:(h + 1) * HEAD_DIM] = y.astype(BF16)

    states = [sgdn_ref[h] for h in range(HEADS)]
    for blk in range(nblk):
        for h in range(HEADS):
            p = (blk, h)
            q_eff = (cols(gdn_ref, 0, blk, h).astype(F32) * egcs[p] - apn[p][:c, :HEAD_DIM]).astype(BF16)
            pq = jnp.concatenate([apn[p][c:, :HEAD_DIM].astype(BF16), q_eff], axis=0)
            ps = _dot(pq, states[h].astype(BF16))
            o = ps[c:] + apn[p][:c, HEAD_DIM:]
            states[h] = jnp.exp(glasts[p]) * states[h] - ps[:c] + apn[p][c:, HEAD_DIM:]
            y = _rms(o) * ogain * _silu(cols(gdn_ref, 3, blk, h).astype(F32))
            ybuf_ref[blk * c:(blk + 1) * c, width + h * HEAD_DIM:width + (h + 1) * HEAD_DIM] = y.astype(BF16)
    for h in range(HEADS):
        sgdn_ref[h] = states[h]

    out_ref[0] = x_ref[0] + _dot(ybuf_ref[...], wout_ref[...])


def _rglru_kernel(h_ref, gain_ref, win_hbm, convw_ref, convb_ref, wa_hbm, wx_hbm, ba_ref, bx_ref,
                  lam_ref, wout_hbm, out_ref, state_ref, a_ref, u_ref, hs_ref, xr_ref, hb_ref,
                  hin0_ref, hin1_ref, gate0_ref, gate1_ref, xp0_ref, xp1_ref,
                  win_ref, wout_ref, wax_ref, stage_in_ref, stage_out_ref, stage_blk_ref, sem_ref, *, nsteps):
    s = pl.program_id(0)
    rows = h_ref.shape[0]
    d_rnn = lam_ref.shape[1]
    steps = rows // SUBLANES
    hist = (LRU_CONV - 1) * SUBLANES
    nblock = d_rnn // LRU_BLOCK

    @pl.when(s == 0)
    def _():
        state_ref[...] = jnp.zeros(state_ref.shape, F32)
        xp0_ref[0:hist, :] = jnp.zeros((hist, d_rnn), F32)
        _stage_bf16(win_hbm.at[0], win_ref, stage_in_ref, sem_ref)
        _stage_bf16(wout_hbm.at[0], wout_ref, stage_out_ref, sem_ref)
        _stage_gate_blocks(wa_hbm.at[0], wx_hbm.at[0], wax_ref, stage_blk_ref, sem_ref)

    def step(set_a, set_b, do_a, do_b):
        hin_a, gate_a, xp_a = set_a
        hin_b, gate_b, xp_b = set_b
        if do_a:
            hin_a[...] = h_ref[...]
            for r in range(rows // ROW_CHUNK):
                rs = slice(r * ROW_CHUNK, (r + 1) * ROW_CHUNK)
                hb_ref[rs, :] = (_rms(h_ref[rs, :]) * gain_ref[...]).astype(BF16)

        ris = []
        slab = 2 * d_rnn // nblock
        for n in range(nblock):
            c0 = n * LRU_BLOCK
            if do_b:
                cw = convw_ref[:, c0:c0 + LRU_BLOCK]
                for r in range(rows // ROW_CHUNK):
                    lo = hist + r * ROW_CHUNK
                    xr = (xp_b[lo:lo + ROW_CHUNK, c0:c0 + LRU_BLOCK] * cw[LRU_CONV - 1:LRU_CONV, :]
                          + convb_ref[:, c0:c0 + LRU_BLOCK])
                    for j in range(LRU_CONV - 1):
                        back = (LRU_CONV - 1 - j) * SUBLANES
                        xr = xr + xp_b[lo - back:lo - back + ROW_CHUNK, c0:c0 + LRU_BLOCK] * cw[j:j + 1, :]
                    xr_ref[r * ROW_CHUNK:(r + 1) * ROW_CHUNK, c0:c0 + LRU_BLOCK] = xr
                ris.append(_dot(xr_ref[:, c0:c0 + LRU_BLOCK].astype(BF16), wax_ref[n]))
            if do_a:
                p0 = n * slab
                proj = _dot(hb_ref[...], win_ref[:, p0:p0 + slab])
                if p0 < d_rnn:
                    gate_a[:, p0:p0 + slab] = proj
                else:
                    xp_a[hist:hist + rows, p0 - d_rnn:p0 - d_rnn + slab] = proj
        if not do_b:
            return
        xp_a[0:hist, :] = xp_b[rows:rows + hist, :]

        sp = _softplus(-lam_ref[...]) * (-LRU_C)
        for n in range(nblock):
            c0 = n * LRU_BLOCK
            ri = ris[n]
            r = _sigmoid(ri[:, :LRU_BLOCK] + ba_ref[:, c0:c0 + LRU_BLOCK])
            i = _sigmoid(ri[:, LRU_BLOCK:] + bx_ref[:, c0:c0 + LRU_BLOCK])
            a = jnp.exp(r * sp[:, c0:c0 + LRU_BLOCK])
            a_ref[:, c0:c0 + LRU_BLOCK] = a
            x = 1.0 - a * a
            root = jnp.where(x > 0.0, x * lax.rsqrt(x), 0.0)
            u_ref[:, c0:c0 + LRU_BLOCK] = root * (i * xr_ref[:, c0:c0 + LRU_BLOCK])

        state = state_ref[...]
        for t in range(steps):
            sl = slice(t * SUBLANES, (t + 1) * SUBLANES)
            state = a_ref[sl, :] * state + u_ref[sl, :]
            hs_ref[sl, :] = state
        state_ref[...] = state

        y = (_gelu_tanh(gate_b[...]) * hs_ref[...]).astype(BF16)
        out_ref[...] = hin_b[...] + _dot(y, wout_ref[...])

    _run_pipeline(s, nsteps, (hin0_ref, gate0_ref, xp0_ref), (hin1_ref, gate1_ref, xp1_ref), step)


def _ffn_kernel(h_ref, gain_ref, wup_hbm, convw_ref, convb_ref, wdown_hbm, fgain_ref, out_ref,
                halo_ref, slab_ref, hin0_ref, hin1_ref, hb0_ref, hb1_ref, wup_ref, wdown_ref,
                stage_up_ref, stage_down_ref, sem_ref,
                *, layer, nsteps, d_ff, col_chunk, batch_major_in, final):
    s = pl.program_id(0)
    halo = (FFN_CONV - 1) * SUBLANES
    rows = hin0_ref.shape[0]

    @pl.when(s == 0)
    def _():
        halo_ref[...] = jnp.zeros(halo_ref.shape, F32)
        _stage_bf16(wup_hbm.at[layer], wup_ref, stage_up_ref, sem_ref)
        _stage_bf16(wdown_hbm.at[layer], wdown_ref, stage_down_ref, sem_ref)

    def stage_a(hin_a, hb_a):
        if batch_major_in:
            steps = h_ref.shape[1]
            nslab = h_ref.shape[2] // LANES
            for i in range(nslab):
                for b in range(SUBLANES):
                    slab_ref[i, pl.ds(b, steps, stride=SUBLANES), :] = h_ref[b, :, i * LANES:(i + 1) * LANES]
            for i in range(nslab):
                hin_a[:, i * LANES:(i + 1) * LANES] = slab_ref[i]
        else:
            hin_a[...] = h_ref[...]
        for r in range(rows // ROW_CHUNK):
            rs = slice(r * ROW_CHUNK, (r + 1) * ROW_CHUNK)
            hb_a[rs, :] = (_rms(hin_a[rs, :]) * gain_ref[...]).astype(BF16)

    def stage_b(hin_b, hb_b):
        def conv(cur, c0):
            prev = halo_ref[:, c0:c0 + col_chunk]
            cw = convw_ref[:, c0:c0 + col_chunk]
            out = cur * cw[FFN_CONV - 1:FFN_CONV, :] + convb_ref[:, c0:c0 + col_chunk]
            for j in range(FFN_CONV - 1):
                back = (FFN_CONV - 1 - j) * SUBLANES
                shifted = jnp.concatenate([prev[halo - back:, :], cur[:rows - back, :]], axis=0)
                out = out + shifted * cw[j:j + 1, :]
            halo_ref[:, c0:c0 + col_chunk] = cur[rows - halo:, :]
            return out

        def up(j):
            g0 = j * col_chunk
            v0 = d_ff + j * col_chunk
            return _dot(hb_b[...], wup_ref[:, g0:g0 + col_chunk]), _dot(hb_b[...], wup_ref[:, v0:v0 + col_chunk])

        nchunk = d_ff // col_chunk
        acc = jnp.zeros((rows, hin_b.shape[1]), F32)
        queue = [up(j) for j in range(min(UP_AHEAD, nchunk))]
        for j in range(nchunk):
            if j + UP_AHEAD < nchunk:
                queue.append(up(j + UP_AHEAD))
            cur = queue.pop(0)
            g0 = j * col_chunk
            gate = conv(cur[0], g0)
            val = conv(cur[1], d_ff + g0)
            act = (_silu(gate) * val).astype(BF16)
            acc = acc + _dot(act, wdown_ref[g0:g0 + col_chunk, :])
        res = hin_b[...] + acc
        if not final:
            out_ref[...] = res
        else:
            fin = _rms(res) * fgain_ref[...]
            steps = rows // SUBLANES
            for i in range(fin.shape[1] // LANES):
                slab_ref[i] = fin[:, i * LANES:(i + 1) * LANES]
            for i in range(fin.shape[1] // LANES):
                for b in range(SUBLANES):
                    out_ref[b, :, i * LANES:(i + 1) * LANES] = slab_ref[i, pl.ds(b, steps, stride=SUBLANES), :]

    def step(set_a, set_b, do_a, do_b):
        if do_a:
            stage_a(*set_a)
        if do_b:
            stage_b(*set_b)

    _run_pipeline(s, nsteps, (hin0_ref, hb0_ref), (hin1_ref, hb1_ref), step)


def _const_spec(shape):
    zeros = (0,) * len(shape)
    return pl.BlockSpec(shape, lambda *_: zeros)


def _params(sem, flags=None):
    return pltpu.CompilerParams(dimension_semantics=sem, vmem_limit_bytes=VMEM_LIMIT, flags=flags)


def _rope_tables(t):
    half = HEAD_DIM // 2
    inv_freq = ROPE_BASE ** (-jnp.arange(half, dtype=F32) / half)
    ang = jnp.arange(t, dtype=F32)[:, None] * inv_freq[None, :]
    cos = jnp.cos(ang)
    sin = jnp.sin(ang)
    return jnp.concatenate([cos, cos], axis=-1), jnp.concatenate([-sin, sin], axis=-1)


def _layer0_inproj(x, gain, w_in_all, conv_w, a_log, dt_bias, tile):
    b, t, d = x.shape
    width = HEADS * HEAD_DIM
    wsmall = jnp.zeros((d, LANES), F32).at[:, :2 * HEADS].set(w_in_all[0, :, 8 * width:]).astype(BF16)
    alog = jnp.zeros((1, LANES), F32).at[0, HEADS:2 * HEADS].set(a_log)
    dtb = jnp.zeros((1, LANES), F32).at[0, HEADS:2 * HEADS].set(dt_bias)
    cos, sin = _rope_tables(t)
    tiles_per_seq = t // tile
    ntiles = b * tiles_per_seq

    def in_tile(s):
        s = jnp.minimum(s, ntiles - 1)
        return s // tiles_per_seq, s % tiles_per_seq

    def out_tile(s):
        s = jnp.maximum(s - 1, 0)
        return s // tiles_per_seq, s % tiles_per_seq

    out_map = lambda s: (*out_tile(s), 0)
    pbuf = pltpu.VMEM((SUBLANES + tile, 8 * width + LANES), F32)
    return pl.pallas_call(
        functools.partial(_inproj_kernel, tiles_per_seq=tiles_per_seq, nsteps=ntiles + 1),
        grid=(ntiles + 1,),
        in_specs=[
            pl.BlockSpec((1, tile, d), lambda s: (*in_tile(s), 0)),
            _const_spec((1, d)),
            pl.BlockSpec(memory_space=pl.ANY),
            _const_spec((d, LANES)),
            pl.BlockSpec((tile, HEAD_DIM), lambda s: (out_tile(s)[1], 0)),
            pl.BlockSpec((tile, HEAD_DIM), lambda s: (out_tile(s)[1], 0)),
            _const_spec((GDN_CONV, 3 * width)),
            _const_spec((1, LANES)),
            _const_spec((1, LANES)),
        ],
        out_specs=[
            pl.BlockSpec((1, tile, 4 * width), out_map),
            pl.BlockSpec((1, tile, 4 * width), out_map),
            pl.BlockSpec((1, tile, LANES), out_map),
        ],
        out_shape=[
            jax.ShapeDtypeStruct((b, t, 4 * width), BF16),
            jax.ShapeDtypeStruct((b, t, 4 * width), BF16),
            jax.ShapeDtypeStruct((b, t, LANES), F32),
        ],
        scratch_shapes=[pbuf, pbuf, pltpu.VMEM((tile, d), BF16),
                        pltpu.VMEM((d, 8 * width), BF16),
                        _stage_scratch(d, 8 * width),
                        pltpu.SemaphoreType.DMA((2,))],
        compiler_params=_params(("arbitrary",)),
        name="l0_inproj",
    )(x, gain.reshape(1, d), w_in_all, wsmall, cos, sin, conv_w, alog, dtb)


def _layer0_mixer(ret, gdn, small, x, w_out_all, out_gain, tile):
    b, t, d = x.shape
    width = HEADS * HEAD_DIM
    return pl.pallas_call(
        _mixer_kernel,
        grid=(b, t // tile),
        in_specs=[
            pl.BlockSpec((1, tile, 4 * width), lambda i, j: (i, j, 0)),
            pl.BlockSpec((1, tile, 4 * width), lambda i, j: (i, j, 0)),
            pl.BlockSpec((1, tile, LANES), lambda i, j: (i, j, 0)),
            pl.BlockSpec((1, tile, d), lambda i, j: (i, j, 0)),
            pl.BlockSpec(memory_space=pl.ANY),
            _const_spec((1, HEAD_DIM)),
        ],
        out_specs=pl.BlockSpec((1, tile, d), lambda i, j: (i, j, 0)),
        out_shape=jax.ShapeDtypeStruct((b, t, d), F32),
        scratch_shapes=[
            pltpu.VMEM((HEADS, HEAD_DIM, HEAD_DIM), F32),
            pltpu.VMEM((HEADS, HEAD_DIM, HEAD_DIM), F32),
            pltpu.VMEM((tile, 2 * width), BF16),
            pltpu.VMEM((2 * width, d), BF16),
            _stage_scratch(2 * width, d),
            pltpu.SemaphoreType.DMA((2,)),
        ],
        compiler_params=_params(("arbitrary", "arbitrary")),
        name="l0_mixer",
    )(ret, gdn, small, x, w_out_all, out_gain.reshape(1, HEAD_DIM))


def _rglru(h_tm, gain, w_in_all, conv_w, conv_b, w_a_all, b_a, w_x_all, b_x, lam, w_out_all, steps):
    n, d = h_tm.shape
    d_rnn = lam.shape[0]
    rows = steps * SUBLANES
    ntiles = n // rows
    nblock = d_rnn // LRU_BLOCK
    hist = (LRU_CONV - 1) * SUBLANES
    any_spec = pl.BlockSpec(memory_space=pl.ANY)
    return pl.pallas_call(
        functools.partial(_rglru_kernel, nsteps=ntiles + 1),
        grid=(ntiles + 1,),
        in_specs=[
            pl.BlockSpec((rows, d), lambda s: (jnp.minimum(s, ntiles - 1), 0)),
            _const_spec((1, d)),
            any_spec,
            _const_spec((LRU_CONV, d_rnn)),
            _const_spec((1, d_rnn)),
            any_spec,
            any_spec,
            _const_spec((1, d_rnn)),
            _const_spec((1, d_rnn)),
            _const_spec((1, d_rnn)),
            any_spec,
        ],
        out_specs=pl.BlockSpec((rows, d), lambda s: (jnp.maximum(s - 1, 0), 0)),
        out_shape=jax.ShapeDtypeStruct((n, d), F32),
        scratch_shapes=[
            pltpu.VMEM((SUBLANES, d_rnn), F32),
            pltpu.VMEM((rows, d_rnn), F32),
            pltpu.VMEM((rows, d_rnn), F32),
            pltpu.VMEM((rows, d_rnn), F32),
            pltpu.VMEM((rows, d_rnn), F32),
            pltpu.VMEM((rows, d), BF16),
            pltpu.VMEM((rows, d), F32), pltpu.VMEM((rows, d), F32),
            pltpu.VMEM((rows, d_rnn), F32), pltpu.VMEM((rows, d_rnn), F32),
            pltpu.VMEM((hist + rows, d_rnn), F32), pltpu.VMEM((hist + rows, d_rnn), F32),
            pltpu.VMEM((d, 2 * d_rnn), BF16),
            pltpu.VMEM((d_rnn, d), BF16),
            pltpu.VMEM((nblock, LRU_BLOCK, 2 * LRU_BLOCK), BF16),
            _stage_scratch(d, 2 * d_rnn),
            _stage_scratch(d_rnn, d),
            pltpu.VMEM((2, nblock, LRU_BLOCK, LRU_BLOCK), F32),
            pltpu.SemaphoreType.DMA((2,)),
        ],
        compiler_params=_params(("arbitrary",)),
        name="l1_rglru",
    )(h_tm, gain.reshape(1, d), w_in_all, conv_w, conv_b.reshape(1, d_rnn), w_a_all, w_x_all,
      b_a.reshape(1, d_rnn), b_x.reshape(1, d_rnn), lam.reshape(1, d_rnn), w_out_all)


def _ffn(h, gain, w_up_all, conv_w, conv_b, w_down_all, final_gain, layer, steps, batch_major_in, final):
    d = h.shape[-1]
    n = h.size // d
    d_ff = w_down_all.shape[1]
    rows = steps * SUBLANES
    col_chunk = 2 * LANES
    ntiles = n // rows
    kern = functools.partial(_ffn_kernel, layer=layer, nsteps=ntiles + 1, d_ff=d_ff, col_chunk=col_chunk,
                             batch_major_in=batch_major_in, final=final)
    scratch = [pltpu.VMEM(((FFN_CONV - 1) * SUBLANES, 2 * d_ff), F32),
               pltpu.VMEM((d // LANES, rows, LANES), F32),
               pltpu.VMEM((rows, d), F32), pltpu.VMEM((rows, d), F32),
               pltpu.VMEM((rows, d), BF16), pltpu.VMEM((rows, d), BF16),
               pltpu.VMEM((d, 2 * d_ff), BF16),
               pltpu.VMEM((d_ff, d), BF16),
               _stage_scratch(d, 2 * d_ff),
               _stage_scratch(d_ff, d),
               pltpu.SemaphoreType.DMA((2,))]
    in_tile = lambda s: jnp.minimum(s, ntiles - 1)
    out_tile = lambda s: jnp.maximum(s - 1, 0)
    bm_spec = lambda tile_of: pl.BlockSpec((SUBLANES, steps, d), lambda s: (0, tile_of(s), 0))
    tm_spec = lambda tile_of: pl.BlockSpec((rows, d), lambda s: (tile_of(s), 0))
    if final:
        out_spec = bm_spec(out_tile)
        out_shape = jax.ShapeDtypeStruct((SUBLANES, n // SUBLANES, d), F32)
    else:
        out_spec = tm_spec(out_tile)
        out_shape = jax.ShapeDtypeStruct((n, d), F32)
    return pl.pallas_call(
        kern,
        grid=(ntiles + 1,),
        in_specs=[
            bm_spec(in_tile) if batch_major_in else tm_spec(in_tile),
            _const_spec((1, d)),
            pl.BlockSpec(memory_space=pl.ANY),
            _const_spec((FFN_CONV, 2 * d_ff)),
            _const_spec((1, 2 * d_ff)),
            pl.BlockSpec(memory_space=pl.ANY),
            _const_spec((1, d)),
        ],
        out_specs=out_spec,
        out_shape=out_shape,
        scratch_shapes=scratch,
        compiler_params=_params(("arbitrary",)),
        name="ffn_final" if final else "ffn",
    )(h, gain.reshape(1, d), w_up_all, conv_w, conv_b.reshape(1, 2 * d_ff),
      w_down_all, final_gain.reshape(1, d))


def kernel(x, norm_mix, norm_ffn, ret_gdn_w_in, gdn_conv_w, gdn_a_log, gdn_dt_bias, gdn_out_gain,
           ret_gdn_w_out, lru_w_in, lru_conv_w, lru_conv_b, lru_w_a, lru_b_a, lru_w_x, lru_b_x,
           lru_lambda, lru_w_out, ffn_w_up, ffn_conv_w, ffn_conv_b, ffn_w_down, norm_final):
    b, t, d = x.shape
    assert b == SUBLANES and norm_mix.shape[0] == 2
    tile_in = 512
    tile_mix = 512
    steps = 32
    steps_lru = 64

    ret, gdn, small = _layer0_inproj(x, norm_mix[0], ret_gdn_w_in, gdn_conv_w[0], gdn_a_log[0],
                                     gdn_dt_bias[0], tile_in)
    h = _layer0_mixer(ret, gdn, small, x, ret_gdn_w_out, gdn_out_gain[0], tile_mix)
    h = _ffn(h, norm_ffn[0], ffn_w_up, ffn_conv_w[0], ffn_conv_b[0], ffn_w_down, norm_final,
             0, steps, batch_major_in=True, final=False)
    h = _rglru(h, norm_mix[1], lru_w_in, lru_conv_w[0], lru_conv_b[0], lru_w_a, lru_b_a[0],
               lru_w_x, lru_b_x[0], lru_lambda[0], lru_w_out, steps_lru)
    return _ffn(h, norm_ffn[1], ffn_w_up, ffn_conv_w[1], ffn_conv_b[1], ffn_w_down, norm_final,
                1, steps, batch_major_in=False, final=True)
```

```python
import functools
import math

import jax
import jax.numpy as jnp
from jax import lax
from jax.experimental import pallas as pl
from jax.experimental.pallas import tpu as pltpu

F32 = jnp.float32
BF16 = jnp.bfloat16

EPS = 1e-6
ROPE_BASE = 10000.0
HEADS = 4
HEAD_DIM = 128
CHUNK = 128
UP_AHEAD = 2
STAGE_BYTES = 3 * 1024 * 1024
ROW_CHUNK = 64
INV_BASE = 16
GDN_CONV = 4
LRU_CONV = 4
FFN_CONV = 3
LRU_C = 8.0
LRU_BLOCK = 128
SUBLANES = 8
LANES = 128
NEG_BIG = -1e30
VMEM_LIMIT = 56 * 1024 * 1024

RET_LOG_GAMMA = tuple(math.log1p(-2.0 ** (-5.0 - h)) for h in range(HEADS))


def _rms(x):
    return x * lax.rsqrt(jnp.mean(x * x, axis=-1, keepdims=True) + EPS)


def _sigmoid(x):
    return 1.0 / (1.0 + jnp.exp2(x * (-1.0 / math.log(2.0))))


def _silu(x):
    return x * _sigmoid(x)


def _softplus(x):
    return jnp.maximum(x, 0.0) + jnp.log1p(jnp.exp(-jnp.abs(x)))


def _gelu_tanh(x):
    c = 2.0 * math.sqrt(2.0 / math.pi)
    return x * _sigmoid(x * (c + (0.044715 * c) * (x * x)))


def _dot(a, b):
    return jnp.dot(a, b, preferred_element_type=F32)


def _dot_nt(a, b):
    return lax.dot_general(a, b, (((1,), (1,)), ((), ())), preferred_element_type=F32)


def _mm(a, b):
    return _dot(a.astype(BF16), b.astype(BF16))


def _split3(x):
    hi = x.astype(BF16)
    r1 = x - hi.astype(F32)
    mid = r1.astype(BF16)
    lo = (r1 - mid.astype(F32)).astype(BF16)
    return hi, mid, lo


def _run_pipeline(s, nsteps, set0, set1, step):
    last = nsteps - 1
    parity = lax.rem(s, 2)

    @pl.when(s == 0)
    def _():
        step(set0, set1, True, False)

    @pl.when((s > 0) & (s < last) & (parity == 0))
    def _():
        step(set0, set1, True, True)

    @pl.when((s < last) & (parity == 1))
    def _():
        step(set1, set0, True, True)

    @pl.when(s == last)
    def _():
        if last % 2 == 0:
            step(set0, set1, False, True)
        else:
            step(set1, set0, False, True)


def _stage_rows(rows, cols):
    fits = [r for r in range(SUBLANES, rows + 1, SUBLANES) if rows % r == 0 and r * cols * 4 <= STAGE_BYTES]
    return fits[-1]


def _stage_scratch(rows, cols):
    return pltpu.VMEM((2, _stage_rows(rows, cols), cols), F32)


def _stage_bf16(src, dst_ref, stage_ref, sem_ref):
    rows, cols = src.shape
    row_chunk = stage_ref.shape[1]
    nchunk = rows // row_chunk

    def copy(i):
        return pltpu.make_async_copy(src.at[pl.ds(i * row_chunk, row_chunk), :], stage_ref.at[i % 2],
                                     sem_ref.at[i % 2])

    copy(0).start()
    for i in range(nchunk):
        if i + 1 < nchunk:
            copy(i + 1).start()
        copy(i).wait()
        dst_ref[pl.ds(i * row_chunk, row_chunk), :] = stage_ref[i % 2].astype(BF16)


def _stage_gate_blocks(wa, wx, wax_ref, stage_ref, sem_ref):
    copies = [pltpu.make_async_copy(w, stage_ref.at[i], sem_ref.at[i]) for i, w in enumerate((wa, wx))]
    for c in copies:
        c.start()
    for c in copies:
        c.wait()
    for n in range(wax_ref.shape[0]):
        wax_ref[n, :, 0:LRU_BLOCK] = stage_ref[0, n].astype(BF16)
        wax_ref[n, :, LRU_BLOCK:2 * LRU_BLOCK] = stage_ref[1, n].astype(BF16)


def _inproj_kernel(x_ref, gain_ref, win_hbm, wsmall_ref, cos_ref, sin_ref, convw_ref,
                   alog_ref, dtb_ref, ret_ref, gdn_ref, small_ref, pbuf0_ref, pbuf1_ref, hb_ref,
                   wmain_ref, stage_ref, sem_ref, *, tiles_per_seq, nsteps):
    s = pl.program_id(0)
    tile = x_ref.shape[1]
    width = HEADS * HEAD_DIM
    nmain = 8 * width
    body = slice(SUBLANES, SUBLANES + tile)
    qkv = slice(4 * width, 7 * width)

    @pl.when(s == 0)
    def _():
        pbuf0_ref[0:SUBLANES, qkv] = jnp.zeros((SUBLANES, 3 * width), F32)
        _stage_bf16(win_hbm.at[0, :, pl.ds(0, 8 * HEADS * HEAD_DIM)], wmain_ref, stage_ref, sem_ref)

    def stage_a(pa_ref):
        for r in range(tile // ROW_CHUNK):
            rows = slice(r * ROW_CHUNK, (r + 1) * ROW_CHUNK)
            hb_ref[rows, :] = (_rms(x_ref[0, rows, :]) * gain_ref[...]).astype(BF16)
        for c0, c1 in ((0, 2 * width), (2 * width, 4 * width), (4 * width, 7 * width), (7 * width, nmain)):
            pa_ref[body, c0:c1] = _dot(hb_ref[...], wmain_ref[:, c0:c1])
        pa_ref[body, nmain:nmain + LANES] = _dot(hb_ref[...], wsmall_ref[...])

    def stage_b(pa_ref, pb_ref):
        first_of_seq = lax.rem(jnp.maximum(s - 1, 0), tiles_per_seq) == 0
        pb_ref[0:SUBLANES, qkv] = jnp.where(first_of_seq, 0.0, pb_ref[0:SUBLANES, qkv])
        for r in range(tile // ROW_CHUNK):
            out_rows = slice(r * ROW_CHUNK, (r + 1) * ROW_CHUNK)
            rows = slice(SUBLANES + r * ROW_CHUNK, SUBLANES + (r + 1) * ROW_CHUNK)
            cos = cos_ref[out_rows, :]
            sin = sin_ref[out_rows, :]
            for part in range(2):
                for h in range(HEADS):
                    c0 = part * width + h * HEAD_DIM
                    ph = pb_ref[rows, c0:c0 + HEAD_DIM]
                    rot = ph * cos + pltpu.roll(ph, HEAD_DIM // 2, 1) * sin
                    if part == 1:
                        rot = rot * (HEAD_DIM ** -0.5)
                    ret_ref[0, out_rows, c0:c0 + HEAD_DIM] = rot.astype(BF16)
            for c0 in range(2 * width, 4 * width, LANES):
                ret_ref[0, out_rows, c0:c0 + LANES] = pb_ref[rows, c0:c0 + LANES].astype(BF16)
            for c0 in range(0, width, LANES):
                gdn_ref[0, out_rows, 3 * width + c0:3 * width + c0 + LANES] = (
                    pb_ref[rows, 7 * width + c0:7 * width + c0 + LANES].astype(BF16))
            for part in range(3):
                for h in range(HEADS):
                    c0 = part * width + h * HEAD_DIM
                    src_cols = slice(4 * width + c0, 4 * width + c0 + HEAD_DIM)
                    cw = convw_ref[:, c0:c0 + HEAD_DIM]
                    acc = pb_ref[rows, src_cols] * cw[GDN_CONV - 1:GDN_CONV, :]
                    for j in range(GDN_CONV - 1):
                        back = GDN_CONV - 1 - j
                        shifted = pb_ref[rows.start - back:rows.stop - back, src_cols]
                        acc = acc + shifted * cw[j:j + 1, :]
                    act = _silu(acc)
                    if part < 2:
                        act = act * lax.rsqrt(jnp.sum(act * act, axis=-1, keepdims=True) + EPS)
                    if part == 0:
                        act = act * (HEAD_DIM ** -0.5)
                    gdn_ref[0, out_rows, c0:c0 + HEAD_DIM] = act.astype(BF16)
            sm = pb_ref[rows, nmain:nmain + LANES]
            col = lax.broadcasted_iota(jnp.int32, sm.shape, 1)
            beta = _sigmoid(sm)
            g = -jnp.exp(alog_ref[...]) * _softplus(sm + dtb_ref[...])
            small_ref[0, out_rows, :] = jnp.where(col < HEADS, beta, g)
        pa_ref[0:SUBLANES, qkv] = pb_ref[tile:tile + SUBLANES, qkv]

    def step(set_a, set_b, do_a, do_b):
        if do_a:
            stage_a(*set_a)
        if do_b:
            stage_b(*set_a, *set_b)

    _run_pipeline(s, nsteps, (pbuf0_ref,), (pbuf1_ref,), step)


def _mixer_kernel(ret_ref, gdn_ref, small_ref, x_ref, wout_hbm, ogain_ref, out_ref,
                  sret_ref, sgdn_ref, ybuf_ref, wout_ref, stage_ref, sem_ref):
    ti = pl.program_id(1)

    @pl.when((pl.program_id(0) == 0) & (ti == 0))
    def _():
        _stage_bf16(wout_hbm.at[0], wout_ref, stage_ref, sem_ref)

    tile = x_ref.shape[1]
    width = HEADS * HEAD_DIM
    c = CHUNK

    @pl.when(ti == 0)
    def _():
        sret_ref[...] = jnp.zeros(sret_ref.shape, F32)
        sgdn_ref[...] = jnp.zeros(sgdn_ref.shape, F32)

    row = lax.broadcasted_iota(jnp.int32, (c, c), 0)
    col = lax.broadcasted_iota(jnp.int32, (c, c), 1)
    tril = row >= col
    strict = row > col
    rowf = row.astype(F32)
    relf = (row - col).astype(F32)
    eye = jnp.where(row == col, 1.0, 0.0).astype(F32)
    tri_b = jnp.where(tril, 1.0, 0.0).astype(BF16)
    ogain = ogain_ref[...]
    nblk = tile // c
    probs = [(blk, h) for blk in range(nblk) for h in range(HEADS)]

    def cols(ref, part, blk, h):
        c0 = part * width + h * HEAD_DIM
        return ref[0, blk * c:(blk + 1) * c, c0:c0 + HEAD_DIM]


    sms, gcums, gcum_ts = [], [], []
    for blk in range(nblk):
        sm = small_ref[0, blk * c:(blk + 1) * c, :]
        parts = _dot(tri_b, jnp.concatenate(_split3(sm), axis=1))
        gcum = parts[:, :LANES] + parts[:, LANES:2 * LANES] + parts[:, 2 * LANES:]
        sms.append(sm)
        gcums.append(gcum)
        gcum_ts.append(gcum.T)
    ret_qk = {}
    gdn_kq = {}
    for p in probs:
        blk, h = p
        ret_qk[p] = _dot_nt(cols(ret_ref, 0, blk, h), cols(ret_ref, 1, blk, h))
        k = cols(gdn_ref, 1, blk, h)
        gdn_kq[p] = _dot_nt(jnp.concatenate([k, cols(gdn_ref, 0, blk, h)], axis=0), k)

    ret_iv = {}
    for p in probs:
        blk, h = p
        lg = RET_LOG_GAMMA[h]
        dmask = jnp.where(tril, jnp.exp(lg * jnp.where(tril, relf, 0.0)), 0.0)
        scores = (ret_qk[p] * dmask).astype(BF16)
        k_tail_t = (cols(ret_ref, 1, blk, h).astype(F32) * jnp.exp(lg * (c - 1.0 - rowf))).T.astype(BF16)
        ret_iv[p] = _dot(jnp.concatenate([scores, k_tail_t], axis=0), cols(ret_ref, 2, blk, h))

    lmats, attns, rhs, kts, egcs, glasts = {}, {}, {}, {}, {}, {}
    for p in probs:
        blk, h = p
        gcum, sm = gcums[blk], sms[blk]
        gc_col = jnp.broadcast_to(gcum[:, HEADS + h:HEADS + h + 1], (c, c))
        gc_row = jnp.broadcast_to(gcum_ts[blk][HEADS + h:HEADS + h + 1, :], (c, c))
        g_last = jnp.broadcast_to(gcum[c - 1:c, HEADS + h:HEADS + h + 1], (c, c))
        beta_col = jnp.broadcast_to(sm[:, h:h + 1], (c, c))
        decay = jnp.exp(jnp.where(tril, gc_col - gc_row, NEG_BIG))
        kf = cols(gdn_ref, 1, blk, h).astype(F32)
        egc = jnp.exp(gc_col)
        lmats[p] = jnp.where(strict, gdn_kq[p][:c] * beta_col * decay, 0.0)
        attns[p] = jnp.where(tril, gdn_kq[p][c:] * decay, 0.0).astype(BF16)
        rhs[p] = jnp.concatenate([(cols(gdn_ref, 2, blk, h).astype(F32) * beta_col).astype(BF16),
                                  (kf * (beta_col * egc)).astype(BF16)], axis=1)
        kts[p] = (kf * jnp.exp(g_last - gc_col)).T.astype(BF16)
        egcs[p] = egc
        glasts[p] = g_last

    def same_block(size):
        return (row // size) == (col // size)

    diag = same_block(INV_BASE)
    tinv, power = {}, {}
    for p in probs:
        ld = jnp.where(diag, lmats[p], 0.0)
        tinv[p] = eye - ld
        power[p] = _mm(ld, ld)
    factors = int(math.log2(INV_BASE)) - 1
    for f in range(factors):
        last = f == factors - 1
        for p in probs:
            pw = power[p].astype(BF16)
            if last:
                tinv[p] = tinv[p] + _dot(tinv[p].astype(BF16), pw)
            else:
                both = _dot(jnp.concatenate([tinv[p].astype(BF16), pw], axis=0), pw)
                tinv[p] = tinv[p] + both[:c]
                power[p] = both[c:]
    size = INV_BASE
    while size < c:
        off = same_block(2 * size) & jnp.logical_not(same_block(size))
        ct = {p: _dot(jnp.where(off, lmats[p], 0.0).astype(BF16), tinv[p].astype(BF16)) for p in probs}
        for p in probs:
            tinv[p] = tinv[p] - _dot(tinv[p].astype(BF16), ct[p].astype(BF16))
        size *= 2

    ret_inter = {}
    for h in range(HEADS):
        lg = RET_LOG_GAMMA[h]
        state = sret_ref[h]
        for blk in range(nblk):
            ret_inter[(blk, h)] = _dot(cols(ret_ref, 0, blk, h), state.astype(BF16))
            state = state * math.exp(lg * c) + ret_iv[(blk, h)][c:]
        sret_ref[h] = state

    uw = {p: _dot(tinv[p].astype(BF16), rhs[p]).astype(BF16) for p in probs}
    apn = {}
    for p in probs:
        wu = jnp.concatenate([uw[p][:, HEAD_DIM:], uw[p][:, :HEAD_DIM]], axis=1)
        apn[p] = _dot(jnp.concatenate([attns[p], kts[p]], axis=0), wu)

    for p in probs:
        blk, h = p
        lg = RET_LOG_GAMMA[h]
        o = ret_iv[p][:c] + ret_inter[p] * jnp.exp(lg * (rowf + 1.0))
        y = _rms(o) * _silu(cols(ret_ref, 3, blk, h).astype(F32))
        ybuf_ref[blk * c:(blk + 1) * c, h * HEAD_DIM:(h + 1) * HEAD_DIM] = y.astype(BF16)

    states = [sgdn_ref[h] for h in range(HEADS)]
    for blk in range(nblk):
        for h in range(HEADS):
            p = (blk, h)
            q_eff = (cols(gdn_ref, 0, blk, h).astype(F32) * egcs[p] - apn[p][:c, :HEAD_DIM]).astype(BF16)
            pq = jnp.concatenate([apn[p][c:, :HEAD_DIM].astype(BF16), q_eff], axis=0)
            ps = _dot(pq, states[h].astype(BF16))
            o = ps[c:] + apn[p][:c, HEAD_DIM:]
            states[h] = jnp.exp(glasts[p]) * states[h] - ps[:c] + apn[p][c:, HEAD_DIM:]
            y = _rms(o) * ogain * _silu(cols(gdn_ref, 3, blk, h).astype(F32))
            ybuf_ref[blk * c:(blk + 1) * c, width + h * HEAD_DIM:width + (h + 1) * HEAD_DIM] = y.astype(BF16)
    for h in range(HEADS):
        sgdn_ref[h] = states[h]

    out_ref[0] = x_ref[0] + _dot(ybuf_ref[...], wout_ref[...])


def _rglru_kernel(h_ref, gain_ref, win_hbm, convw_ref, convb_ref, wa_hbm, wx_hbm, ba_ref, bx_ref,
                  lam_ref, wout_hbm, out_ref, state_ref, a_ref, u_ref, hs_ref, xr_ref, hb_ref,
                  hin0_ref, hin1_ref, gate0_ref, gate1_ref, xp0_ref, xp1_ref,
                  win_ref, wout_ref, wax_ref, stage_in_ref, stage_out_ref, stage_blk_ref, sem_ref, *, nsteps):
    s = pl.program_id(0)
    rows = h_ref.shape[0]
    d_rnn = lam_ref.shape[1]
    steps = rows // SUBLANES
    hist = (LRU_CONV - 1) * SUBLANES
    nblock = d_rnn // LRU_BLOCK

    @pl.when(s == 0)
    def _():
        state_ref[...] = jnp.zeros(state_ref.shape, F32)
        xp0_ref[0:hist, :] = jnp.zeros((hist, d_rnn), F32)
        _stage_bf16(win_hbm.at[0], win_ref, stage_in_ref, sem_ref)
        _stage_bf16(wout_hbm.at[0], wout_ref, stage_out_ref, sem_ref)
        _stage_gate_blocks(wa_hbm.at[0], wx_hbm.at[0], wax_ref, stage_blk_ref, sem_ref)

    def step(set_a, set_b, do_a, do_b):
        hin_a, gate_a, xp_a = set_a
        hin_b, gate_b, xp_b = set_b
        if do_a:
            hin_a[...] = h_ref[...]
            for r in range(rows // ROW_CHUNK):
                rs = slice(r * ROW_CHUNK, (r + 1) * ROW_CHUNK)
                hb_ref[rs, :] = (_rms(h_ref[rs, :]) * gain_ref[...]).astype(BF16)

        ris = []
        slab = 2 * d_rnn // nblock
        for n in range(nblock):
            c0 = n * LRU_BLOCK
            if do_b:
                cw = convw_ref[:, c0:c0 + LRU_BLOCK]
                for r in range(rows // ROW_CHUNK):
                    lo = hist + r * ROW_CHUNK
                    xr = (xp_b[lo:lo + ROW_CHUNK, c0:c0 + LRU_BLOCK] * cw[LRU_CONV - 1:LRU_CONV, :]
                          + convb_ref[:, c0:c0 + LRU_BLOCK])
                    for j in range(LRU_CONV - 1):
                        back = (LRU_CONV - 1 - j) * SUBLANES
                        xr = xr + xp_b[lo - back:lo - back + ROW_CHUNK, c0:c0 + LRU_BLOCK] * cw[j:j + 1, :]
                    xr_ref[r * ROW_CHUNK:(r + 1) * ROW_CHUNK, c0:c0 + LRU_BLOCK] = xr
                ris.append(_dot(xr_ref[:, c0:c0 + LRU_BLOCK].astype(BF16), wax_ref[n]))
            if do_a:
                p0 = n * slab
                proj = _dot(hb_ref[...], win_ref[:, p0:p0 + slab])
                if p0 < d_rnn:
                    gate_a[:, p0:p0 + slab] = proj
                else:
                    xp_a[hist:hist + rows, p0 - d_rnn:p0 - d_rnn + slab] = proj
        if not do_b:
            return
        xp_a[0:hist, :] = xp_b[rows:rows + hist, :]

        sp = _softplus(-lam_ref[...]) * (-LRU_C)
        for n in range(nblock):
            c0 = n * LRU_BLOCK
            ri = ris[n]
            r = _sigmoid(ri[:, :LRU_BLOCK] + ba_ref[:, c0:c0 + LRU_BLOCK])
            i = _sigmoid(ri[:, LRU_BLOCK:] + bx_ref[:, c0:c0 + LRU_BLOCK])
            a = jnp.exp(r * sp[:, c0:c0 + LRU_BLOCK])
            a_ref[:, c0:c0 + LRU_BLOCK] = a
            x = 1.0 - a * a
            root = jnp.where(x > 0.0, x * lax.rsqrt(x), 0.0)
            u_ref[:, c0:c0 + LRU_BLOCK] = root * (i * xr_ref[:, c0:c0 + LRU_BLOCK])

        state = state_ref[...]
        for t in range(steps):
            sl = slice(t * SUBLANES, (t + 1) * SUBLANES)
            state = a_ref[sl, :] * state + u_ref[sl, :]
            hs_ref[sl, :] = state
        state_ref[...] = state

        y = (_gelu_tanh(gate_b[...]) * hs_ref[...]).astype(BF16)
        out_ref[...] = hin_b[...] + _dot(y, wout_ref[...])

    _run_pipeline(s, nsteps, (hin0_ref, gate0_ref, xp0_ref), (hin1_ref, gate1_ref, xp1_ref), step)


def _ffn_kernel(h_ref, gain_ref, wup_hbm, convw_ref, convb_ref, wdown_hbm, fgain_ref, out_ref,
                halo_ref, slab_ref, hin0_ref, hin1_ref, hb0_ref, hb1_ref, wup_ref, wdown_ref,
                stage_up_ref, stage_down_ref, sem_ref,
                *, layer, nsteps, d_ff, col_chunk, batch_major_in, final):
    s = pl.program_id(0)
    halo = (FFN_CONV - 1) * SUBLANES
    rows = hin0_ref.shape[0]

    @pl.when(s == 0)
    def _():
        halo_ref[...] = jnp.zeros(halo_ref.shape, F32)
        _stage_bf16(wup_hbm.at[layer], wup_ref, stage_up_ref, sem_ref)
        _stage_bf16(wdown_hbm.at[layer], wdown_ref, stage_down_ref, sem_ref)

    def stage_a(hin_a, hb_a):
        if batch_major_in:
            steps = h_ref.shape[1]
            nslab = h_ref.shape[2] // LANES
            for i in range(nslab):
                for b in range(SUBLANES):
                    slab_ref[i, pl.ds(b, steps, stride=SUBLANES), :] = h_ref[b, :, i * LANES:(i + 1) * LANES]
            for i in range(nslab):
                hin_a[:, i * LANES:(i + 1) * LANES] = slab_ref[i]
        else:
            hin_a[...] = h_ref[...]
        for r in range(rows // ROW_CHUNK):
            rs = slice(r * ROW_CHUNK, (r + 1) * ROW_CHUNK)
            hb_a[rs, :] = (_rms(hin_a[rs, :]) * gain_ref[...]).astype(BF16)

    def stage_b(hin_b, hb_b):
        def conv(cur, c0):
            prev = halo_ref[:, c0:c0 + col_chunk]
            cw = convw_ref[:, c0:c0 + col_chunk]
            out = cur * cw[FFN_CONV - 1:FFN_CONV, :] + convb_ref[:, c0:c0 + col_chunk]
            for j in range(FFN_CONV - 1):
                back = (FFN_CONV - 1 - j) * SUBLANES
                shifted = jnp.concatenate([prev[halo - back:, :], cur[:rows - back, :]], axis=0)
                out = out + shifted * cw[j:j + 1, :]
            halo_ref[:, c0:c0 + col_chunk] = cur[rows - halo:, :]
            return out

        def up(j):
            g0 = j * col_chunk
            v0 = d_ff + j * col_chunk
            return _dot(hb_b[...], wup_ref[:, g0:g0 + col_chunk]), _dot(hb_b[...], wup_ref[:, v0:v0 + col_chunk])

        nchunk = d_ff // col_chunk
        acc = jnp.zeros((rows, hin_b.shape[1]), F32)
        queue = [up(j) for j in range(min(UP_AHEAD, nchunk))]
        for j in range(nchunk):
            if j + UP_AHEAD < nchunk:
                queue.append(up(j + UP_AHEAD))
            cur = queue.pop(0)
            g0 = j * col_chunk
            gate = conv(cur[0], g0)
            val = conv(cur[1], d_ff + g0)
            act = (_silu(gate) * val).astype(BF16)
            acc = acc + _dot(act, wdown_ref[g0:g0 + col_chunk, :])
        res = hin_b[...] + acc
        if not final:
            out_ref[...] = res
        else:
            fin = _rms(res) * fgain_ref[...]
            steps = rows // SUBLANES
            for i in range(fin.shape[1] // LANES):
                slab_ref[i] = fin[:, i * LANES:(i + 1) * LANES]
            for i in range(fin.shape[1] // LANES):
                for b in range(SUBLANES):
                    out_ref[b, :, i * LANES:(i + 1) * LANES] = slab_ref[i, pl.ds(b, steps, stride=SUBLANES), :]

    def step(set_a, set_b, do_a, do_b):
        if do_a:
            stage_a(*set_a)
        if do_b:
            stage_b(*set_b)

    _run_pipeline(s, nsteps, (hin0_ref, hb0_ref), (hin1_ref, hb1_ref), step)


def _const_spec(shape):
    zeros = (0,) * len(shape)
    return pl.BlockSpec(shape, lambda *_: zeros)


def _params(sem, flags=None):
    return pltpu.CompilerParams(dimension_semantics=sem, vmem_limit_bytes=VMEM_LIMIT, flags=flags)


def _rope_tables(t):
    half = HEAD_DIM // 2
    inv_freq = ROPE_BASE ** (-jnp.arange(half, dtype=F32) / half)
    ang = jnp.arange(t, dtype=F32)[:, None] * inv_freq[None, :]
    cos = jnp.cos(ang)
    sin = jnp.sin(ang)
    return jnp.concatenate([cos, cos], axis=-1), jnp.concatenate([-sin, sin], axis=-1)


def _layer0_inproj(x, gain, w_in_all, conv_w, a_log, dt_bias, tile):
    b, t, d = x.shape
    width = HEADS * HEAD_DIM
    wsmall = jnp.zeros((d, LANES), F32).at[:, :2 * HEADS].set(w_in_all[0, :, 8 * width:]).astype(BF16)
    alog = jnp.zeros((1, LANES), F32).at[0, HEADS:2 * HEADS].set(a_log)
    dtb = jnp.zeros((1, LANES), F32).at[0, HEADS:2 * HEADS].set(dt_bias)
    cos, sin = _rope_tables(t)
    tiles_per_seq = t // tile
    ntiles = b * tiles_per_seq

    def in_tile(s):
        s = jnp.minimum(s, ntiles - 1)
        return s // tiles_per_seq, s % tiles_per_seq

    def out_tile(s):
        s = jnp.maximum(s - 1, 0)
        return s // tiles_per_seq, s % tiles_per_seq

    out_map = lambda s: (*out_tile(s), 0)
    pbuf = pltpu.VMEM((SUBLANES + tile, 8 * width + LANES), F32)
    return pl.pallas_call(
        functools.partial(_inproj_kernel, tiles_per_seq=tiles_per_seq, nsteps=ntiles + 1),
        grid=(ntiles + 1,),
        in_specs=[
            pl.BlockSpec((1, tile, d), lambda s: (*in_tile(s), 0)),
            _const_spec((1, d)),
            pl.BlockSpec(memory_space=pl.ANY),
            _const_spec((d, LANES)),
            pl.BlockSpec((tile, HEAD_DIM), lambda s: (out_tile(s)[1], 0)),
            pl.BlockSpec((tile, HEAD_DIM), lambda s: (out_tile(s)[1], 0)),
            _const_spec((GDN_CONV, 3 * width)),
            _const_spec((1, LANES)),
            _const_spec((1, LANES)),
        ],
        out_specs=[
            pl.BlockSpec((1, tile, 4 * width), out_map),
            pl.BlockSpec((1, tile, 4 * width), out_map),
            pl.BlockSpec((1, tile, LANES), out_map),
        ],
        out_shape=[
            jax.ShapeDtypeStruct((b, t, 4 * width), BF16),
            jax.ShapeDtypeStruct((b, t, 4 * width), BF16),
            jax.ShapeDtypeStruct((b, t, LANES), F32),
        ],
        scratch_shapes=[pbuf, pbuf, pltpu.VMEM((tile, d), BF16),
                        pltpu.VMEM((d, 8 * width), BF16),
                        _stage_scratch(d, 8 * width),
                        pltpu.SemaphoreType.DMA((2,))],
        compiler_params=_params(("arbitrary",)),
        name="l0_inproj",
    )(x, gain.reshape(1, d), w_in_all, wsmall, cos, sin, conv_w, alog, dtb)


def _layer0_mixer(ret, gdn, small, x, w_out_all, out_gain, tile):
    b, t, d = x.shape
    width = HEADS * HEAD_DIM
    return pl.pallas_call(
        _mixer_kernel,
        grid=(b, t // tile),
        in_specs=[
            pl.BlockSpec((1, tile, 4 * width), lambda i, j: (i, j, 0)),
            pl.BlockSpec((1, tile, 4 * width), lambda i, j: (i, j, 0)),
            pl.BlockSpec((1, tile, LANES), lambda i, j: (i, j, 0)),
            pl.BlockSpec((1, tile, d), lambda i, j: (i, j, 0)),
            pl.BlockSpec(memory_space=pl.ANY),
            _const_spec((1, HEAD_DIM)),
        ],
        out_specs=pl.BlockSpec((1, tile, d), lambda i, j: (i, j, 0)),
        out_shape=jax.ShapeDtypeStruct((b, t, d), F32),
        scratch_shapes=[
            pltpu.VMEM((HEADS, HEAD_DIM, HEAD_DIM), F32),
            pltpu.VMEM((HEADS, HEAD_DIM, HEAD_DIM), F32),
            pltpu.VMEM((tile, 2 * width), BF16),
            pltpu.VMEM((2 * width, d), BF16),
            _stage_scratch(2 * width, d),
            pltpu.SemaphoreType.DMA((2,)),
        ],
        compiler_params=_params(("arbitrary", "arbitrary")),
        name="l0_mixer",
    )(ret, gdn, small, x, w_out_all, out_gain.reshape(1, HEAD_DIM))


def _rglru(h_tm, gain, w_in_all, conv_w, conv_b, w_a_all, b_a, w_x_all, b_x, lam, w_out_all, steps):
    n, d = h_tm.shape
    d_rnn = lam.shape[0]
    rows = steps * SUBLANES
    ntiles = n // rows
    nblock = d_rnn // LRU_BLOCK
    hist = (LRU_CONV - 1) * SUBLANES
    any_spec = pl.BlockSpec(memory_space=pl.ANY)
    return pl.pallas_call(
        functools.partial(_rglru_kernel, nsteps=ntiles + 1),
        grid=(ntiles + 1,),
        in_specs=[
            pl.BlockSpec((rows, d), lambda s: (jnp.minimum(s, ntiles - 1), 0)),
            _const_spec((1, d)),
            any_spec,
            _const_spec((LRU_CONV, d_rnn)),
            _const_spec((1, d_rnn)),
            any_spec,
            any_spec,
            _const_spec((1, d_rnn)),
            _const_spec((1, d_rnn)),
            _const_spec((1, d_rnn)),
            any_spec,
        ],
        out_specs=pl.BlockSpec((rows, d), lambda s: (jnp.maximum(s - 1, 0), 0)),
        out_shape=jax.ShapeDtypeStruct((n, d), F32),
        scratch_shapes=[
            pltpu.VMEM((SUBLANES, d_rnn), F32),
            pltpu.VMEM((rows, d_rnn), F32),
            pltpu.VMEM((rows, d_rnn), F32),
            pltpu.VMEM((rows, d_rnn), F32),
            pltpu.VMEM((rows, d_rnn), F32),
            pltpu.VMEM((rows, d), BF16),
            pltpu.VMEM((rows, d), F32), pltpu.VMEM((rows, d), F32),
            pltpu.VMEM((rows, d_rnn), F32), pltpu.VMEM((rows, d_rnn), F32),
            pltpu.VMEM((hist + rows, d_rnn), F32), pltpu.VMEM((hist + rows, d_rnn), F32),
            pltpu.VMEM((d, 2 * d_rnn), BF16),
            pltpu.VMEM((d_rnn, d), BF16),
            pltpu.VMEM((nblock, LRU_BLOCK, 2 * LRU_BLOCK), BF16),
            _stage_scratch(d, 2 * d_rnn),
            _stage_scratch(d_rnn, d),
            pltpu.VMEM((2, nblock, LRU_BLOCK, LRU_BLOCK), F32),
            pltpu.SemaphoreType.DMA((2,)),
        ],
        compiler_params=_params(("arbitrary",)),
        name="l1_rglru",
    )(h_tm, gain.reshape(1, d), w_in_all, conv_w, conv_b.reshape(1, d_rnn), w_a_all, w_x_all,
      b_a.reshape(1, d_rnn), b_x.reshape(1, d_rnn), lam.reshape(1, d_rnn), w_out_all)


def _ffn(h, gain, w_up_all, conv_w, conv_b, w_down_all, final_gain, layer, steps, batch_major_in, final):
    d = h.shape[-1]
    n = h.size // d
    d_ff = w_down_all.shape[1]
    rows = steps * SUBLANES
    col_chunk = 2 * LANES
    ntiles = n // rows
    kern = functools.partial(_ffn_kernel, layer=layer, nsteps=ntiles + 1, d_ff=d_ff, col_chunk=col_chunk,
                             batch_major_in=batch_major_in, final=final)
    scratch = [pltpu.VMEM(((FFN_CONV - 1) * SUBLANES, 2 * d_ff), F32),
               pltpu.VMEM((d // LANES, rows, LANES), F32),
               pltpu.VMEM((rows, d), F32), pltpu.VMEM((rows, d), F32),
               pltpu.VMEM((rows, d), BF16), pltpu.VMEM((rows, d), BF16),
               pltpu.VMEM((d, 2 * d_ff), BF16),
               pltpu.VMEM((d_ff, d), BF16),
               _stage_scratch(d, 2 * d_ff),
               _stage_scratch(d_ff, d),
               pltpu.SemaphoreType.DMA((2,))]
    in_tile = lambda s: jnp.minimum(s, ntiles - 1)
    out_tile = lambda s: jnp.maximum(s - 1, 0)
    bm_spec = lambda tile_of: pl.BlockSpec((SUBLANES, steps, d), lambda s: (0, tile_of(s), 0))
    tm_spec = lambda tile_of: pl.BlockSpec((rows, d), lambda s: (tile_of(s), 0))
    if final:
        out_spec = bm_spec(out_tile)
        out_shape = jax.ShapeDtypeStruct((SUBLANES, n // SUBLANES, d), F32)
    else:
        out_spec = tm_spec(out_tile)
        out_shape = jax.ShapeDtypeStruct((n, d), F32)
    return pl.pallas_call(
        kern,
        grid=(ntiles + 1,),
        in_specs=[
            bm_spec(in_tile) if batch_major_in else tm_spec(in_tile),
            _const_spec((1, d)),
            pl.BlockSpec(memory_space=pl.ANY),
            _const_spec((FFN_CONV, 2 * d_ff)),
            _const_spec((1, 2 * d_ff)),
            pl.BlockSpec(memory_space=pl.ANY),
            _const_spec((1, d)),
        ],
        out_specs=out_spec,
        out_shape=out_shape,
        scratch_shapes=scratch,
        compiler_params=_params(("arbitrary",)),
        name="ffn_final" if final else "ffn",
    )(h, gain.reshape(1, d), w_up_all, conv_w, conv_b.reshape(1, 2 * d_ff),
      w_down_all, final_gain.reshape(1, d))


def kernel(x, norm_mix, norm_ffn, ret_gdn_w_in, gdn_conv_w, gdn_a_log, gdn_dt_bias, gdn_out_gain,
           ret_gdn_w_out, lru_w_in, lru_conv_w, lru_conv_b, lru_w_a, lru_b_a, lru_w_x, lru_b_x,
           lru_lambda, lru_w_out, ffn_w_up, ffn_conv_w, ffn_conv_b, ffn_w_down, norm_final):
    b, t, d = x.shape
    assert b == SUBLANES and norm_mix.shape[0] == 2
    tile_in = 512
    tile_mix = 512
    steps = 32
    steps_lru = 64

    ret, gdn, small = _layer0_inproj(x, norm_mix[0], ret_gdn_w_in, gdn_conv_w[0], gdn_a_log[0],
                                     gdn_dt_bias[0], tile_in)
    h = _layer0_mixer(ret, gdn, small, x, ret_gdn_w_out, gdn_out_gain[0], tile_mix)
    h = _ffn(h, norm_ffn[0], ffn_w_up, ffn_conv_w[0], ffn_conv_b[0], ffn_w_down, norm_final,
             0, steps, batch_major_in=True, final=False)
    h = _rglru(h, norm_mix[1], lru_w_in, lru_conv_w[0], lru_conv_b[0], lru_w_a, lru_b_a[0],
               lru_w_x, lru_b_x[0], lru_lambda[0], lru_w_out, steps_lru)
    return _ffn(h, norm_ffn[1], ffn_w_up, ffn_conv_w[1], ffn_conv_b[1], ffn_w_down, norm_final,
                1, steps, batch_major_in=False, final=True)
```
